```python
import jax, jax.numpy as jnp
from jax import lax
import numpy as np

D_MODEL = 1024
BATCH = 4
SEQ = 8192
DEPTH = 2

GRID_W = 64
HEAD_DIM = 64
MIX_WIDTH = D_MODEL // 2
ATTN_Q_HEADS = MIX_WIDTH // HEAD_DIM
ATTN_KV_HEADS = ATTN_Q_HEADS // 4
NAT_HEADS = MIX_WIDTH // HEAD_DIM
NAT_WIN_ROWS = 8
NAT_WIN_COLS = 16
SSM_GROUP = 16
SSM_GROUPS = MIX_WIDTH // SSM_GROUP
SSM_STATE = 64
N_BRANCHES = 3
FFN_DIM = 4 * D_MODEL
Q_BLOCK = 128
ROPE_THETA = 10000.0
LN_EPS = 1e-5
RMS_EPS = 1e-6
DEEPNORM_ALPHA = (2 * DEPTH) ** 0.25
DEEPNORM_BETA = (8 * DEPTH) ** -0.25
IN_SPLITS = (ATTN_Q_HEADS * HEAD_DIM, ATTN_KV_HEADS * HEAD_DIM, ATTN_KV_HEADS * HEAD_DIM,
             NAT_HEADS * HEAD_DIM, NAT_HEADS * HEAD_DIM, NAT_HEADS * HEAD_DIM,
             MIX_WIDTH, N_BRANCHES * D_MODEL)
IN_WIDTH = sum(IN_SPLITS)

kernel_name = 'hybrid_gated_gqa_nat_s5_encoder'


def _layer_norm(x, gain, bias):
    xf = x.astype(jnp.float32)
    mu = jnp.mean(xf, axis=-1, keepdims=True)
    xc = xf - mu
    var = jnp.mean(xc * xc, axis=-1, keepdims=True)
    y = xc * lax.rsqrt(var + LN_EPS) * gain.astype(jnp.float32) + bias.astype(jnp.float32)
    return y.astype(x.dtype)


def _rms_norm_f32(x, gain):
    xf = x.astype(jnp.float32)
    return xf * lax.rsqrt(jnp.mean(xf * xf, axis=-1, keepdims=True) + RMS_EPS) * gain.astype(jnp.float32)


def _axial_rope_tables(seq_len):
    t = jnp.arange(seq_len)
    row = (t // GRID_W).astype(jnp.float32)
    col = (t % GRID_W).astype(jnp.float32)
    axis_dim = HEAD_DIM // 2
    inv_freq = 1.0 / (ROPE_THETA ** (jnp.arange(0, axis_dim, 2, dtype=jnp.float32) / axis_dim))
    ang_r = row[:, None] * inv_freq[None, :]
    ang_c = col[:, None] * inv_freq[None, :]
    return (jnp.cos(ang_r), jnp.sin(ang_r), jnp.cos(ang_c), jnp.sin(ang_c))


def _rope_rotate(x, cos, sin):
    half = x.shape[-1] // 2
    x1 = x[..., :half]
    x2 = x[..., half:]
    c = cos[None, :, None, :]
    s = sin[None, :, None, :]
    return jnp.concatenate([x1 * c - x2 * s, x2 * c + x1 * s], axis=-1)


def _apply_axial_rope(x, rope):
    cos_r, sin_r, cos_c, sin_c = rope
    axis_dim = HEAD_DIM // 2
    return jnp.concatenate([_rope_rotate(x[..., :axis_dim], cos_r, sin_r),
                            _rope_rotate(x[..., axis_dim:], cos_c, sin_c)], axis=-1)


def _gqa_block_attention(q, k, v):
    bsz, seq_len, n_q, d = q.shape
    n_kv = k.shape[2]
    grp = n_q // n_kv
    n_blocks = seq_len // Q_BLOCK
    scale = d ** -0.5
    qb = jnp.moveaxis(q.reshape(bsz, n_blocks, Q_BLOCK, n_kv, grp, d), 1, 0)

    def block(qi):
        s = jnp.einsum('bqkgd,bskd->bkgqs', qi, k).astype(jnp.float32) * scale
        p = jax.nn.softmax(s, axis=-1).astype(v.dtype)
        return jnp.einsum('bkgqs,bskd->bqkgd', p, v)

    out = lax.map(block, qb)
    return jnp.moveaxis(out, 0, 1).reshape(bsz, seq_len, n_q * d)


def _neighbourhood_attention(q, k, v, rel_bias):
    bsz, seq_len, n_h, d = q.shape
    rows = seq_len // GRID_W
    kr = min(NAT_WIN_ROWS, rows)
    kc = NAT_WIN_COLS
    scale = d ** -0.5
    qg = q.reshape(bsz, rows, GRID_W, n_h, d)
    kg = k.reshape(bsz, rows, GRID_W, n_h, d)
    vg = v.reshape(bsz, rows, GRID_W, n_h, d)
    cols = jnp.arange(GRID_W)
    col_start = jnp.clip(cols - kc // 2, 0, GRID_W - kc)
    col_idx = col_start[:, None] + jnp.arange(kc)[None, :]
    col_off = col_idx - cols[:, None] + (NAT_WIN_COLS - 1)
    row_ids = jnp.arange(rows)
    row_start = jnp.clip(row_ids - kr // 2, 0, rows - kr)

    def one_row(args):
        r, rs = args
        qr = lax.dynamic_index_in_dim(qg, r, axis=1, keepdims=False)
        k_rows = lax.dynamic_slice_in_dim(kg, rs, kr, axis=1)
        v_rows = lax.dynamic_slice_in_dim(vg, rs, kr, axis=1)
        kw = k_rows[:, :, col_idx]
        vw = v_rows[:, :, col_idx]
        row_off = rs + jnp.arange(kr) - r + (NAT_WIN_ROWS - 1)
        bias = rel_bias[:, row_off[None, :, None], col_off[:, None, :]].astype(jnp.float32)
        s = jnp.einsum('bqhd,biqjhd->bhqij', qr, kw).astype(jnp.float32) * scale + bias[None]
        p = jax.nn.softmax(s.reshape(bsz, n_h, GRID_W, kr * kc), axis=-1).reshape(s.shape).astype(vw.dtype)
        return jnp.einsum('bhqij,biqjhd->bqhd', p, vw)

    out = lax.map(one_row, (row_ids, row_start))
    return jnp.moveaxis(out, 0, 1).reshape(bsz, seq_len, n_h * d)


def _complex_diag_combine(e1, e2):
    ar1, ai1, br1, bi1 = e1
    ar2, ai2, br2, bi2 = e2
    ar = ar2 * ar1 - ai2 * ai1
    ai = ar2 * ai1 + ai2 * ar1
    br = ar2 * br1 - ai2 * bi1 + br2
    bi = ar2 * bi1 + ai2 * br1 + bi2
    return (ar, ai, br, bi)


def _s5_scan_direction(ug, a_re, a_im, log_dt, b_re, b_im, c_re, c_im, reverse):
    seq_len = ug.shape[1]
    a_re = a_re.astype(jnp.float32)
    a_im = a_im.astype(jnp.float32)
    dt = jnp.exp(log_dt.astype(jnp.float32))[:, None]
    decay = jnp.exp(a_re * dt)
    phase = a_im * dt
    lam_re = decay * jnp.cos(phase)
    lam_im = decay * jnp.sin(phase)
    den = a_re * a_re + a_im * a_im
    num_re = lam_re - 1.0
    coef_re = (num_re * a_re + lam_im * a_im) / den
    coef_im = (lam_im * a_re - num_re * a_im) / den
    b_re = b_re.astype(jnp.float32)
    b_im = b_im.astype(jnp.float32)
    bbar_re = coef_re[..., None] * b_re - coef_im[..., None] * b_im
    bbar_im = coef_re[..., None] * b_im + coef_im[..., None] * b_re
    bu_re = jnp.einsum('blgh,gph->lbgp', ug, bbar_re)
    bu_im = jnp.einsum('blgh,gph->lbgp', ug, bbar_im)
    lam_re_seq = jnp.broadcast_to(lam_re[None, None], (seq_len, 1) + lam_re.shape)
    lam_im_seq = jnp.broadcast_to(lam_im[None, None], (seq_len, 1) + lam_im.shape)
    _, _, s_re, s_im = lax.associative_scan(_complex_diag_combine,
                                            (lam_re_seq, lam_im_seq, bu_re, bu_im),
                                            reverse=reverse, axis=0)
    return (jnp.einsum('lbgp,ghp->blgh', s_re, c_re.astype(jnp.float32))
            - jnp.einsum('lbgp,ghp->blgh', s_im, c_im.astype(jnp.float32)))


def _s5_bidirectional(u, a_re, a_im, log_dt, b_re, b_im, c_re, c_im, d_skip, w_glu):
    bsz, seq_len, _ = u.shape
    ug = u.reshape(bsz, seq_len, SSM_GROUPS, SSM_GROUP).astype(jnp.float32)
    y = d_skip.astype(jnp.float32) * ug
    for direction, reverse in ((0, False), (1, True)):
        y = y + _s5_scan_direction(ug, a_re[direction], a_im[direction], log_dt[direction],
                                   b_re, b_im, c_re[direction], c_im[direction], reverse)
    y = y.reshape(bsz, seq_len, MIX_WIDTH)
    z = jax.nn.gelu(y)
    out = z * jax.nn.sigmoid(z @ w_glu.astype(jnp.float32))
    return out.astype(u.dtype)


def _hybrid_mixer(h, w_in, q_norm_gain, k_norm_gain, nat_rel_bias, ssm_a_re, ssm_a_im, ssm_log_dt,
                  ssm_b_re, ssm_b_im, ssm_c_re, ssm_c_im, ssm_d, ssm_w_glu, w_branch, w_out, rope):
    bsz, seq_len, _ = h.shape
    proj = h @ w_in
    split_points = np.cumsum(IN_SPLITS)[:-1].tolist()
    aq, ak, av, nq, nk, nv, su, gate_pre = jnp.split(proj, split_points, axis=-1)
    aq = aq.reshape(bsz, seq_len, ATTN_Q_HEADS, HEAD_DIM)
    ak = ak.reshape(bsz, seq_len, ATTN_KV_HEADS, HEAD_DIM)
    av = av.reshape(bsz, seq_len, ATTN_KV_HEADS, HEAD_DIM)
    aq = _apply_axial_rope(_rms_norm_f32(aq, q_norm_gain), rope).astype(h.dtype)
    ak = _apply_axial_rope(_rms_norm_f32(ak, k_norm_gain), rope).astype(h.dtype)
    attn_o = _gqa_block_attention(aq, ak, av)
    nq = nq.reshape(bsz, seq_len, NAT_HEADS, HEAD_DIM)
    nk = nk.reshape(bsz, seq_len, NAT_HEADS, HEAD_DIM)
    nv = nv.reshape(bsz, seq_len, NAT_HEADS, HEAD_DIM)
    nat_o = _neighbourhood_attention(nq, nk, nv, nat_rel_bias)
    ssm_o = _s5_bidirectional(su, ssm_a_re, ssm_a_im, ssm_log_dt, ssm_b_re, ssm_b_im,
                              ssm_c_re, ssm_c_im, ssm_d, ssm_w_glu)
    branches = jnp.stack([attn_o, nat_o, ssm_o], axis=2)
    branch_d = jnp.einsum('blnm,nmd->blnd', branches, w_branch)
    gates = jax.nn.sigmoid(gate_pre.reshape(bsz, seq_len, N_BRANCHES, D_MODEL))
    merged = jnp.sum(gates * branch_d, axis=2)
    return merged @ w_out


def _squared_relu_mlp(h, w_up, w_down):
    return jnp.square(jax.nn.relu(h @ w_up)) @ w_down


def setup_inputs(seed: int = 0) -> dict:
    key = jax.random.key(seed)
    ks = jax.random.split(key, 24)
    f32 = jnp.float32

    def nrm(k, shape, scale):
        return jax.random.normal(k, shape, f32) * scale

    G, P, Hs = SSM_GROUPS, SSM_STATE, SSM_GROUP
    x = nrm(ks[0], (BATCH, SEQ, D_MODEL), 1.0)
    w_in = nrm(ks[1], (DEPTH, D_MODEL, IN_WIDTH), D_MODEL ** -0.5)
    q_norm_gain = 1.0 + nrm(ks[2], (DEPTH, HEAD_DIM), 0.02)
    k_norm_gain = 1.0 + nrm(ks[3], (DEPTH, HEAD_DIM), 0.02)
    nat_rel_bias = nrm(ks[4], (DEPTH, NAT_HEADS, 2 * NAT_WIN_ROWS - 1, 2 * NAT_WIN_COLS - 1), 0.1)
    ssm_a_re = -0.5 + nrm(ks[5], (DEPTH, 2, G, P), 0.01)
    ssm_a_im = jnp.pi * jnp.arange(P, dtype=f32) + nrm(ks[6], (DEPTH, 2, G, P), 0.01)
    ssm_log_dt = jax.random.uniform(ks[7], (DEPTH, 2, G), f32,
                                    minval=float(np.log(1e-3)), maxval=float(np.log(1e-1)))
    ssm_b_re = nrm(ks[8], (DEPTH, G, P, Hs), (2 * Hs) ** -0.5)
    ssm_b_im = nrm(ks[9], (DEPTH, G, P, Hs), (2 * Hs) ** -0.5)
    ssm_c_re = nrm(ks[10], (DEPTH, 2, G, Hs, P), P ** -0.5)
    ssm_c_im = nrm(ks[11], (DEPTH, 2, G, Hs, P), P ** -0.5)
    ssm_d = nrm(ks[12], (DEPTH, G, Hs), 1.0)
    ssm_w_glu = nrm(ks[13], (DEPTH, MIX_WIDTH, MIX_WIDTH), MIX_WIDTH ** -0.5)
    w_branch = nrm(ks[14], (DEPTH, N_BRANCHES, MIX_WIDTH, D_MODEL), MIX_WIDTH ** -0.5)
    w_out = nrm(ks[15], (DEPTH, D_MODEL, D_MODEL), D_MODEL ** -0.5 * DEEPNORM_BETA)
    ln1_gain = 1.0 + nrm(ks[16], (DEPTH, D_MODEL), 0.02)
    ln1_bias = nrm(ks[17], (DEPTH, D_MODEL), 0.02)
    w_ffn_up = nrm(ks[18], (DEPTH, D_MODEL, FFN_DIM), D_MODEL ** -0.5)
    w_ffn_down = nrm(ks[19], (DEPTH, FFN_DIM, D_MODEL), FFN_DIM ** -0.5 * DEEPNORM_BETA)
    ln2_gain = 1.0 + nrm(ks[20], (DEPTH, D_MODEL), 0.02)
    ln2_bias = nrm(ks[21], (DEPTH, D_MODEL), 0.02)
    return {'x': x, 'w_in': w_in, 'q_norm_gain': q_norm_gain, 'k_norm_gain': k_norm_gain,
            'nat_rel_bias': nat_rel_bias, 'ssm_a_re': ssm_a_re, 'ssm_a_im': ssm_a_im,
            'ssm_log_dt': ssm_log_dt, 'ssm_b_re': ssm_b_re, 'ssm_b_im': ssm_b_im,
            'ssm_c_re': ssm_c_re, 'ssm_c_im': ssm_c_im, 'ssm_d': ssm_d, 'ssm_w_glu': ssm_w_glu,
            'w_branch': w_branch, 'w_out': w_out, 'ln1_gain': ln1_gain, 'ln1_bias': ln1_bias,
            'w_ffn_up': w_ffn_up, 'w_ffn_down': w_ffn_down, 'ln2_gain': ln2_gain, 'ln2_bias': ln2_bias}


def reference(x, w_in, q_norm_gain, k_norm_gain, nat_rel_bias, ssm_a_re, ssm_a_im, ssm_log_dt,
              ssm_b_re, ssm_b_im, ssm_c_re, ssm_c_im, ssm_d, ssm_w_glu, w_branch, w_out,
              ln1_gain, ln1_bias, w_ffn_up, w_ffn_down, ln2_gain, ln2_bias):
    rope = _axial_rope_tables(x.shape[1])
    for layer in range(DEPTH):
        mix = _hybrid_mixer(x, w_in[layer], q_norm_gain[layer], k_norm_gain[layer], nat_rel_bias[layer],
                            ssm_a_re[layer], ssm_a_im[layer], ssm_log_dt[layer], ssm_b_re[layer],
                            ssm_b_im[layer], ssm_c_re[layer], ssm_c_im[layer], ssm_d[layer],
                            ssm_w_glu[layer], w_branch[layer], w_out[layer], rope)
        x = _layer_norm(DEEPNORM_ALPHA * x + mix, ln1_gain[layer], ln1_bias[layer])
        ffn = _squared_relu_mlp(x, w_ffn_up[layer], w_ffn_down[layer])
        x = _layer_norm(DEEPNORM_ALPHA * x + ffn, ln2_gain[layer], ln2_bias[layer])
    return x
```

```python
import functools
import math

import jax
import jax.numpy as jnp
import numpy as np
from jax import lax
from jax.experimental import pallas as pl
from jax.experimental.pallas import tpu as pltpu

D_MODEL = 1024
DEPTH = 2
GRID_W = 64
HEAD_DIM = 64
MIX_WIDTH = 512
ATTN_Q_HEADS = 8
ATTN_KV_HEADS = 2
ATTN_GROUP = ATTN_Q_HEADS // ATTN_KV_HEADS
NAT_HEADS = 8
NAT_WIN_ROWS = 8
NAT_WIN_COLS = 16
SSM_GROUP = 16
SSM_GROUPS = 32
SSM_STATE = 64
SSM_CHUNK = 16
N_BRANCHES = 3
FFN_DIM = 4 * D_MODEL
ROPE_THETA = 10000.0
LN_EPS = 1e-5
RMS_EPS = 1e-6
DEEPNORM_ALPHA = (2 * DEPTH) ** 0.25
ATTN_SCALE = HEAD_DIM ** -0.5
MASK_VALUE = -1e30

QK_WIDTH = (ATTN_Q_HEADS + ATTN_KV_HEADS) * HEAD_DIM
MID_WIDTH = ATTN_KV_HEADS * HEAD_DIM + 3 * MIX_WIDTH + MIX_WIDTH
GATE_WIDTH = N_BRANCHES * D_MODEL

LANES = 128
SUBLANES = 8
V7X_VMEM_BYTES = 64 * 1024 * 1024
VMEM_LIMIT = V7X_VMEM_BYTES - 8 * 1024 * 1024

F32 = jnp.float32
BF16 = jnp.bfloat16


def _params(*semantics):
    return pltpu.CompilerParams(dimension_semantics=semantics, vmem_limit_bytes=VMEM_LIMIT)


def _dot(a, b):
    return jnp.dot(a, b, preferred_element_type=F32)


def _layer_norm(x, gain, bias):
    mu = jnp.mean(x, axis=-1, keepdims=True)
    xc = x - mu
    var = jnp.mean(xc * xc, axis=-1, keepdims=True)
    return xc * lax.rsqrt(var + LN_EPS) * gain + bias


def _qk_proj_kernel(x_ref, w_ref, gain_ref, cos_ref, sin_ref, seg_ref, q_ref, k_ref):
    tm = x_ref.shape[0]
    y = _dot(x_ref[...].astype(BF16), w_ref[...])
    y2 = y * y
    hi = y2.astype(BF16)
    lo = (y2 - hi.astype(F32)).astype(BF16)
    ms = (_dot(hi, seg_ref[...]) + _dot(lo, seg_ref[...])) * (1.0 / HEAD_DIM)
    yn = y * lax.rsqrt(ms + RMS_EPS) * gain_ref[...]
    cos = cos_ref[...]
    sin = sin_ref[...]
    lane = lax.broadcasted_iota(jnp.int32, (tm, LANES), 1)
    first_half = (lane % 32) < 16
    outs = []
    for s in range(QK_WIDTH // LANES):
        xs = yn[:, LANES * s:LANES * (s + 1)]
        partner = jnp.where(first_half, pltpu.roll(xs, LANES - 16, 1), pltpu.roll(xs, 16, 1))
        outs.append(xs * cos + partner * sin)
    nq = ATTN_Q_HEADS * HEAD_DIM // LANES
    q_ref[...] = (jnp.concatenate(outs[:nq], axis=1) * ATTN_SCALE).astype(BF16)
    k_ref[...] = jnp.concatenate(outs[nq:], axis=1).astype(BF16)


def _mid_proj_kernel(x_ref, w_ref, av_ref, nq_ref, nk_ref, nv_ref, su_ref):
    y = _dot(x_ref[...].astype(BF16), w_ref[...])
    o = ATTN_KV_HEADS * HEAD_DIM
    av_ref[...] = y[:, :o].astype(BF16)
    nq_ref[...] = (y[:, o:o + MIX_WIDTH] * ATTN_SCALE).astype(BF16)
    nk_ref[...] = y[:, o + MIX_WIDTH:o + 2 * MIX_WIDTH].astype(BF16)
    nv_ref[...] = y[:, o + 2 * MIX_WIDTH:o + 3 * MIX_WIDTH].astype(BF16)
    su_ref[...] = y[:, o + 3 * MIX_WIDTH:].astype(BF16)


def _gate_proj_kernel(x_ref, w_ref, g_ref):
    y = _dot(x_ref[...].astype(BF16), w_ref[...])
    g_ref[...] = jax.nn.sigmoid(y).astype(BF16)


def _rope_tables(seq_len):
    t = jnp.arange(seq_len)
    row = (t // GRID_W).astype(F32)
    col = (t % GRID_W).astype(F32)
    axis_dim = HEAD_DIM // 2
    inv_freq = 1.0 / (ROPE_THETA ** (jnp.arange(0, axis_dim, 2, dtype=F32) / axis_dim))
    ang_r = row[:, None] * inv_freq[None, :]
    ang_c = col[:, None] * inv_freq[None, :]
    cos_head = jnp.concatenate([jnp.cos(ang_r), jnp.cos(ang_r), jnp.cos(ang_c), jnp.cos(ang_c)], axis=1)
    sin_head = jnp.concatenate([-jnp.sin(ang_r), jnp.sin(ang_r), -jnp.sin(ang_c), jnp.sin(ang_c)], axis=1)
    reps = LANES // HEAD_DIM
    return jnp.tile(cos_head, (1, reps)), jnp.tile(sin_head, (1, reps))


def _input_projections(x2d, w_in, q_gain, k_gain, rope, seq_len, tm):
    m = x2d.shape[0]
    nl = seq_len // tm
    w = w_in.astype(BF16)
    w_qk = w[:, :QK_WIDTH]
    w_mid = w[:, QK_WIDTH:QK_WIDTH + MID_WIDTH]
    w_gate = w[:, QK_WIDTH + MID_WIDTH:]
    gain = jnp.concatenate([jnp.tile(q_gain.astype(F32), ATTN_Q_HEADS),
                            jnp.tile(k_gain.astype(F32), ATTN_KV_HEADS)])[None, :]
    seg = (jnp.arange(QK_WIDTH)[:, None] // HEAD_DIM == jnp.arange(QK_WIDTH)[None, :] // HEAD_DIM).astype(BF16)
    cos, sin = rope
    row = lambda width: pl.BlockSpec((tm, width), lambda i: (i, 0))
    whole = lambda a: pl.BlockSpec(a.shape, lambda i: (0,) * a.ndim)
    x_spec = row(D_MODEL)
    q, k = pl.pallas_call(
        _qk_proj_kernel,
        grid=(m // tm,),
        in_specs=[x_spec, whole(w_qk), whole(gain),
                  pl.BlockSpec((tm, LANES), lambda i: (i % nl, 0)),
                  pl.BlockSpec((tm, LANES), lambda i: (i % nl, 0)),
                  whole(seg)],
        out_specs=[row(ATTN_Q_HEADS * HEAD_DIM), row(ATTN_KV_HEADS * HEAD_DIM)],
        out_shape=[jax.ShapeDtypeStruct((m, ATTN_Q_HEADS * HEAD_DIM), BF16),
                   jax.ShapeDtypeStruct((m, ATTN_KV_HEADS * HEAD_DIM), BF16)],
        compiler_params=_params("parallel"),
        name="qk_proj",
    )(x2d, w_qk, gain, cos, sin, seg)
    av, nq, nk, nv, su = pl.pallas_call(
        _mid_proj_kernel,
        grid=(m // tm,),
        in_specs=[x_spec, whole(w_mid)],
        out_specs=[row(ATTN_KV_HEADS * HEAD_DIM)] + [row(MIX_WIDTH)] * 4,
        out_shape=[jax.ShapeDtypeStruct((m, ATTN_KV_HEADS * HEAD_DIM), BF16)]
        + [jax.ShapeDtypeStruct((m, MIX_WIDTH), BF16)] * 4,
        compiler_params=_params("parallel"),
        name="mid_proj",
    )(x2d, w_mid)
    gates = pl.pallas_call(
        _gate_proj_kernel,
        grid=(m // tm,),
        in_specs=[x_spec, whole(w_gate)],
        out_specs=row(GATE_WIDTH),
        out_shape=jax.ShapeDtypeStruct((m, GATE_WIDTH), BF16),
        compiler_params=_params("parallel"),
        name="gate_proj",
    )(x2d, w_gate)
    return q, k, av, nq, nk, nv, su, gates


def _flash_kernel(q_ref, kt_ref, v_ref, o_ref, m_scr, l_scr, acc_scr):
    kv = pl.program_id(3)

    @pl.when(kv == 0)
    def _():
        m_scr[...] = jnp.full_like(m_scr, -jnp.inf)
        l_scr[...] = jnp.zeros_like(l_scr)
        acc_scr[...] = jnp.zeros_like(acc_scr)

    s = _dot(q_ref[...], kt_ref[...])
    m_prev = m_scr[...]
    m_new = jnp.maximum(m_prev, jnp.max(s, axis=1, keepdims=True))
    alpha = jnp.exp(m_prev - m_new)
    p = jnp.exp(s - m_new)
    l_scr[...] = alpha * l_scr[...] + jnp.sum(p, axis=1, keepdims=True)
    acc_scr[...] = alpha * acc_scr[...] + _dot(p.astype(BF16), v_ref[...])
    m_scr[...] = m_new

    @pl.when(kv == pl.num_programs(3) - 1)
    def _():
        o_ref[...] = (acc_scr[...] / l_scr[...]).astype(o_ref.dtype)


def _gqa_attention(q, k, v, bsz, seq_len, tq, tk):
    d, hk, grp = HEAD_DIM, ATTN_KV_HEADS, ATTN_GROUP
    nq, nk = seq_len // tq, seq_len // tk
    q5 = q.reshape(bsz, nq, tq, hk, grp, d).transpose(0, 3, 1, 4, 2, 5).reshape(bsz, hk, nq, grp * tq, d)
    kt = k.reshape(bsz, seq_len, hk, d).transpose(0, 2, 3, 1)
    vh = v.reshape(bsz, seq_len, hk, d).transpose(0, 2, 1, 3)
    rows = grp * tq
    o5 = pl.pallas_call(
        _flash_kernel,
        grid=(bsz, hk, nq, nk),
        in_specs=[pl.BlockSpec((None, None, None, rows, d), lambda b, j, i, kv: (b, j, i, 0, 0)),
                  pl.BlockSpec((None, None, d, tk), lambda b, j, i, kv: (b, j, 0, kv)),
                  pl.BlockSpec((None, None, tk, d), lambda b, j, i, kv: (b, j, kv, 0))],
        out_specs=pl.BlockSpec((None, None, None, rows, d), lambda b, j, i, kv: (b, j, i, 0, 0)),
        out_shape=jax.ShapeDtypeStruct((bsz, hk, nq, rows, d), BF16),
        scratch_shapes=[pltpu.VMEM((rows, 1), F32), pltpu.VMEM((rows, 1), F32), pltpu.VMEM((rows, d), F32)],
        compiler_params=_params("parallel", "parallel", "parallel", "arbitrary"),
        name="gqa_flash",
    )(q5, kt, vh)
    o = o5.reshape(bsz, hk, nq, grp, tq, d).transpose(0, 2, 4, 1, 3, 5)
    return o.reshape(bsz * seq_len, ATTN_Q_HEADS * d)


NAT_ROWS_PER_STEP = 8
NAT_STEP_TOKENS = NAT_ROWS_PER_STEP * GRID_W
NAT_WIN_TOKENS = NAT_WIN_ROWS * GRID_W


def _nat_kernel(q_ref, ktp_ref, ktc_ref, ktn_ref, vp_ref, vc_ref, vn_ref, bias_ref, o_ref, kt_scr, v_scr):
    g = pl.program_id(2)
    ng = pl.num_programs(2)
    st = NAT_STEP_TOKENS
    kt_scr[:, 0:st] = ktp_ref[...]
    kt_scr[:, st:2 * st] = ktc_ref[...]
    kt_scr[:, 2 * st:3 * st] = ktn_ref[...]
    v_scr[0:st, :] = vp_ref[...]
    v_scr[st:2 * st, :] = vc_ref[...]
    v_scr[2 * st:3 * st, :] = vn_ref[...]

    def rows(frame_rows, bias_index):
        for r in range(NAT_ROWS_PER_STEP):
            q = q_ref[GRID_W * r:GRID_W * (r + 1), :]
            start = frame_rows[r] * GRID_W
            s = _dot(q, kt_scr[:, start:start + NAT_WIN_TOKENS]) + bias_ref[bias_index[r]]
            m = jnp.max(s, axis=1, keepdims=True)
            p = jnp.exp(s - m)
            l = jnp.sum(p, axis=1, keepdims=True)
            o = _dot(p.astype(BF16), v_scr[start:start + NAT_WIN_TOKENS, :])
            o_ref[GRID_W * r:GRID_W * (r + 1), :] = (o / l).astype(o_ref.dtype)

    half = NAT_WIN_ROWS // 2
    n = NAT_ROWS_PER_STEP

    @pl.when(g == 0)
    def _():
        rows([n + max(r - half, 0) for r in range(n)], [min(r, half) for r in range(n)])

    @pl.when(jnp.logical_and(g > 0, g < ng - 1))
    def _():
        rows([r + half for r in range(n)], [half] * n)

    @pl.when(jnp.logical_and(g == ng - 1, g > 0))
    def _():
        rows([min(r + half, n) for r in range(n)], [max(r, half) for r in range(n)])


def _nat_bias_table(rel_bias):
    cols = jnp.arange(GRID_W)
    col_start = jnp.clip(cols - NAT_WIN_COLS // 2, 0, GRID_W - NAT_WIN_COLS)
    kc = jnp.arange(GRID_W)
    in_win = (kc[None, :] >= col_start[:, None]) & (kc[None, :] < col_start[:, None] + NAT_WIN_COLS)
    col_off = jnp.clip(kc[None, :] - cols[:, None] + (NAT_WIN_COLS - 1), 0, 2 * NAT_WIN_COLS - 2)
    v = jnp.arange(NAT_WIN_ROWS)
    i = jnp.arange(NAT_WIN_ROWS)
    row_off = i[None, :] - v[:, None] + (NAT_WIN_ROWS - 1)
    b = rel_bias.astype(F32)[:, row_off[:, :, None, None], col_off[None, None, :, :]]
    b = jnp.where(in_win[None, None, None], b, MASK_VALUE)
    b = b.transpose(1, 0, 3, 2, 4)
    return b.reshape(NAT_WIN_ROWS, NAT_HEADS, GRID_W, NAT_WIN_TOKENS)


def _neighbourhood_attention(nq, nk, nv, rel_bias, bsz, seq_len):
    h, d, st = NAT_HEADS, HEAD_DIM, NAT_STEP_TOKENS
    ng = seq_len // st
    assert ng >= 2 and seq_len // GRID_W >= NAT_WIN_ROWS
    qh = nq.reshape(bsz, seq_len, h, d).transpose(0, 2, 1, 3)
    kt = nk.reshape(bsz, seq_len, h, d).transpose(0, 2, 3, 1)
    vh = nv.reshape(bsz, seq_len, h, d).transpose(0, 2, 1, 3)
    bias = _nat_bias_table(rel_bias)
    prev = lambda g: jnp.maximum(g - 1, 0)
    nxt = lambda g: jnp.minimum(g + 1, ng - 1)
    kt_spec = lambda f: pl.BlockSpec((None, None, d, st), lambda hh, b, g: (b, hh, 0, f(g)))
    v_spec = lambda f: pl.BlockSpec((None, None, st, d), lambda hh, b, g: (b, hh, f(g), 0))
    same = lambda g: g
    o = pl.pallas_call(
        _nat_kernel,
        grid=(h, bsz, ng),
        in_specs=[pl.BlockSpec((None, None, st, d), lambda hh, b, g: (b, hh, g, 0)),
                  kt_spec(prev), kt_spec(same), kt_spec(nxt),
                  v_spec(prev), v_spec(same), v_spec(nxt),
                  pl.BlockSpec((NAT_WIN_ROWS, None, GRID_W, NAT_WIN_TOKENS), lambda hh, b, g: (0, hh, 0, 0))],
        out_specs=pl.BlockSpec((None, None, st, d), lambda hh, b, g: (b, hh, g, 0)),
        out_shape=jax.ShapeDtypeStruct((bsz, h, seq_len, d), BF16),
        scratch_shapes=[pltpu.VMEM((d, 3 * st), BF16), pltpu.VMEM((3 * st, d), BF16)],
        compiler_params=_params("parallel", "parallel", "parallel"),
        name="nat",
    )(qh, kt, kt, kt, vh, vh, vh, bias)
    return o.transpose(0, 2, 1, 3).reshape(bsz * seq_len, h * d)


S5_TILE = SSM_CHUNK * SSM_GROUP
S5_STATE_COLS = 4 * SSM_STATE


def _s5_matrices(a_re, a_im, log_dt, b_re, b_im, c_re, c_im, d_skip):
    t_len, hs = SSM_CHUNK, SSM_GROUP
    a_re = a_re.astype(F32)
    a_im = a_im.astype(F32)
    dt = jnp.exp(log_dt.astype(F32))[..., None]
    decay = jnp.exp(a_re * dt)
    phase = a_im * dt
    lam_re = decay * jnp.cos(phase)
    lam_im = decay * jnp.sin(phase)
    den = a_re * a_re + a_im * a_im
    num_re = lam_re - 1.0
    coef_re = (num_re * a_re + lam_im * a_im) / den
    coef_im = (lam_im * a_re - num_re * a_im) / den
    b_re = b_re.astype(F32)[None]
    b_im = b_im.astype(F32)[None]
    bbar_re = coef_re[..., None] * b_re - coef_im[..., None] * b_im
    bbar_im = coef_re[..., None] * b_im + coef_im[..., None] * b_re
    k = jnp.arange(t_len + 1, dtype=F32)[:, None, None, None]
    pow_mag = jnp.exp(k * (a_re * dt)[None])
    pow_re = pow_mag * jnp.cos(k * phase[None])
    pow_im = pow_mag * jnp.sin(k * phase[None])
    c_re = c_re.astype(F32)
    c_im = c_im.astype(F32)

    cl_re = jnp.einsum('dghp,tdgp->tdghp', c_re, pow_re) - jnp.einsum('dghp,tdgp->tdghp', c_im, pow_im)
    cl_im = jnp.einsum('dghp,tdgp->tdghp', c_re, pow_im) + jnp.einsum('dghp,tdgp->tdghp', c_im, pow_re)
    taps = jnp.einsum('tdgop,dgpi->tdgoi', cl_re, bbar_re) - jnp.einsum('tdgop,dgpi->tdgoi', cl_im, bbar_im)
    taps = taps[:t_len]
    tk = jnp.arange(t_len)
    lag = tk[None, :] - tk[:, None]
    fwd = jnp.where((lag >= 0)[:, :, None, None, None], taps[jnp.clip(lag, 0, t_len - 1), 0], 0.0)
    rev = jnp.where((lag <= 0)[:, :, None, None, None], taps[jnp.clip(-lag, 0, t_len - 1), 1], 0.0)
    skip = jnp.where((lag == 0)[:, :, None, None, None],
                     (jnp.eye(hs, dtype=F32)[None] * d_skip.astype(F32)[:, :, None])[None, None], 0.0)
    toep = (fwd + rev + skip).transpose(2, 0, 4, 1, 3)
    toep = toep.reshape(SSM_GROUPS, S5_TILE, S5_TILE)

    def in_mat(pw_re, pw_im, d):
        re = pw_re[..., None] * bbar_re[d][None] - pw_im[..., None] * bbar_im[d][None]
        im = pw_re[..., None] * bbar_im[d][None] + pw_im[..., None] * bbar_re[d][None]
        to = lambda m: m.transpose(1, 0, 3, 2).reshape(SSM_GROUPS, S5_TILE, SSM_STATE)
        return to(re), to(im)
    f_re, f_im = in_mat(pow_re[:t_len, 0][::-1], pow_im[:t_len, 0][::-1], 0)
    r_re, r_im = in_mat(pow_re[:t_len, 1], pow_im[:t_len, 1], 1)
    b_mat = jnp.concatenate([f_re, r_re, f_im, r_im], axis=2)

    def out_mat(cre, cim):
        to = lambda m: m.transpose(1, 3, 0, 2).reshape(SSM_GROUPS, SSM_STATE, S5_TILE)
        return to(cre), to(-cim)
    mf_re, mf_im = out_mat(cl_re[1:t_len + 1, 0], cl_im[1:t_len + 1, 0])
    mr_re, mr_im = out_mat(cl_re[1:t_len + 1, 1][::-1], cl_im[1:t_len + 1, 1][::-1])
    m_mat = jnp.concatenate([mf_re, mr_re, mf_im, mr_im], axis=1)

    lam_t = jnp.concatenate([pow_re[t_len, 0], pow_re[t_len, 1], pow_im[t_len, 0], pow_im[t_len, 1]], axis=1)
    return toep.astype(BF16), b_mat.astype(BF16), m_mat.astype(BF16), lam_t.reshape(1, SSM_GROUPS * S5_STATE_COLS)


def _s5_state_in_kernel(u_ref, b_ref, z_ref):
    z_ref[...] = _dot(u_ref[...], b_ref[...])


def _s5_scan_kernel(z_ref, lam_ref, s_ref, *, n_chunks, bsz, groups):
    assert 2 * bsz == SUBLANES and n_chunks % 2 == 0
    n_tiles = n_chunks // 2
    lane = lax.broadcasted_iota(jnp.int32, (SUBLANES, LANES), 1)
    sub = lax.broadcasted_iota(jnp.int32, (SUBLANES, LANES), 0)
    is_fwd = lane < SSM_STATE
    is_rev = jnp.logical_not(is_fwd)
    first = sub < bsz
    swap = lambda a: pltpu.roll(a, bsz, 0)
    lam = [(jnp.broadcast_to(lam_ref[:, 256 * j:256 * j + LANES], (SUBLANES, LANES)),
            jnp.broadcast_to(lam_ref[:, 256 * j + LANES:256 * (j + 1)], (SUBLANES, LANES))) for j in range(groups)]

    def step(k, carry):
        rf = pl.multiple_of(k * SUBLANES, SUBLANES)
        rr = pl.multiple_of((n_tiles - 1 - k) * SUBLANES, SUBLANES)
        new = []
        for j in range(groups):
            c_re, c_im = carry[j]
            lr, li = lam[j]
            cre = slice(256 * j, 256 * j + LANES)
            cim = slice(256 * j + LANES, 256 * (j + 1))
            w_re = jnp.where(is_fwd, z_ref[pl.ds(rf, SUBLANES), cre], swap(z_ref[pl.ds(rr, SUBLANES), cre]))
            w_im = jnp.where(is_fwd, z_ref[pl.ds(rf, SUBLANES), cim], swap(z_ref[pl.ds(rr, SUBLANES), cim]))
            a_re = swap(lr * c_re - li * c_im + w_re)
            a_im = swap(lr * c_im + li * c_re + w_im)
            b_re = lr * a_re - li * a_im + w_re
            b_im = lr * a_im + li * a_re + w_im
            e_re = jnp.where(first, c_re, a_re)
            e_im = jnp.where(first, c_im, a_im)
            pltpu.store(s_ref.at[pl.ds(rf, SUBLANES), cre], e_re, mask=is_fwd)
            pltpu.store(s_ref.at[pl.ds(rf, SUBLANES), cim], e_im, mask=is_fwd)
            pltpu.store(s_ref.at[pl.ds(rr, SUBLANES), cre], swap(e_re), mask=is_rev)
            pltpu.store(s_ref.at[pl.ds(rr, SUBLANES), cim], swap(e_im), mask=is_rev)
            new.append((jnp.where(first, swap(b_re), b_re), jnp.where(first, swap(b_im), b_im)))
        return tuple(new)

    zero = jnp.zeros((SUBLANES, LANES), F32)
    lax.fori_loop(0, n_tiles, step, tuple((zero, zero) for _ in range(groups)))


def _s5_out_kernel(u_ref, s_ref, t_ref, m_ref, y_ref):
    y_ref[...] = _dot(u_ref[...], t_ref[...]) + _dot(s_ref[...].astype(BF16), m_ref[...])


def _s5_bidirectional(su, mats, bsz, seq_len):
    toep, b_mat, m_mat, lam_t = mats
    g, t_len, hs = SSM_GROUPS, SSM_CHUNK, SSM_GROUP
    n_chunks = seq_len // t_len
    rows = n_chunks * bsz
    u2 = su.reshape(bsz, n_chunks, t_len, g, hs).transpose(3, 1, 0, 2, 4).reshape(g, rows, S5_TILE)
    u_spec = pl.BlockSpec((None, rows, S5_TILE), lambda j: (j, 0, 0))
    w_spec = pl.BlockSpec((None, S5_TILE, S5_TILE), lambda j: (j, 0, 0))
    col_spec = pl.BlockSpec((rows, S5_STATE_COLS), lambda j: (0, j))
    z = pl.pallas_call(
        _s5_state_in_kernel,
        grid=(g,),
        in_specs=[u_spec, w_spec],
        out_specs=col_spec,
        out_shape=jax.ShapeDtypeStruct((rows, g * S5_STATE_COLS), F32),
        compiler_params=_params("parallel"),
        name="s5_state_in",
    )(u2, b_mat)
    gb = 2
    s_prev = pl.pallas_call(
        functools.partial(_s5_scan_kernel, n_chunks=n_chunks, bsz=bsz, groups=gb),
        grid=(g // gb,),
        in_specs=[pl.BlockSpec((rows, gb * S5_STATE_COLS), lambda j: (0, j)),
                  pl.BlockSpec((1, gb * S5_STATE_COLS), lambda j: (0, j))],
        out_specs=pl.BlockSpec((rows, gb * S5_STATE_COLS), lambda j: (0, j)),
        out_shape=jax.ShapeDtypeStruct((rows, g * S5_STATE_COLS), F32),
        compiler_params=_params("parallel"),
        name="s5_scan",
    )(z, lam_t)
    y2 = pl.pallas_call(
        _s5_out_kernel,
        grid=(g,),
        in_specs=[u_spec, col_spec, w_spec, w_spec],
        out_specs=pl.BlockSpec((None, rows, S5_TILE), lambda j: (j, 0, 0)),
        out_shape=jax.ShapeDtypeStruct((g, rows, S5_TILE), F32),
        compiler_params=_params("parallel"),
        name="s5_out",
    )(u2, s_prev, toep, m_mat)
    y = y2.reshape(g, n_chunks, bsz, t_len, hs).transpose(2, 1, 3, 0, 4)
    return y.reshape(bsz * seq_len, MIX_WIDTH)


def _merge_kernel(x_ref, attn_ref, nat_ref, y_ref, gate_ref, wglu_ref, wb_ref, wout_ref, gain_ref, bias_ref, o_ref):
    z = jax.nn.gelu(y_ref[...])
    ssm = z * jax.nn.sigmoid(_dot(z.astype(BF16), wglu_ref[...]))
    d = D_MODEL
    merged = gate_ref[:, 0:d].astype(F32) * _dot(attn_ref[...], wb_ref[0])
    merged += gate_ref[:, d:2 * d].astype(F32) * _dot(nat_ref[...], wb_ref[1])
    merged += gate_ref[:, 2 * d:3 * d].astype(F32) * _dot(ssm.astype(BF16), wb_ref[2])
    mix = _dot(merged.astype(BF16), wout_ref[...])
    o_ref[...] = _layer_norm(DEEPNORM_ALPHA * x_ref[...] + mix, gain_ref[...], bias_ref[...])


def _merge(x2d, attn_o, nat_o, y_ssm, gates, w_glu, w_branch, w_out, gain, bias, tm):
    m = x2d.shape[0]
    row = lambda width: pl.BlockSpec((tm, width), lambda i: (i, 0))
    whole = lambda a: pl.BlockSpec(a.shape, lambda i: (0,) * a.ndim)
    w_glu, w_branch, w_out = w_glu.astype(BF16), w_branch.astype(BF16), w_out.astype(BF16)
    gain, bias = gain.astype(F32)[None, :], bias.astype(F32)[None, :]
    return pl.pallas_call(
        _merge_kernel,
        grid=(m // tm,),
        in_specs=[row(D_MODEL), row(MIX_WIDTH), row(MIX_WIDTH), row(MIX_WIDTH), row(GATE_WIDTH),
                  whole(w_glu), whole(w_branch), whole(w_out), whole(gain), whole(bias)],
        out_specs=row(D_MODEL),
        out_shape=jax.ShapeDtypeStruct((m, D_MODEL), F32),
        compiler_params=_params("parallel"),
        name="merge",
    )(x2d, attn_o, nat_o, y_ssm, gates, w_glu, w_branch, w_out, gain, bias)


def _ffn_kernel(x_ref, wup_ref, wdown_ref, gain_ref, bias_ref, o_ref, xb_scr, acc_scr):
    f = pl.program_id(1)

    @pl.when(f == 0)
    def _():
        xb_scr[...] = x_ref[...].astype(BF16)
        acc_scr[...] = jnp.zeros_like(acc_scr)

    h = jnp.maximum(_dot(xb_scr[...], wup_ref[...]), 0.0)
    acc_scr[...] += _dot((h * h).astype(BF16), wdown_ref[...])

    @pl.when(f == pl.num_programs(1) - 1)
    def _():
        o_ref[...] = _layer_norm(DEEPNORM_ALPHA * x_ref[...] + acc_scr[...], gain_ref[...], bias_ref[...])


def _ffn(x2d, w_up, w_down, gain, bias, tm, tf):
    m = x2d.shape[0]
    w_up, w_down = w_up.astype(BF16), w_down.astype(BF16)
    gain, bias = gain.astype(F32)[None, :], bias.astype(F32)[None, :]
    vec = pl.BlockSpec((1, D_MODEL), lambda i, f: (0, 0))
    return pl.pallas_call(
        _ffn_kernel,
        grid=(m // tm, FFN_DIM // tf),
        in_specs=[pl.BlockSpec((tm, D_MODEL), lambda i, f: (i, 0)),
                  pl.BlockSpec((D_MODEL, tf), lambda i, f: (0, f)),
                  pl.BlockSpec((tf, D_MODEL), lambda i, f: (f, 0)),
                  vec, vec],
        out_specs=pl.BlockSpec((tm, D_MODEL), lambda i, f: (i, 0)),
        out_shape=jax.ShapeDtypeStruct((m, D_MODEL), F32),
        scratch_shapes=[pltpu.VMEM((tm, D_MODEL), BF16), pltpu.VMEM((tm, D_MODEL), F32)],
        compiler_params=_params("parallel", "arbitrary"),
        name="ffn",
    )(x2d, w_up, w_down, gain, bias)


def _tile_sizes(seq_len):
    proj_tm = min(512, seq_len)
    attn_tq = min(256, seq_len)
    attn_tk = min(1024, seq_len)
    ffn_tm = min(1024, seq_len)
    ffn_tf = 512
    return proj_tm, attn_tq, attn_tk, ffn_tm, ffn_tf


def kernel(x, w_in, q_norm_gain, k_norm_gain, nat_rel_bias, ssm_a_re, ssm_a_im, ssm_log_dt, ssm_b_re, ssm_b_im, ssm_c_re, ssm_c_im, ssm_d, ssm_w_glu, w_branch, w_out, ln1_gain, ln1_bias, w_ffn_up, w_ffn_down, ln2_gain, ln2_bias):
    bsz, seq_len, _ = x.shape
    proj_tm, attn_tq, attn_tk, ffn_tm, ffn_tf = _tile_sizes(seq_len)
    rope = _rope_tables(seq_len)
    h = x.reshape(bsz * seq_len, D_MODEL)
    for layer in range(w_in.shape[0]):
        q, k, av, nq, nk, nv, su, gates = _input_projections(
            h, w_in[layer], q_norm_gain[layer], k_norm_gain[layer], rope, seq_len, proj_tm)
        attn_o = _gqa_attention(q, k, av, bsz, seq_len, attn_tq, attn_tk)
        nat_o = _neighbourhood_attention(nq, nk, nv, nat_rel_bias[layer], bsz, seq_len)
        mats = _s5_matrices(ssm_a_re[layer], ssm_a_im[layer], ssm_log_dt[layer], ssm_b_re[layer], ssm_b_im[layer],
                            ssm_c_re[layer], ssm_c_im[layer], ssm_d[layer])
        y_ssm = _s5_bidirectional(su, mats, bsz, seq_len)
        h = _merge(h, attn_o, nat_o, y_ssm, gates, ssm_w_glu[layer], w_branch[layer], w_out[layer],
                   ln1_gain[layer], ln1_bias[layer], proj_tm)
        h = _ffn(h, w_ffn_up[layer], w_ffn_down[layer], ln2_gain[layer], ln2_bias[layer], ffn_tm, ffn_tf)
    return h.reshape(bsz, seq_len, D_MODEL)
```

```python
import functools

import jax
import jax.numpy as jnp
from jax import lax
from jax.experimental import pallas as pl
from jax.experimental.pallas import tpu as pltpu

D_MODEL = 1024
DEPTH = 2
GRID_W = 64
HEAD_DIM = 64
MIX_WIDTH = 512
ATTN_Q_HEADS = 8
ATTN_KV_HEADS = 2
ATTN_GROUP = ATTN_Q_HEADS // ATTN_KV_HEADS
NAT_HEADS = 8
NAT_WIN_ROWS = 8
NAT_WIN_COLS = 16
SSM_GROUP = 16
SSM_GROUPS = 32
SSM_STATE = 64
SSM_CHUNK = 16
N_BRANCHES = 3
FFN_DIM = 4 * D_MODEL
ROPE_THETA = 10000.0
LN_EPS = 1e-5
RMS_EPS = 1e-6
DEEPNORM_ALPHA = (2 * DEPTH) ** 0.25
ATTN_SCALE = HEAD_DIM ** -0.5
LOG2_E = 1.4426950408889634
MASK_VALUE = -1e30

Q_WIDTH = ATTN_Q_HEADS * HEAD_DIM
KV_WIDTH = ATTN_KV_HEADS * HEAD_DIM
GATE_WIDTH = N_BRANCHES * D_MODEL

LANES = 128
SUBLANES = 8
MXU_WIDTH = 256
V7X_VMEM_BYTES = 64 * 1024 * 1024
VMEM_LIMIT = V7X_VMEM_BYTES - 8 * 1024 * 1024

F32 = jnp.float32
BF16 = jnp.bfloat16
NT_DIMS = (((1,), (1,)), ((), ()))


def _params(*semantics):
    return pltpu.CompilerParams(dimension_semantics=semantics, vmem_limit_bytes=VMEM_LIMIT)


def _dot(a, b):
    return jnp.dot(a, b, preferred_element_type=F32)


def _layer_norm(x, gain, bias):
    mu = jnp.mean(x, axis=-1, keepdims=True)
    xc = x - mu
    var = jnp.mean(xc * xc, axis=-1, keepdims=True)
    return xc * lax.rsqrt(var + LN_EPS) * gain + bias


def _resident(a):
    return pl.BlockSpec(a.shape, lambda *_: (0,) * a.ndim, pipeline_mode=pl.Buffered(1))


ATTN_KV_PAGE = 512
ATTN_TQ = MXU_WIDTH
PROJ_T_ROWS = Q_WIDTH + KV_WIDTH + MIX_WIDTH
PROJ_N_COLS = KV_WIDTH + 3 * MIX_WIDTH + GATE_WIDTH


def _proj_kernel(x_ref, wt_ref, wn_ref, qgain_ref, kgain_ref, cost_ref, sint_ref, cos_ref, sin_ref, seg_ref,
                 qt_ref, vt_ref, nkt_ref, k_ref, nq_ref, nv_ref, su_ref, g_ref):
    tm = x_ref.shape[0]
    d = HEAD_DIM
    xb = x_ref[...].astype(BF16)

    yt = lax.dot_general(wt_ref[...], xb, NT_DIMS, preferred_element_type=F32)
    cost = cost_ref[...]
    sint = sint_ref[...]
    qgain = qgain_ref[...]
    for h in range(ATTN_Q_HEADS):
        blk = yt[d * h:d * (h + 1), :]
        ms = jnp.mean(blk * blk, axis=0, keepdims=True)
        yn = blk * lax.rsqrt(ms + RMS_EPS) * qgain
        partner = jnp.concatenate([yn[16:32], yn[0:16], yn[48:64], yn[32:48]], axis=0)
        qh = ((yn * cost + partner * sint) * (ATTN_SCALE * LOG2_E)).astype(BF16)
        j, g = divmod(h, ATTN_GROUP)
        for qb in range(tm // ATTN_TQ):
            qt_ref[j, qb, :, ATTN_TQ * g:ATTN_TQ * (g + 1)] = qh[:, ATTN_TQ * qb:ATTN_TQ * (qb + 1)]
    vt = yt[Q_WIDTH:Q_WIDTH + KV_WIDTH, :].astype(BF16)
    for pg in range(tm // ATTN_KV_PAGE):
        vt_ref[pg] = vt[:, ATTN_KV_PAGE * pg:ATTN_KV_PAGE * (pg + 1)]
    nkt_ref[...] = yt[Q_WIDTH + KV_WIDTH:, :].astype(BF16)

    yk = _dot(xb, wn_ref[:, 0:KV_WIDTH])
    y2 = yk * yk
    hi = y2.astype(BF16)
    lo = (y2 - hi.astype(F32)).astype(BF16)
    ms = (_dot(hi, seg_ref[...]) + _dot(lo, seg_ref[...])) * (1.0 / d)
    kn = yk * lax.rsqrt(ms + RMS_EPS) * kgain_ref[...]
    lane = lax.broadcasted_iota(jnp.int32, (tm, LANES), 1)
    partner = jnp.where((lane % 32) < 16, pltpu.roll(kn, LANES - 16, 1), pltpu.roll(kn, 16, 1))
    kk = kn * cos_ref[...] + partner * sin_ref[...]
    for j in range(ATTN_KV_HEADS):
        k_ref[j] = kk[:, d * j:d * (j + 1)].astype(BF16)

    c0 = KV_WIDTH
    ynq = _dot(xb, wn_ref[:, c0:c0 + MIX_WIDTH]) * ATTN_SCALE
    ynv = _dot(xb, wn_ref[:, c0 + MIX_WIDTH:c0 + 2 * MIX_WIDTH])
    for h in range(NAT_HEADS):
        nq_ref[h] = ynq[:, d * h:d * (h + 1)].astype(BF16)
        nv_ref[h] = ynv[:, d * h:d * (h + 1)].astype(BF16)
    su_ref[...] = _dot(xb, wn_ref[:, c0 + 2 * MIX_WIDTH:c0 + 3 * MIX_WIDTH]).astype(BF16)
    c1 = c0 + 3 * MIX_WIDTH
    for n in range(N_BRANCHES):
        y = _dot(xb, wn_ref[:, c1 + D_MODEL * n:c1 + D_MODEL * (n + 1)])
        g_ref[:, D_MODEL * n:D_MODEL * (n + 1)] = jax.nn.sigmoid(y).astype(BF16)


def _rope_tables(seq_len):
    t = jnp.arange(seq_len)
    row = (t // GRID_W).astype(F32)
    col = (t % GRID_W).astype(F32)
    axis_dim = HEAD_DIM // 2
    inv_freq = 1.0 / (ROPE_THETA ** (jnp.arange(0, axis_dim, 2, dtype=F32) / axis_dim))
    ang_r = row[:, None] * inv_freq[None, :]
    ang_c = col[:, None] * inv_freq[None, :]
    cos_head = jnp.concatenate([jnp.cos(ang_r), jnp.cos(ang_r), jnp.cos(ang_c), jnp.cos(ang_c)], axis=1)
    sin_head = jnp.concatenate([-jnp.sin(ang_r), jnp.sin(ang_r), -jnp.sin(ang_c), jnp.sin(ang_c)], axis=1)
    reps = LANES // HEAD_DIM
    return (jnp.tile(cos_head, (1, reps)), jnp.tile(sin_head, (1, reps)), cos_head.T, sin_head.T)


def _input_projections(x2d, w_in, q_gain, k_gain, rope, bsz, seq_len, tm):
    m = x2d.shape[0]
    nl = seq_len // tm
    pages = tm // ATTN_KV_PAGE
    w = w_in.astype(BF16)
    o = [0, Q_WIDTH, Q_WIDTH + KV_WIDTH, Q_WIDTH + 2 * KV_WIDTH]
    o += [o[-1] + MIX_WIDTH * i for i in range(1, 5)]
    aq, ak, av, nq, nk, nv, su, gate = (w[:, a:b] for a, b in zip(o, o[1:] + [w.shape[1]]))
    wt = jnp.concatenate([aq, av, nk], axis=1).T
    wn = jnp.concatenate([ak, nq, nv, su, gate], axis=1)
    qgain = jnp.broadcast_to(q_gain.astype(F32)[:, None], (HEAD_DIM, tm))
    kgain = jnp.tile(k_gain.astype(F32), ATTN_KV_HEADS)[None, :]
    seg = (jnp.arange(KV_WIDTH)[:, None] // HEAD_DIM == jnp.arange(KV_WIDTH)[None, :] // HEAD_DIM).astype(BF16)
    cos, sin, cos_t, sin_t = rope
    tok = lambda width: pl.BlockSpec((tm, width), lambda i: (i, 0))
    heads = lambda n: pl.BlockSpec((n, tm, HEAD_DIM), lambda i: (0, i, 0))
    feat_t = lambda rows: pl.BlockSpec((None, rows, tm), lambda i: (i // nl, 0, i % nl))
    return pl.pallas_call(
        _proj_kernel,
        grid=(m // tm,),
        in_specs=[tok(D_MODEL), _resident(wt), _resident(wn), _resident(qgain), _resident(kgain),
                  pl.BlockSpec((HEAD_DIM, tm), lambda i: (0, i % nl)),
                  pl.BlockSpec((HEAD_DIM, tm), lambda i: (0, i % nl)),
                  pl.BlockSpec((tm, LANES), lambda i: (i % nl, 0)),
                  pl.BlockSpec((tm, LANES), lambda i: (i % nl, 0)),
                  _resident(seg)],
        out_specs=[pl.BlockSpec((None, ATTN_KV_HEADS, tm // ATTN_TQ, HEAD_DIM, ATTN_GROUP * ATTN_TQ),
                                lambda i: (i // nl, 0, i % nl, 0, 0)),
                   pl.BlockSpec((None, pages, KV_WIDTH, ATTN_KV_PAGE), lambda i: (i // nl, i % nl, 0, 0)),
                   feat_t(MIX_WIDTH),
                   heads(ATTN_KV_HEADS), heads(NAT_HEADS), heads(NAT_HEADS),
                   tok(MIX_WIDTH), tok(GATE_WIDTH)],
        out_shape=[jax.ShapeDtypeStruct((bsz, ATTN_KV_HEADS, seq_len // ATTN_TQ, HEAD_DIM, ATTN_GROUP * ATTN_TQ), BF16),
                   jax.ShapeDtypeStruct((bsz, seq_len // ATTN_KV_PAGE, KV_WIDTH, ATTN_KV_PAGE), BF16),
                   jax.ShapeDtypeStruct((bsz, MIX_WIDTH, seq_len), BF16),
                   jax.ShapeDtypeStruct((ATTN_KV_HEADS, m, HEAD_DIM), BF16),
                   jax.ShapeDtypeStruct((NAT_HEADS, m, HEAD_DIM), BF16),
                   jax.ShapeDtypeStruct((NAT_HEADS, m, HEAD_DIM), BF16),
                   jax.ShapeDtypeStruct((m, MIX_WIDTH), BF16),
                   jax.ShapeDtypeStruct((m, GATE_WIDTH), BF16)],
        compiler_params=_params("parallel"),
        name="in_proj",
    )(x2d, wt, wn, qgain, kgain, cos_t, sin_t, cos, sin, seg)


def _flash_kernel(qg_ref, k_ref, vt_ref, o_ref, acc_scr, *, n_pages):
    qg = qg_ref[...]
    nq = qg.shape[1]
    acc_scr[...] = jnp.zeros_like(acc_scr)

    def scores(c):
        return _dot(k_ref[ATTN_KV_PAGE * c:ATTN_KV_PAGE * (c + 1), :], qg)

    m = jnp.full((1, nq), -jnp.inf, F32)
    l = jnp.zeros((1, nq), F32)
    s_next = scores(0)
    for c in range(n_pages):
        s = s_next
        if c + 1 < n_pages:
            s_next = scores(c + 1)
        m_new = jnp.maximum(m, jnp.max(s, axis=0, keepdims=True))
        alpha = jnp.exp2(m - m_new)
        p = jnp.exp2(s - m_new)
        l = alpha * l + jnp.sum(p, axis=0, keepdims=True)
        acc_scr[...] = alpha * acc_scr[...] + _dot(vt_ref[c], p.astype(BF16))
        m = m_new
    o = acc_scr[...] / l
    o_ref[...] = jnp.concatenate([o[:, ATTN_TQ * g:ATTN_TQ * (g + 1)].T for g in range(ATTN_GROUP)],
                                 axis=1).astype(o_ref.dtype)


def _gqa_attention(qg, k, vt, bsz, seq_len):
    d, hk, grp, tq = HEAD_DIM, ATTN_KV_HEADS, ATTN_GROUP, ATTN_TQ
    nq, n_pages = seq_len // tq, seq_len // ATTN_KV_PAGE
    return pl.pallas_call(
        functools.partial(_flash_kernel, n_pages=n_pages),
        grid=(bsz, hk, nq),
        in_specs=[pl.BlockSpec((None, None, None, d, grp * tq), lambda b, j, i: (b, j, i, 0, 0)),
                  pl.BlockSpec((None, seq_len, d), lambda b, j, i: (j, b, 0)),
                  pl.BlockSpec((None, n_pages, d, ATTN_KV_PAGE), lambda b, j, i: (b, 0, j, 0))],
        out_specs=pl.BlockSpec((tq, grp * d), lambda b, j, i: (b * nq + i, j)),
        out_shape=jax.ShapeDtypeStruct((bsz * seq_len, ATTN_Q_HEADS * d), BF16),
        scratch_shapes=[pltpu.VMEM((d, grp * tq), F32)],
        compiler_params=_params("parallel", "parallel", "parallel"),
        name="gqa_flash",
    )(qg, k, vt)


NAT_ROWS_PER_STEP = 8
NAT_STEP_TOKENS = NAT_ROWS_PER_STEP * GRID_W
NAT_WIN_TOKENS = NAT_WIN_ROWS * GRID_W


def _nat_kernel(q_ref, ktp_ref, ktc_ref, ktn_ref, vp_ref, vc_ref, vn_ref, bias_ref, o_ref, kt_scr, v_scr, oh_scr):
    g = pl.program_id(1)
    ng = pl.num_programs(1)
    st = NAT_STEP_TOKENS
    d = HEAD_DIM
    kt_scr[:, 0:st] = ktp_ref[...]
    kt_scr[:, st:2 * st] = ktc_ref[...]
    kt_scr[:, 2 * st:3 * st] = ktn_ref[...]
    v_scr[:, 0:st, :] = vp_ref[...]
    v_scr[:, st:2 * st, :] = vc_ref[...]
    v_scr[:, 2 * st:3 * st, :] = vn_ref[...]

    def all_heads(frame_rows, bias_index):
        starts = [fr * GRID_W for fr in frame_rows]

        def head(h, carry):
            q = q_ref[h]
            r0 = pl.multiple_of(h * d, d)
            s = jnp.concatenate(
                [_dot(q[GRID_W * r:GRID_W * (r + 1), :],
                      kt_scr[pl.ds(r0, d), starts[r]:starts[r] + NAT_WIN_TOKENS]) + bias_ref[bias_index[r], h]
                 for r in range(NAT_ROWS_PER_STEP)], axis=0)
            m = jnp.max(s, axis=1, keepdims=True)
            p = jnp.exp(s - m)
            l = jnp.sum(p, axis=1, keepdims=True)
            pb = p.astype(BF16)
            o = jnp.concatenate(
                [_dot(pb[GRID_W * r:GRID_W * (r + 1), :], v_scr[h, starts[r]:starts[r] + NAT_WIN_TOKENS, :])
                 for r in range(NAT_ROWS_PER_STEP)], axis=0)
            oh_scr[h] = o / l
            return carry

        lax.fori_loop(0, NAT_HEADS, head, 0)

    half = NAT_WIN_ROWS // 2
    n = NAT_ROWS_PER_STEP

    @pl.when(g == 0)
    def _():
        all_heads([n + max(r - half, 0) for r in range(n)], [min(r, half) for r in range(n)])

    @pl.when(jnp.logical_and(g > 0, g < ng - 1))
    def _():
        all_heads([r + half for r in range(n)], [half] * n)

    @pl.when(jnp.logical_and(g == ng - 1, g > 0))
    def _():
        all_heads([min(r + half, n) for r in range(n)], [max(r, half) for r in range(n)])

    o_ref[...] = jnp.concatenate([oh_scr[h] for h in range(NAT_HEADS)], axis=1).astype(o_ref.dtype)


def _nat_bias_table(rel_bias):
    cols = jnp.arange(GRID_W)
    col_start = jnp.clip(cols - NAT_WIN_COLS // 2, 0, GRID_W - NAT_WIN_COLS)
    kc = jnp.arange(GRID_W)
    in_win = (kc[None, :] >= col_start[:, None]) & (kc[None, :] < col_start[:, None] + NAT_WIN_COLS)
    col_off = kc[None, :] - cols[:, None] + (NAT_WIN_COLS - 1)
    v = jnp.arange(NAT_WIN_ROWS)
    i = jnp.arange(NAT_WIN_ROWS)
    row_off = i[None, :] - v[:, None] + (NAT_WIN_ROWS - 1)
    row_hot = (row_off[:, :, None] == jnp.arange(2 * NAT_WIN_ROWS - 1)[None, None, :]).astype(F32)
    col_hot = (col_off[:, :, None] == jnp.arange(2 * NAT_WIN_COLS - 1)[None, None, :]).astype(F32)
    b = jnp.einsum('hab,via,ckb->vhcik', rel_bias.astype(F32), row_hot, col_hot, precision=lax.Precision.HIGHEST)
    b = jnp.where(in_win[None, None, :, None, :], b, MASK_VALUE)
    return b.reshape(NAT_WIN_ROWS, NAT_HEADS, GRID_W, NAT_WIN_TOKENS)


def _neighbourhood_attention(nq, nkt, nv, rel_bias, bsz, seq_len):
    h, d, st = NAT_HEADS, HEAD_DIM, NAT_STEP_TOKENS
    ng = seq_len // st
    assert ng >= 2 and seq_len // GRID_W >= NAT_WIN_ROWS
    bias = _nat_bias_table(rel_bias)
    prev = lambda g: jnp.maximum(g - 1, 0)
    nxt = lambda g: jnp.minimum(g + 1, ng - 1)
    same = lambda g: g
    kt_spec = lambda f: pl.BlockSpec((None, h * d, st), lambda b, g: (b, 0, f(g)))
    v_spec = lambda f: pl.BlockSpec((h, st, d), lambda b, g: (0, b * ng + f(g), 0))
    return pl.pallas_call(
        _nat_kernel,
        grid=(bsz, ng),
        in_specs=[v_spec(same), kt_spec(prev), kt_spec(same), kt_spec(nxt),
                  v_spec(prev), v_spec(same), v_spec(nxt), _resident(bias)],
        out_specs=pl.BlockSpec((st, h * d), lambda b, g: (b * ng + g, 0)),
        out_shape=jax.ShapeDtypeStruct((bsz * seq_len, h * d), BF16),
        scratch_shapes=[pltpu.VMEM((h * d, 3 * st), BF16), pltpu.VMEM((h, 3 * st, d), BF16),
                        pltpu.VMEM((h, st, d), F32)],
        compiler_params=_params("parallel", "parallel"),
        name="nat",
    )(nq, nkt, nkt, nkt, nv, nv, nv, bias)


S5_TILE = SSM_CHUNK * SSM_GROUP
S5_STATE_COLS = 4 * SSM_STATE


def _s5_matrices(a_re, a_im, log_dt, b_re, b_im, c_re, c_im, d_skip):
    t_len, hs = SSM_CHUNK, SSM_GROUP
    a_re = a_re.astype(F32)
    a_im = a_im.astype(F32)
    dt = jnp.exp(log_dt.astype(F32))[..., None]
    decay = jnp.exp(a_re * dt)
    phase = a_im * dt
    lam_re = decay * jnp.cos(phase)
    lam_im = decay * jnp.sin(phase)
    den = a_re * a_re + a_im * a_im
    num_re = lam_re - 1.0
    coef_re = (num_re * a_re + lam_im * a_im) / den
    coef_im = (lam_im * a_re - num_re * a_im) / den
    b_re = b_re.astype(F32)[None]
    b_im = b_im.astype(F32)[None]
    bbar_re = coef_re[..., None] * b_re - coef_im[..., None] * b_im
    bbar_im = coef_re[..., None] * b_im + coef_im[..., None] * b_re
    k = jnp.arange(t_len + 1, dtype=F32)[:, None, None, None]
    pow_mag = jnp.exp(k * (a_re * dt)[None])
    pow_re = pow_mag * jnp.cos(k * phase[None])
    pow_im = pow_mag * jnp.sin(k * phase[None])
    c_re = c_re.astype(F32)
    c_im = c_im.astype(F32)

    cl_re = jnp.einsum('dghp,tdgp->tdghp', c_re, pow_re) - jnp.einsum('dghp,tdgp->tdghp', c_im, pow_im)
    cl_im = jnp.einsum('dghp,tdgp->tdghp', c_re, pow_im) + jnp.einsum('dghp,tdgp->tdghp', c_im, pow_re)
    hp = lax.Precision.HIGHEST
    taps = (jnp.einsum('tdgop,dgpi->tdgoi', cl_re, bbar_re, precision=hp)
            - jnp.einsum('tdgop,dgpi->tdgoi', cl_im, bbar_im, precision=hp))
    taps = taps[:t_len]
    fwd = jnp.stack([jnp.concatenate([jnp.zeros_like(taps[:kk, 0]), taps[:t_len - kk, 0]], axis=0)
                     for kk in range(t_len)], axis=0)
    rev = jnp.stack([jnp.concatenate([taps[:kk + 1, 1][::-1], jnp.zeros_like(taps[:t_len - kk - 1, 1])], axis=0)
                     for kk in range(t_len)], axis=0)
    eye_t = jnp.eye(t_len, dtype=F32)[:, :, None, None, None]
    skip = eye_t * (jnp.eye(hs, dtype=F32)[None] * d_skip.astype(F32)[:, :, None])[None, None]
    toep = (fwd + rev + skip).transpose(2, 0, 4, 1, 3)
    toep = toep.reshape(SSM_GROUPS, S5_TILE, S5_TILE)

    def in_mat(pw_re, pw_im, d):
        re = pw_re[..., None] * bbar_re[d][None] - pw_im[..., None] * bbar_im[d][None]
        im = pw_re[..., None] * bbar_im[d][None] + pw_im[..., None] * bbar_re[d][None]
        to = lambda m: m.transpose(1, 0, 3, 2).reshape(SSM_GROUPS, S5_TILE, SSM_STATE)
        return to(re), to(im)
    f_re, f_im = in_mat(pow_re[:t_len, 0][::-1], pow_im[:t_len, 0][::-1], 0)
    r_re, r_im = in_mat(pow_re[:t_len, 1], pow_im[:t_len, 1], 1)
    b_mat = jnp.concatenate([f_re, r_re, f_im, r_im], axis=2)

    def out_mat(cre, cim):
        to = lambda m: m.transpose(1, 3, 0, 2).reshape(SSM_GROUPS, SSM_STATE, S5_TILE)
        return to(cre), to(-cim)
    mf_re, mf_im = out_mat(cl_re[1:t_len + 1, 0], cl_im[1:t_len + 1, 0])
    mr_re, mr_im = out_mat(cl_re[1:t_len + 1, 1][::-1], cl_im[1:t_len + 1, 1][::-1])
    m_mat = jnp.concatenate([mf_re, mr_re, mf_im, mr_im], axis=1)

    lam_t = jnp.concatenate([pow_re[t_len, 0], pow_re[t_len, 1], pow_im[t_len, 0], pow_im[t_len, 1]], axis=1)
    return toep.astype(BF16), b_mat.astype(BF16), m_mat.astype(BF16), lam_t.reshape(1, SSM_GROUPS * S5_STATE_COLS)


def _s5_state_in_kernel(u_ref, b_ref, z_ref):
    z_ref[...] = _dot(u_ref[...], b_ref[...])


def _s5_scan_kernel(z_ref, lam_ref, s_ref, *, n_chunks, bsz, groups):
    assert 2 * bsz == SUBLANES and n_chunks % 2 == 0
    n_tiles = n_chunks // 2
    lane = lax.broadcasted_iota(jnp.int32, (SUBLANES, LANES), 1)
    sub = lax.broadcasted_iota(jnp.int32, (SUBLANES, LANES), 0)
    is_fwd = lane < SSM_STATE
    is_rev = jnp.logical_not(is_fwd)
    first = sub < bsz
    swap = lambda a: pltpu.roll(a, bsz, 0)
    lam = [(jnp.broadcast_to(lam_ref[:, 256 * j:256 * j + LANES], (SUBLANES, LANES)),
            jnp.broadcast_to(lam_ref[:, 256 * j + LANES:256 * (j + 1)], (SUBLANES, LANES))) for j in range(groups)]

    def step(k, carry):
        rf = pl.multiple_of(k * SUBLANES, SUBLANES)
        rr = pl.multiple_of((n_tiles - 1 - k) * SUBLANES, SUBLANES)
        new = []
        for j in range(groups):
            c_re, c_im = carry[j]
            lr, li = lam[j]
            cre = slice(256 * j, 256 * j + LANES)
            cim = slice(256 * j + LANES, 256 * (j + 1))
            w_re = jnp.where(is_fwd, z_ref[pl.ds(rf, SUBLANES), cre], swap(z_ref[pl.ds(rr, SUBLANES), cre]))
            w_im = jnp.where(is_fwd, z_ref[pl.ds(rf, SUBLANES), cim], swap(z_ref[pl.ds(rr, SUBLANES), cim]))
            a_re = swap(lr * c_re - li * c_im + w_re)
            a_im = swap(lr * c_im + li * c_re + w_im)
            b_re = lr * a_re - li * a_im + w_re
            b_im = lr * a_im + li * a_re + w_im
            e_re = jnp.where(first, c_re, a_re)
            e_im = jnp.where(first, c_im, a_im)
            pltpu.store(s_ref.at[pl.ds(rf, SUBLANES), cre], e_re, mask=is_fwd)
            pltpu.store(s_ref.at[pl.ds(rf, SUBLANES), cim], e_im, mask=is_fwd)
            pltpu.store(s_ref.at[pl.ds(rr, SUBLANES), cre], swap(e_re), mask=is_rev)
            pltpu.store(s_ref.at[pl.ds(rr, SUBLANES), cim], swap(e_im), mask=is_rev)
            new.append((jnp.where(first, swap(b_re), b_re), jnp.where(first, swap(b_im), b_im)))
        return tuple(new)

    zero = jnp.zeros((SUBLANES, LANES), F32)
    lax.fori_loop(0, n_tiles, step, tuple((zero, zero) for _ in range(groups)))


def _s5_out_kernel(u_ref, s_ref, t_ref, m_ref, y_ref):
    y = _dot(u_ref[...], t_ref[...]) + _dot(s_ref[...].astype(BF16), m_ref[...])
    y_ref[...] = y.astype(y_ref.dtype)


def _s5_bidirectional(su, mats, bsz, seq_len):
    toep, b_mat, m_mat, lam_t = mats
    g, t_len, hs = SSM_GROUPS, SSM_CHUNK, SSM_GROUP
    n_chunks = seq_len // t_len
    rows = n_chunks * bsz
    u2 = su.reshape(bsz, n_chunks, t_len, g, hs).transpose(3, 1, 0, 2, 4).reshape(g, rows, S5_TILE)
    u_spec = pl.BlockSpec((None, rows, S5_TILE), lambda j: (j, 0, 0))
    w_spec = pl.BlockSpec((None, S5_TILE, S5_TILE), lambda j: (j, 0, 0))
    col_spec = pl.BlockSpec((rows, S5_STATE_COLS), lambda j: (0, j))
    z = pl.pallas_call(
        _s5_state_in_kernel,
        grid=(g,),
        in_specs=[u_spec, w_spec],
        out_specs=col_spec,
        out_shape=jax.ShapeDtypeStruct((rows, g * S5_STATE_COLS), F32),
        compiler_params=_params("parallel"),
        name="s5_state_in",
    )(u2, b_mat)
    gb = 2
    s_prev = pl.pallas_call(
        functools.partial(_s5_scan_kernel, n_chunks=n_chunks, bsz=bsz, groups=gb),
        grid=(g // gb,),
        in_specs=[pl.BlockSpec((rows, gb * S5_STATE_COLS), lambda j: (0, j)),
                  pl.BlockSpec((1, gb * S5_STATE_COLS), lambda j: (0, j))],
        out_specs=pl.BlockSpec((rows, gb * S5_STATE_COLS), lambda j: (0, j)),
        out_shape=jax.ShapeDtypeStruct((rows, g * S5_STATE_COLS), F32),
        compiler_params=_params("parallel"),
        name="s5_scan",
    )(z, lam_t)
    y2 = pl.pallas_call(
        _s5_out_kernel,
        grid=(g,),
        in_specs=[u_spec, col_spec, w_spec, w_spec],
        out_specs=pl.BlockSpec((None, rows, S5_TILE), lambda j: (j, 0, 0)),
        out_shape=jax.ShapeDtypeStruct((g, rows, S5_TILE), BF16),
        compiler_params=_params("parallel"),
        name="s5_out",
    )(u2, s_prev, toep, m_mat)
    y = y2.reshape(g, n_chunks, bsz, t_len, hs).transpose(2, 1, 3, 0, 4)
    return y.reshape(bsz * seq_len, MIX_WIDTH)


def _merge_kernel(x_ref, attn_ref, nat_ref, y_ref, gate_ref, wglu_ref, wb_ref, wout_ref, gain_ref, bias_ref, o_ref):
    z = jax.nn.gelu(y_ref[...].astype(F32))
    ssm = z * jax.nn.sigmoid(_dot(z.astype(BF16), wglu_ref[...]))
    d = D_MODEL
    merged = gate_ref[:, 0:d].astype(F32) * _dot(attn_ref[...], wb_ref[0])
    merged += gate_ref[:, d:2 * d].astype(F32) * _dot(nat_ref[...], wb_ref[1])
    merged += gate_ref[:, 2 * d:3 * d].astype(F32) * _dot(ssm.astype(BF16), wb_ref[2])
    mix = _dot(merged.astype(BF16), wout_ref[...])
    o_ref[...] = _layer_norm(DEEPNORM_ALPHA * x_ref[...] + mix, gain_ref[...], bias_ref[...])


def _merge(x2d, attn_o, nat_o, y_ssm, gates, w_glu, w_branch, w_out, gain, bias, tm):
    m = x2d.shape[0]
    row = lambda width: pl.BlockSpec((tm, width), lambda i: (i, 0))
    w_glu, w_branch, w_out = w_glu.astype(BF16), w_branch.astype(BF16), w_out.astype(BF16)
    gain, bias = gain.astype(F32)[None, :], bias.astype(F32)[None, :]
    return pl.pallas_call(
        _merge_kernel,
        grid=(m // tm,),
        in_specs=[row(D_MODEL), row(MIX_WIDTH), row(MIX_WIDTH), row(MIX_WIDTH), row(GATE_WIDTH),
                  _resident(w_glu), _resident(w_branch), _resident(w_out), _resident(gain), _resident(bias)],
        out_specs=row(D_MODEL),
        out_shape=jax.ShapeDtypeStruct((m, D_MODEL), F32),
        compiler_params=_params("parallel"),
        name="merge",
    )(x2d, attn_o, nat_o, y_ssm, gates, w_glu, w_branch, w_out, gain, bias)


def _ffn_kernel(x_ref, wup_ref, wdown_ref, gain_ref, bias_ref, o_ref, xb_scr, acc_scr):
    f = pl.program_id(1)

    @pl.when(f == 0)
    def _():
        xb_scr[...] = x_ref[...].astype(BF16)
        acc_scr[...] = jnp.zeros_like(acc_scr)

    h = jnp.maximum(_dot(xb_scr[...], wup_ref[...]), 0.0)
    acc_scr[...] += _dot((h * h).astype(BF16), wdown_ref[...])

    @pl.when(f == pl.num_programs(1) - 1)
    def _():
        o_ref[...] = _layer_norm(DEEPNORM_ALPHA * x_ref[...] + acc_scr[...], gain_ref[...], bias_ref[...])


def _ffn(x2d, w_up, w_down, gain, bias, tm, tf):
    m = x2d.shape[0]
    w_up, w_down = w_up.astype(BF16), w_down.astype(BF16)
    gain, bias = gain.astype(F32)[None, :], bias.astype(F32)[None, :]
    vec = pl.BlockSpec((1, D_MODEL), lambda i, f: (0, 0))
    return pl.pallas_call(
        _ffn_kernel,
        grid=(m // tm, FFN_DIM // tf),
        in_specs=[pl.BlockSpec((tm, D_MODEL), lambda i, f: (i, 0)),
                  pl.BlockSpec((D_MODEL, tf), lambda i, f: (0, f)),
                  pl.BlockSpec((tf, D_MODEL), lambda i, f: (f, 0)),
                  vec, vec],
        out_specs=pl.BlockSpec((tm, D_MODEL), lambda i, f: (i, 0)),
        out_shape=jax.ShapeDtypeStruct((m, D_MODEL), F32),
        scratch_shapes=[pltpu.VMEM((tm, D_MODEL), BF16), pltpu.VMEM((tm, D_MODEL), F32)],
        compiler_params=_params("parallel", "arbitrary"),
        name="ffn",
    )(x2d, w_up, w_down, gain, bias)


def _tile_sizes(seq_len):
    proj_tm = min(512, seq_len)
    ffn_tm = min(1024, seq_len)
    ffn_tf = 512
    return proj_tm, ffn_tm, ffn_tf


def kernel(x, w_in, q_norm_gain, k_norm_gain, nat_rel_bias, ssm_a_re, ssm_a_im, ssm_log_dt, ssm_b_re, ssm_b_im, ssm_c_re, ssm_c_im, ssm_d, ssm_w_glu, w_branch, w_out, ln1_gain, ln1_bias, w_ffn_up, w_ffn_down, ln2_gain, ln2_bias):
    bsz, seq_len, _ = x.shape
    proj_tm, ffn_tm, ffn_tf = _tile_sizes(seq_len)
    rope = _rope_tables(seq_len)
    h = x.reshape(bsz * seq_len, D_MODEL)
    for layer in range(w_in.shape[0]):
        qt, vt, nkt, k, nq, nv, su, gates = _input_projections(
            h, w_in[layer], q_norm_gain[layer], k_norm_gain[layer], rope, bsz, seq_len, proj_tm)
        attn_o = _gqa_attention(qt, k, vt, bsz, seq_len)
        nat_o = _neighbourhood_attention(nq, nkt, nv, nat_rel_bias[layer], bsz, seq_len)
        mats = _s5_matrices(ssm_a_re[layer], ssm_a_im[layer], ssm_log_dt[layer], ssm_b_re[layer], ssm_b_im[layer],
                            ssm_c_re[layer], ssm_c_im[layer], ssm_d[layer])
        y_ssm = _s5_bidirectional(su, mats, bsz, seq_len)
        h = _merge(h, attn_o, nat_o, y_ssm, gates, ssm_w_glu[layer], w_branch[layer], w_out[layer],
                   ln1_gain[layer], ln1_bias[layer], proj_tm)
        h = _ffn(h, w_ffn_up[layer], w_ffn_down[layer], ln2_gain[layer], ln2_bias[layer], ffn_tm, ffn_tf)
    return h.reshape(bsz, seq_len, D_MODEL)
```

```python
import functools

import jax
import jax.numpy as jnp
from jax import lax
from jax.experimental import pallas as pl
from jax.experimental.pallas import tpu as pltpu

D_MODEL = 1024
DEPTH = 2
GRID_W = 64
HEAD_DIM = 64
MIX_WIDTH = 512
ATTN_Q_HEADS = 8
ATTN_KV_HEADS = 2
ATTN_GROUP = ATTN_Q_HEADS // ATTN_KV_HEADS
NAT_HEADS = 8
NAT_WIN_ROWS = 8
NAT_WIN_COLS = 16
SSM_GROUP = 16
SSM_GROUPS = 32
SSM_STATE = 64
SSM_CHUNK = 16
N_BRANCHES = 3
FFN_DIM = 4 * D_MODEL
ROPE_THETA = 10000.0
LN_EPS = 1e-5
RMS_EPS = 1e-6
DEEPNORM_ALPHA = (2 * DEPTH) ** 0.25
ATTN_SCALE = HEAD_DIM ** -0.5
LOG2_E = 1.4426950408889634
MASK_VALUE = -1e30

Q_WIDTH = ATTN_Q_HEADS * HEAD_DIM
KV_WIDTH = ATTN_KV_HEADS * HEAD_DIM
GATE_WIDTH = N_BRANCHES * D_MODEL

LANES = 128
SUBLANES = 8
MXU_WIDTH = 256
V7X_VMEM_BYTES = 64 * 1024 * 1024
VMEM_LIMIT = V7X_VMEM_BYTES - 8 * 1024 * 1024

F32 = jnp.float32
BF16 = jnp.bfloat16
NT_DIMS = (((1,), (1,)), ((), ()))


def _params(*semantics):
    return pltpu.CompilerParams(dimension_semantics=semantics, vmem_limit_bytes=VMEM_LIMIT)


def _dot(a, b):
    return jnp.dot(a, b, preferred_element_type=F32)


def _layer_norm(x, gain, bias):
    mu = jnp.mean(x, axis=-1, keepdims=True)
    xc = x - mu
    var = jnp.mean(xc * xc, axis=-1, keepdims=True)
    return xc * lax.rsqrt(var + LN_EPS) * gain + bias


def _resident(a):
    return pl.BlockSpec(a.shape, lambda *_: (0,) * a.ndim, pipeline_mode=pl.Buffered(1))


ATTN_KV_PAGE = 512
ATTN_TQ = MXU_WIDTH
ATTN_EXT_DIM = LANES
ATTN_V_ROWS = HEAD_DIM + 16
ATTN_BOUND_LIMIT = 60.0
PROJ_T_ROWS = Q_WIDTH + KV_WIDTH + MIX_WIDTH
PROJ_N_COLS = KV_WIDTH + 3 * MIX_WIDTH + GATE_WIDTH


def _proj_kernel(x_ref, wt_ref, wn_ref, qgain_ref, kgain_ref, cost_ref, sint_ref, cos_ref, sin_ref, seg_ref,
                 qext_ref, vext_ref, qt_ref, vt_ref, nkt_ref, k_ref, nq_ref, nv_ref, su_ref, g_ref):
    tm = x_ref.shape[0]
    d = HEAD_DIM
    xb = x_ref[...].astype(BF16)

    yt = lax.dot_general(wt_ref[...], xb, NT_DIMS, preferred_element_type=F32)
    cost = cost_ref[...]
    sint = sint_ref[...]
    qgain = qgain_ref[...]
    for h in range(ATTN_Q_HEADS):
        blk = yt[d * h:d * (h + 1), :]
        ms = jnp.mean(blk * blk, axis=0, keepdims=True)
        yn = blk * lax.rsqrt(ms + RMS_EPS) * qgain
        partner = jnp.concatenate([yn[16:32], yn[0:16], yn[48:64], yn[32:48]], axis=0)
        qh = ((yn * cost + partner * sint) * (ATTN_SCALE * LOG2_E)).astype(BF16)
        j, g = divmod(h, ATTN_GROUP)
        for qb in range(tm // ATTN_TQ):
            qt_ref[j, qb, 0:d, ATTN_TQ * g:ATTN_TQ * (g + 1)] = qh[:, ATTN_TQ * qb:ATTN_TQ * (qb + 1)]
    for j in range(ATTN_KV_HEADS):
        for qb in range(tm // ATTN_TQ):
            qt_ref[j, qb, d:ATTN_EXT_DIM, :] = qext_ref[...]
    vt = yt[Q_WIDTH:Q_WIDTH + KV_WIDTH, :].astype(BF16)
    for pg in range(tm // ATTN_KV_PAGE):
        cols = slice(ATTN_KV_PAGE * pg, ATTN_KV_PAGE * (pg + 1))
        for j in range(ATTN_KV_HEADS):
            vt_ref[pg, ATTN_V_ROWS * j:ATTN_V_ROWS * j + d, :] = vt[d * j:d * (j + 1), cols]
            vt_ref[pg, ATTN_V_ROWS * j + d:ATTN_V_ROWS * (j + 1), :] = vext_ref[...]
    nkt_ref[...] = yt[Q_WIDTH + KV_WIDTH:, :].astype(BF16)

    yk = _dot(xb, wn_ref[:, 0:KV_WIDTH])
    y2 = yk * yk
    hi = y2.astype(BF16)
    lo = (y2 - hi.astype(F32)).astype(BF16)
    ms = (_dot(hi, seg_ref[...]) + _dot(lo, seg_ref[...])) * (1.0 / d)
    kn = yk * lax.rsqrt(ms + RMS_EPS) * kgain_ref[...]
    lane = lax.broadcasted_iota(jnp.int32, (tm, LANES), 1)
    partner = jnp.where((lane % 32) < 16, pltpu.roll(kn, LANES - 16, 1), pltpu.roll(kn, 16, 1))
    kk = kn * cos_ref[...] + partner * sin_ref[...]
    one_hot = (lane == d).astype(F32)
    for j in range(ATTN_KV_HEADS):
        kj = kk if j == 0 else pltpu.roll(kk, LANES - d * j, 1)
        k_ref[j] = jnp.where(lane < d, kj, one_hot).astype(BF16)

    c0 = KV_WIDTH
    ynq = _dot(xb, wn_ref[:, c0:c0 + MIX_WIDTH]) * ATTN_SCALE
    ynv = _dot(xb, wn_ref[:, c0 + MIX_WIDTH:c0 + 2 * MIX_WIDTH])
    for h in range(NAT_HEADS):
        nq_ref[h] = ynq[:, d * h:d * (h + 1)].astype(BF16)
        nv_ref[h] = ynv[:, d * h:d * (h + 1)].astype(BF16)
    su_ref[...] = _dot(xb, wn_ref[:, c0 + 2 * MIX_WIDTH:c0 + 3 * MIX_WIDTH]).astype(BF16)
    c1 = c0 + 3 * MIX_WIDTH
    for n in range(N_BRANCHES):
        y = _dot(xb, wn_ref[:, c1 + D_MODEL * n:c1 + D_MODEL * (n + 1)])
        g_ref[:, D_MODEL * n:D_MODEL * (n + 1)] = jax.nn.sigmoid(y).astype(BF16)


def _rope_tables(seq_len):
    t = jnp.arange(seq_len)
    row = (t // GRID_W).astype(F32)
    col = (t % GRID_W).astype(F32)
    axis_dim = HEAD_DIM // 2
    inv_freq = 1.0 / (ROPE_THETA ** (jnp.arange(0, axis_dim, 2, dtype=F32) / axis_dim))
    ang_r = row[:, None] * inv_freq[None, :]
    ang_c = col[:, None] * inv_freq[None, :]
    cos_head = jnp.concatenate([jnp.cos(ang_r), jnp.cos(ang_r), jnp.cos(ang_c), jnp.cos(ang_c)], axis=1)
    sin_head = jnp.concatenate([-jnp.sin(ang_r), jnp.sin(ang_r), -jnp.sin(ang_c), jnp.sin(ang_c)], axis=1)
    reps = LANES // HEAD_DIM
    return (jnp.tile(cos_head, (1, reps)), jnp.tile(sin_head, (1, reps)), cos_head.T, sin_head.T)


def _input_projections(x2d, w_in, q_gain, k_gain, rope, bsz, seq_len, tm):
    m = x2d.shape[0]
    nl = seq_len // tm
    pages = tm // ATTN_KV_PAGE
    w = w_in.astype(BF16)
    o = [0, Q_WIDTH, Q_WIDTH + KV_WIDTH, Q_WIDTH + 2 * KV_WIDTH]
    o += [o[-1] + MIX_WIDTH * i for i in range(1, 5)]
    aq, ak, av, nq, nk, nv, su, gate = (w[:, a:b] for a, b in zip(o, o[1:] + [w.shape[1]]))
    wt = jnp.concatenate([aq, av, nk], axis=1).T
    wn = jnp.concatenate([ak, nq, nv, su, gate], axis=1)
    qgain = jnp.broadcast_to(q_gain.astype(F32)[:, None], (HEAD_DIM, tm))
    kgain = jnp.tile(k_gain.astype(F32), ATTN_KV_HEADS)[None, :]
    seg = (jnp.arange(KV_WIDTH)[:, None] // HEAD_DIM == jnp.arange(KV_WIDTH)[None, :] // HEAD_DIM).astype(BF16)
    bound = (HEAD_DIM * ATTN_SCALE * LOG2_E * 1.02) * jnp.max(jnp.abs(q_gain.astype(F32))) * jnp.max(jnp.abs(k_gain.astype(F32)))
    first_row = lambda rows, width: (jnp.arange(rows)[:, None] == 0) & (jnp.arange(width)[None, :] >= 0)
    qext = jnp.where(first_row(ATTN_EXT_DIM - HEAD_DIM, ATTN_GROUP * ATTN_TQ), -bound, 0.0).astype(BF16)
    vext = first_row(ATTN_V_ROWS - HEAD_DIM, ATTN_KV_PAGE).astype(BF16)
    cos, sin, cos_t, sin_t = rope
    tok = lambda width: pl.BlockSpec((tm, width), lambda i: (i, 0))
    heads = lambda n, width=HEAD_DIM: pl.BlockSpec((n, tm, width), lambda i: (0, i, 0))
    feat_t = lambda rows: pl.BlockSpec((None, rows, tm), lambda i: (i // nl, 0, i % nl))
    outs = pl.pallas_call(
        _proj_kernel,
        grid=(m // tm,),
        in_specs=[tok(D_MODEL), _resident(wt), _resident(wn), _resident(qgain), _resident(kgain),
                  pl.BlockSpec((HEAD_DIM, tm), lambda i: (0, i % nl)),
                  pl.BlockSpec((HEAD_DIM, tm), lambda i: (0, i % nl)),
                  pl.BlockSpec((tm, LANES), lambda i: (i % nl, 0)),
                  pl.BlockSpec((tm, LANES), lambda i: (i % nl, 0)),
                  _resident(seg), _resident(qext), _resident(vext)],
        out_specs=[pl.BlockSpec((None, ATTN_KV_HEADS, tm // ATTN_TQ, ATTN_EXT_DIM, ATTN_GROUP * ATTN_TQ),
                                lambda i: (i // nl, 0, i % nl, 0, 0)),
                   pl.BlockSpec((None, pages, ATTN_KV_HEADS * ATTN_V_ROWS, ATTN_KV_PAGE),
                                lambda i: (i // nl, i % nl, 0, 0)),
                   feat_t(MIX_WIDTH),
                   heads(ATTN_KV_HEADS, ATTN_EXT_DIM), heads(NAT_HEADS), heads(NAT_HEADS),
                   tok(MIX_WIDTH), tok(GATE_WIDTH)],
        out_shape=[jax.ShapeDtypeStruct((bsz, ATTN_KV_HEADS, seq_len // ATTN_TQ, ATTN_EXT_DIM, ATTN_GROUP * ATTN_TQ), BF16),
                   jax.ShapeDtypeStruct((bsz, seq_len // ATTN_KV_PAGE, ATTN_KV_HEADS * ATTN_V_ROWS, ATTN_KV_PAGE), BF16),
                   jax.ShapeDtypeStruct((bsz, MIX_WIDTH, seq_len), BF16),
                   jax.ShapeDtypeStruct((ATTN_KV_HEADS, m, ATTN_EXT_DIM), BF16),
                   jax.ShapeDtypeStruct((NAT_HEADS, m, HEAD_DIM), BF16),
                   jax.ShapeDtypeStruct((NAT_HEADS, m, HEAD_DIM), BF16),
                   jax.ShapeDtypeStruct((m, MIX_WIDTH), BF16),
                   jax.ShapeDtypeStruct((m, GATE_WIDTH), BF16)],
        compiler_params=_params("parallel"),
        name="in_proj",
    )(x2d, wt, wn, qgain, kgain, cos_t, sin_t, cos, sin, seg, qext, vext)
    return outs, bound


def _finish_attention(o_ref, o):
    o_ref[...] = jnp.concatenate([o[:, ATTN_TQ * g:ATTN_TQ * (g + 1)].T for g in range(ATTN_GROUP)],
                                 axis=1).astype(o_ref.dtype)


def _flash_bounded_kernel(qg_ref, k_ref, vt_ref, o_ref, *, n_pages):
    d = HEAD_DIM
    qg = qg_ref[...]
    acc = jnp.zeros((ATTN_V_ROWS, qg.shape[1]), F32)
    for c in range(n_pages):
        s = _dot(k_ref[ATTN_KV_PAGE * c:ATTN_KV_PAGE * (c + 1), :], qg)
        acc = acc + _dot(vt_ref[c], jnp.exp2(s).astype(BF16))
    _finish_attention(o_ref, acc[:d] / acc[d:d + 1])


def _flash_online_kernel(qg_ref, k_ref, vt_ref, o_ref, acc_scr, *, n_pages):
    d = HEAD_DIM
    qg = qg_ref[...]
    nq = qg.shape[1]
    acc_scr[...] = jnp.zeros_like(acc_scr)

    def scores(c):
        return _dot(k_ref[ATTN_KV_PAGE * c:ATTN_KV_PAGE * (c + 1), :], qg)

    m = jnp.full((1, nq), -jnp.inf, F32)
    l = jnp.zeros((1, nq), F32)
    s_next = scores(0)
    for c in range(n_pages):
        s = s_next
        if c + 1 < n_pages:
            s_next = scores(c + 1)
        m_new = jnp.maximum(m, jnp.max(s, axis=0, keepdims=True))
        alpha = jnp.exp2(m - m_new)
        p = jnp.exp2(s - m_new)
        l = alpha * l + jnp.sum(p, axis=0, keepdims=True)
        acc_scr[...] = alpha * acc_scr[...] + _dot(vt_ref[c, 0:d, :], p.astype(BF16))
        m = m_new
    _finish_attention(o_ref, acc_scr[...] / l)


def _gqa_attention(qg, k, vt, bound, bsz, seq_len):
    d, hk, grp, tq = HEAD_DIM, ATTN_KV_HEADS, ATTN_GROUP, ATTN_TQ
    nq, n_pages = seq_len // tq, seq_len // ATTN_KV_PAGE

    def call(body, scratch):
        return pl.pallas_call(
            functools.partial(body, n_pages=n_pages),
            grid=(bsz, hk, nq),
            in_specs=[pl.BlockSpec((None, None, None, ATTN_EXT_DIM, grp * tq), lambda b, j, i: (b, j, i, 0, 0)),
                      pl.BlockSpec((None, seq_len, ATTN_EXT_DIM), lambda b, j, i: (j, b, 0)),
                      pl.BlockSpec((None, n_pages, ATTN_V_ROWS, ATTN_KV_PAGE), lambda b, j, i: (b, 0, j, 0))],
            out_specs=pl.BlockSpec((tq, grp * d), lambda b, j, i: (b * nq + i, j)),
            out_shape=jax.ShapeDtypeStruct((bsz * seq_len, ATTN_Q_HEADS * d), BF16),
            scratch_shapes=scratch,
            compiler_params=_params("parallel", "parallel", "parallel"),
            name=body.__name__.strip("_").replace("_kernel", ""),
        )(qg, k, vt)

    return lax.cond(bound <= ATTN_BOUND_LIMIT,
                    lambda: call(_flash_bounded_kernel, []),
                    lambda: call(_flash_online_kernel, [pltpu.VMEM((d, grp * tq), F32)]))


NAT_ROWS_PER_STEP = 8
NAT_STEP_TOKENS = NAT_ROWS_PER_STEP * GRID_W
NAT_WIN_TOKENS = NAT_WIN_ROWS * GRID_W


def _nat_kernel(q_ref, ktp_ref, ktc_ref, ktn_ref, vp_ref, vc_ref, vn_ref, bias_ref, o_ref, kt_scr, v_scr, oh_scr):
    g = pl.program_id(1)
    ng = pl.num_programs(1)
    st = NAT_STEP_TOKENS
    d = HEAD_DIM
    kt_scr[:, 0:st] = ktp_ref[...]
    kt_scr[:, st:2 * st] = ktc_ref[...]
    kt_scr[:, 2 * st:3 * st] = ktn_ref[...]
    v_scr[:, 0:st, :] = vp_ref[...]
    v_scr[:, st:2 * st, :] = vc_ref[...]
    v_scr[:, 2 * st:3 * st, :] = vn_ref[...]

    def all_heads(frame_rows, bias_index):
        starts = [fr * GRID_W for fr in frame_rows]

        def head(h, carry):
            q = q_ref[h]
            r0 = pl.multiple_of(h * d, d)
            s = jnp.concatenate(
                [_dot(q[GRID_W * r:GRID_W * (r + 1), :],
                      kt_scr[pl.ds(r0, d), starts[r]:starts[r] + NAT_WIN_TOKENS]) + bias_ref[bias_index[r], h]
                 for r in range(NAT_ROWS_PER_STEP)], axis=0)
            m = jnp.max(s, axis=1, keepdims=True)
            p = jnp.exp(s - m)
            l = jnp.sum(p, axis=1, keepdims=True)
            pb = p.astype(BF16)
            o = jnp.concatenate(
                [_dot(pb[GRID_W * r:GRID_W * (r + 1), :], v_scr[h, starts[r]:starts[r] + NAT_WIN_TOKENS, :])
                 for r in range(NAT_ROWS_PER_STEP)], axis=0)
            oh_scr[h] = o / l
            return carry

        lax.fori_loop(0, NAT_HEADS, head, 0)

    half = NAT_WIN_ROWS // 2
    n = NAT_ROWS_PER_STEP

    @pl.when(g == 0)
    def _():
        all_heads([n + max(r - half, 0) for r in range(n)], [min(r, half) for r in range(n)])

    @pl.when(jnp.logical_and(g > 0, g < ng - 1))
    def _():
        all_heads([r + half for r in range(n)], [half] * n)

    @pl.when(jnp.logical_and(g == ng - 1, g > 0))
    def _():
        all_heads([min(r + half, n) for r in range(n)], [max(r, half) for r in range(n)])

    o_ref[...] = jnp.concatenate([oh_scr[h] for h in range(NAT_HEADS)], axis=1).astype(o_ref.dtype)


def _nat_bias_table(rel_bias):
    cols = jnp.arange(GRID_W)
    col_start = jnp.clip(cols - NAT_WIN_COLS // 2, 0, GRID_W - NAT_WIN_COLS)
    kc = jnp.arange(GRID_W)
    in_win = (kc[None, :] >= col_start[:, None]) & (kc[None, :] < col_start[:, None] + NAT_WIN_COLS)
    col_off = kc[None, :] - cols[:, None] + (NAT_WIN_COLS - 1)
    v = jnp.arange(NAT_WIN_ROWS)
    i = jnp.arange(NAT_WIN_ROWS)
    row_off = i[None, :] - v[:, None] + (NAT_WIN_ROWS - 1)
    row_hot = (row_off[:, :, None] == jnp.arange(2 * NAT_WIN_ROWS - 1)[None, None, :]).astype(F32)
    col_hot = (col_off[:, :, None] == jnp.arange(2 * NAT_WIN_COLS - 1)[None, None, :]).astype(F32)
    b = jnp.einsum('hab,via,ckb->vhcik', rel_bias.astype(F32), row_hot, col_hot, precision=lax.Precision.HIGHEST)
    b = jnp.where(in_win[None, None, :, None, :], b, MASK_VALUE)
    return b.reshape(NAT_WIN_ROWS, NAT_HEADS, GRID_W, NAT_WIN_TOKENS)


def _neighbourhood_attention(nq, nkt, nv, rel_bias, bsz, seq_len):
    h, d, st = NAT_HEADS, HEAD_DIM, NAT_STEP_TOKENS
    ng = seq_len // st
    assert ng >= 2 and seq_len // GRID_W >= NAT_WIN_ROWS
    bias = _nat_bias_table(rel_bias)
    prev = lambda g: jnp.maximum(g - 1, 0)
    nxt = lambda g: jnp.minimum(g + 1, ng - 1)
    same = lambda g: g
    kt_spec = lambda f: pl.BlockSpec((None, h * d, st), lambda b, g: (b, 0, f(g)))
    v_spec = lambda f: pl.BlockSpec((h, st, d), lambda b, g: (0, b * ng + f(g), 0))
    return pl.pallas_call(
        _nat_kernel,
        grid=(bsz, ng),
        in_specs=[v_spec(same), kt_spec(prev), kt_spec(same), kt_spec(nxt),
                  v_spec(prev), v_spec(same), v_spec(nxt), _resident(bias)],
        out_specs=pl.BlockSpec((st, h * d), lambda b, g: (b * ng + g, 0)),
        out_shape=jax.ShapeDtypeStruct((bsz * seq_len, h * d), BF16),
        scratch_shapes=[pltpu.VMEM((h * d, 3 * st), BF16), pltpu.VMEM((h, 3 * st, d), BF16),
                        pltpu.VMEM((h, st, d), F32)],
        compiler_params=_params("parallel", "parallel"),
        name="nat",
    )(nq, nkt, nkt, nkt, nv, nv, nv, bias)


S5_TILE = SSM_CHUNK * SSM_GROUP
S5_STATE_COLS = 4 * SSM_STATE


def _s5_matrices(a_re, a_im, log_dt, b_re, b_im, c_re, c_im, d_skip):
    t_len, hs = SSM_CHUNK, SSM_GROUP
    a_re = a_re.astype(F32)
    a_im = a_im.astype(F32)
    dt = jnp.exp(log_dt.astype(F32))[..., None]
    decay = jnp.exp(a_re * dt)
    phase = a_im * dt
    lam_re = decay * jnp.cos(phase)
    lam_im = decay * jnp.sin(phase)
    den = a_re * a_re + a_im * a_im
    num_re = lam_re - 1.0
    coef_re = (num_re * a_re + lam_im * a_im) / den
    coef_im = (lam_im * a_re - num_re * a_im) / den
    b_re = b_re.astype(F32)[None]
    b_im = b_im.astype(F32)[None]
    bbar_re = coef_re[..., None] * b_re - coef_im[..., None] * b_im
    bbar_im = coef_re[..., None] * b_im + coef_im[..., None] * b_re
    k = jnp.arange(t_len + 1, dtype=F32)[:, None, None, None]
    pow_mag = jnp.exp(k * (a_re * dt)[None])
    pow_re = pow_mag * jnp.cos(k * phase[None])
    pow_im = pow_mag * jnp.sin(k * phase[None])
    c_re = c_re.astype(F32)
    c_im = c_im.astype(F32)

    cl_re = jnp.einsum('dghp,tdgp->tdghp', c_re, pow_re) - jnp.einsum('dghp,tdgp->tdghp', c_im, pow_im)
    cl_im = jnp.einsum('dghp,tdgp->tdghp', c_re, pow_im) + jnp.einsum('dghp,tdgp->tdghp', c_im, pow_re)
    hp = lax.Precision.HIGHEST
    taps = (jnp.einsum('tdgop,dgpi->tdgoi', cl_re, bbar_re, precision=hp)
            - jnp.einsum('tdgop,dgpi->tdgoi', cl_im, bbar_im, precision=hp))
    taps = taps[:t_len]
    fwd = jnp.stack([jnp.concatenate([jnp.zeros_like(taps[:kk, 0]), taps[:t_len - kk, 0]], axis=0)
                     for kk in range(t_len)], axis=0)
    rev = jnp.stack([jnp.concatenate([taps[:kk + 1, 1][::-1], jnp.zeros_like(taps[:t_len - kk - 1, 1])], axis=0)
                     for kk in range(t_len)], axis=0)
    eye_t = jnp.eye(t_len, dtype=F32)[:, :, None, None, None]
    skip = eye_t * (jnp.eye(hs, dtype=F32)[None] * d_skip.astype(F32)[:, :, None])[None, None]
    toep = (fwd + rev + skip).transpose(2, 0, 4, 1, 3)
    toep = toep.reshape(SSM_GROUPS, S5_TILE, S5_TILE)

    def in_mat(pw_re, pw_im, d):
        re = pw_re[..., None] * bbar_re[d][None] - pw_im[..., None] * bbar_im[d][None]
        im = pw_re[..., None] * bbar_im[d][None] + pw_im[..., None] * bbar_re[d][None]
        to = lambda m: m.transpose(1, 0, 3, 2).reshape(SSM_GROUPS, S5_TILE, SSM_STATE)
        return to(re), to(im)
    f_re, f_im = in_mat(pow_re[:t_len, 0][::-1], pow_im[:t_len, 0][::-1], 0)
    r_re, r_im = in_mat(pow_re[:t_len, 1], pow_im[:t_len, 1], 1)
    b_mat = jnp.concatenate([f_re, r_re, f_im, r_im], axis=2)

    def out_mat(cre, cim):
        to = lambda m: m.transpose(1, 3, 0, 2).reshape(SSM_GROUPS, SSM_STATE, S5_TILE)
        return to(cre), to(-cim)
    mf_re, mf_im = out_mat(cl_re[1:t_len + 1, 0], cl_im[1:t_len + 1, 0])
    mr_re, mr_im = out_mat(cl_re[1:t_len + 1, 1][::-1], cl_im[1:t_len + 1, 1][::-1])
    m_mat = jnp.concatenate([mf_re, mr_re, mf_im, mr_im], axis=1)

    lam_t = jnp.concatenate([pow_re[t_len, 0], pow_re[t_len, 1], pow_im[t_len, 0], pow_im[t_len, 1]], axis=1)
    return toep.astype(BF16), b_mat.astype(BF16), m_mat.astype(BF16), lam_t.reshape(1, SSM_GROUPS * S5_STATE_COLS)


def _s5_state_in_kernel(u_ref, b_ref, z_ref):
    z_ref[...] = _dot(u_ref[...], b_ref[...])


def _s5_scan_kernel(z_ref, lam_ref, s_ref, *, n_chunks, bsz, groups):
    assert 2 * bsz == SUBLANES and n_chunks % 2 == 0
    n_tiles = n_chunks // 2
    lane = lax.broadcasted_iota(jnp.int32, (SUBLANES, LANES), 1)
    sub = lax.broadcasted_iota(jnp.int32, (SUBLANES, LANES), 0)
    is_fwd = lane < SSM_STATE
    is_rev = jnp.logical_not(is_fwd)
    first = sub < bsz
    swap = lambda a: pltpu.roll(a, bsz, 0)
    lam = [(jnp.broadcast_to(lam_ref[:, 256 * j:256 * j + LANES], (SUBLANES, LANES)),
            jnp.broadcast_to(lam_ref[:, 256 * j + LANES:256 * (j + 1)], (SUBLANES, LANES))) for j in range(groups)]

    def step(k, carry):
        rf = pl.multiple_of(k * SUBLANES, SUBLANES)
        rr = pl.multiple_of((n_tiles - 1 - k) * SUBLANES, SUBLANES)
        new = []
        for j in range(groups):
            c_re, c_im = carry[j]
            lr, li = lam[j]
            cre = slice(256 * j, 256 * j + LANES)
            cim = slice(256 * j + LANES, 256 * (j + 1))
            w_re = jnp.where(is_fwd, z_ref[pl.ds(rf, SUBLANES), cre], swap(z_ref[pl.ds(rr, SUBLANES), cre]))
            w_im = jnp.where(is_fwd, z_ref[pl.ds(rf, SUBLANES), cim], swap(z_ref[pl.ds(rr, SUBLANES), cim]))
            a_re = swap(lr * c_re - li * c_im + w_re)
            a_im = swap(lr * c_im + li * c_re + w_im)
            b_re = lr * a_re - li * a_im + w_re
            b_im = lr * a_im + li * a_re + w_im
            e_re = jnp.where(first, c_re, a_re)
            e_im = jnp.where(first, c_im, a_im)
            pltpu.store(s_ref.at[pl.ds(rf, SUBLANES), cre], e_re, mask=is_fwd)
            pltpu.store(s_ref.at[pl.ds(rf, SUBLANES), cim], e_im, mask=is_fwd)
            pltpu.store(s_ref.at[pl.ds(rr, SUBLANES), cre], swap(e_re), mask=is_rev)
            pltpu.store(s_ref.at[pl.ds(rr, SUBLANES), cim], swap(e_im), mask=is_rev)
            new.append((jnp.where(first, swap(b_re), b_re), jnp.where(first, swap(b_im), b_im)))
        return tuple(new)

    zero = jnp.zeros((SUBLANES, LANES), F32)
    lax.fori_loop(0, n_tiles, step, tuple((zero, zero) for _ in range(groups)))


def _s5_out_kernel(u_ref, s_ref, t_ref, m_ref, y_ref):
    y = _dot(u_ref[...], t_ref[...]) + _dot(s_ref[...].astype(BF16), m_ref[...])
    y_ref[...] = y.astype(y_ref.dtype)


def _s5_bidirectional(su, mats, bsz, seq_len):
    toep, b_mat, m_mat, lam_t = mats
    g, t_len, hs = SSM_GROUPS, SSM_CHUNK, SSM_GROUP
    n_chunks = seq_len // t_len
    rows = n_chunks * bsz
    u2 = su.reshape(bsz, n_chunks, t_len, g, hs).transpose(3, 1, 0, 2, 4).reshape(g, rows, S5_TILE)
    u_spec = pl.BlockSpec((None, rows, S5_TILE), lambda j: (j, 0, 0))
    w_spec = pl.BlockSpec((None, S5_TILE, S5_TILE), lambda j: (j, 0, 0))
    col_spec = pl.BlockSpec((rows, S5_STATE_COLS), lambda j: (0, j))
    z = pl.pallas_call(
        _s5_state_in_kernel,
        grid=(g,),
        in_specs=[u_spec, w_spec],
        out_specs=col_spec,
        out_shape=jax.ShapeDtypeStruct((rows, g * S5_STATE_COLS), F32),
        compiler_params=_params("parallel"),
        name="s5_state_in",
    )(u2, b_mat)
    gb = 2
    s_prev = pl.pallas_call(
        functools.partial(_s5_scan_kernel, n_chunks=n_chunks, bsz=bsz, groups=gb),
        grid=(g // gb,),
        in_specs=[pl.BlockSpec((rows, gb * S5_STATE_COLS), lambda j: (0, j)),
                  pl.BlockSpec((1, gb * S5_STATE_COLS), lambda j: (0, j))],
        out_specs=pl.BlockSpec((rows, gb * S5_STATE_COLS), lambda j: (0, j)),
        out_shape=jax.ShapeDtypeStruct((rows, g * S5_STATE_COLS), F32),
        compiler_params=_params("parallel"),
        name="s5_scan",
    )(z, lam_t)
    y2 = pl.pallas_call(
        _s5_out_kernel,
        grid=(g,),
        in_specs=[u_spec, col_spec, w_spec, w_spec],
        out_specs=pl.BlockSpec((None, rows, S5_TILE), lambda j: (j, 0, 0)),
        out_shape=jax.ShapeDtypeStruct((g, rows, S5_TILE), BF16),
        compiler_params=_params("parallel"),
        name="s5_out",
    )(u2, s_prev, toep, m_mat)
    y = y2.reshape(g, n_chunks, bsz, t_len, hs).transpose(2, 1, 3, 0, 4)
    return y.reshape(bsz * seq_len, MIX_WIDTH)


def _merge_kernel(x_ref, attn_ref, nat_ref, y_ref, gate_ref, wglu_ref, wb_ref, wout_ref, gain_ref, bias_ref, o_ref):
    z = jax.nn.gelu(y_ref[...].astype(F32))
    ssm = z * jax.nn.sigmoid(_dot(z.astype(BF16), wglu_ref[...]))
    d = D_MODEL
    merged = gate_ref[:, 0:d].astype(F32) * _dot(attn_ref[...], wb_ref[0])
    merged += gate_ref[:, d:2 * d].astype(F32) * _dot(nat_ref[...], wb_ref[1])
    merged += gate_ref[:, 2 * d:3 * d].astype(F32) * _dot(ssm.astype(BF16), wb_ref[2])
    mix = _dot(merged.astype(BF16), wout_ref[...])
    o_ref[...] = _layer_norm(DEEPNORM_ALPHA * x_ref[...] + mix, gain_ref[...], bias_ref[...])


def _merge(x2d, attn_o, nat_o, y_ssm, gates, w_glu, w_branch, w_out, gain, bias, tm):
    m = x2d.shape[0]
    row = lambda width: pl.BlockSpec((tm, width), lambda i: (i, 0))
    w_glu, w_branch, w_out = w_glu.astype(BF16), w_branch.astype(BF16), w_out.astype(BF16)
    gain, bias = gain.astype(F32)[None, :], bias.astype(F32)[None, :]
    return pl.pallas_call(
        _merge_kernel,
        grid=(m // tm,),
        in_specs=[row(D_MODEL), row(MIX_WIDTH), row(MIX_WIDTH), row(MIX_WIDTH), row(GATE_WIDTH),
                  _resident(w_glu), _resident(w_branch), _resident(w_out), _resident(gain), _resident(bias)],
        out_specs=row(D_MODEL),
        out_shape=jax.ShapeDtypeStruct((m, D_MODEL), F32),
        compiler_params=_params("parallel"),
        name="merge",
    )(x2d, attn_o, nat_o, y_ssm, gates, w_glu, w_branch, w_out, gain, bias)


def _ffn_kernel(x_ref, wup_ref, wdown_ref, gain_ref, bias_ref, o_ref, xb_scr, acc_scr):
    f = pl.program_id(1)

    @pl.when(f == 0)
    def _():
        xb_scr[...] = x_ref[...].astype(BF16)
        acc_scr[...] = jnp.zeros_like(acc_scr)

    h = jnp.maximum(_dot(xb_scr[...], wup_ref[...]), 0.0)
    acc_scr[...] += _dot((h * h).astype(BF16), wdown_ref[...])

    @pl.when(f == pl.num_programs(1) - 1)
    def _():
        o_ref[...] = _layer_norm(DEEPNORM_ALPHA * x_ref[...] + acc_scr[...], gain_ref[...], bias_ref[...])


def _ffn(x2d, w_up, w_down, gain, bias, tm, tf):
    m = x2d.shape[0]
    w_up, w_down = w_up.astype(BF16), w_down.astype(BF16)
    gain, bias = gain.astype(F32)[None, :], bias.astype(F32)[None, :]
    vec = pl.BlockSpec((1, D_MODEL), lambda i, f: (0, 0))
    return pl.pallas_call(
        _ffn_kernel,
        grid=(m // tm, FFN_DIM // tf),
        in_specs=[pl.BlockSpec((tm, D_MODEL), lambda i, f: (i, 0)),
                  pl.BlockSpec((D_MODEL, tf), lambda i, f: (0, f)),
                  pl.BlockSpec((tf, D_MODEL), lambda i, f: (f, 0)),
                  vec, vec],
        out_specs=pl.BlockSpec((tm, D_MODEL), lambda i, f: (i, 0)),
        out_shape=jax.ShapeDtypeStruct((m, D_MODEL), F32),
        scratch_shapes=[pltpu.VMEM((tm, D_MODEL), BF16), pltpu.VMEM((tm, D_MODEL), F32)],
        compiler_params=_params("parallel", "arbitrary"),
        name="ffn",
    )(x2d, w_up, w_down, gain, bias)


def _tile_sizes(seq_len):
    proj_tm = min(512, seq_len)
    ffn_tm = min(1024, seq_len)
    ffn_tf = 512
    return proj_tm, ffn_tm, ffn_tf


def kernel(x, w_in, q_norm_gain, k_norm_gain, nat_rel_bias, ssm_a_re, ssm_a_im, ssm_log_dt, ssm_b_re, ssm_b_im, ssm_c_re, ssm_c_im, ssm_d, ssm_w_glu, w_branch, w_out, ln1_gain, ln1_bias, w_ffn_up, w_ffn_down, ln2_gain, ln2_bias):
    bsz, seq_len, _ = x.shape
    proj_tm, ffn_tm, ffn_tf = _tile_sizes(seq_len)
    rope = _rope_tables(seq_len)
    h = x.reshape(bsz * seq_len, D_MODEL)
    for layer in range(w_in.shape[0]):
        (qt, vt, nkt, k, nq, nv, su, gates), bound = _input_projections(
            h, w_in[layer], q_norm_gain[layer], k_norm_gain[layer], rope, bsz, seq_len, proj_tm)
        attn_o = _gqa_attention(qt, k, vt, bound, bsz, seq_len)
        nat_o = _neighbourhood_attention(nq, nkt, nv, nat_rel_bias[layer], bsz, seq_len)
        mats = _s5_matrices(ssm_a_re[layer], ssm_a_im[layer], ssm_log_dt[layer], ssm_b_re[layer], ssm_b_im[layer],
                            ssm_c_re[layer], ssm_c_im[layer], ssm_d[layer])
        y_ssm = _s5_bidirectional(su, mats, bsz, seq_len)
        h = _merge(h, attn_o, nat_o, y_ssm, gates, ssm_w_glu[layer], w_branch[layer], w_out[layer],
                   ln1_gain[layer], ln1_bias[layer], proj_tm)
        h = _ffn(h, w_ffn_up[layer], w_ffn_down[layer], ln2_gain[layer], ln2_bias[layer], ffn_tm, ffn_tf)
    return h.reshape(bsz, seq_len, D_MODEL)
```

```python
import functools

import jax
import jax.numpy as jnp
from jax import lax
from jax.experimental import pallas as pl
from jax.experimental.pallas import tpu as pltpu

D_MODEL = 1024
DEPTH = 2
GRID_W = 64
HEAD_DIM = 64
MIX_WIDTH = 512
ATTN_Q_HEADS = 8
ATTN_KV_HEADS = 2
ATTN_GROUP = ATTN_Q_HEADS // ATTN_KV_HEADS
NAT_HEADS = 8
NAT_WIN_ROWS = 8
NAT_WIN_COLS = 16
SSM_GROUP = 16
SSM_GROUPS = 32
SSM_STATE = 64
SSM_CHUNK = 16
N_BRANCHES = 3
FFN_DIM = 4 * D_MODEL
ROPE_THETA = 10000.0
LN_EPS = 1e-5
RMS_EPS = 1e-6
DEEPNORM_ALPHA = (2 * DEPTH) ** 0.25
ATTN_SCALE = HEAD_DIM ** -0.5
LOG2_E = 1.4426950408889634
MASK_VALUE = -1e30

Q_WIDTH = ATTN_Q_HEADS * HEAD_DIM
KV_WIDTH = ATTN_KV_HEADS * HEAD_DIM
GATE_WIDTH = N_BRANCHES * D_MODEL

LANES = 128
SUBLANES = 8
MXU_WIDTH = 256
V7X_VMEM_BYTES = 64 * 1024 * 1024
VMEM_LIMIT = V7X_VMEM_BYTES - 8 * 1024 * 1024

F32 = jnp.float32
BF16 = jnp.bfloat16
NT_DIMS = (((1,), (1,)), ((), ()))


def _params(*semantics):
    return pltpu.CompilerParams(dimension_semantics=semantics, vmem_limit_bytes=VMEM_LIMIT)


def _dot(a, b):
    return jnp.dot(a, b, preferred_element_type=F32)


def _layer_norm(x, gain, bias):
    mu = jnp.mean(x, axis=-1, keepdims=True)
    xc = x - mu
    var = jnp.mean(xc * xc, axis=-1, keepdims=True)
    return xc * lax.rsqrt(var + LN_EPS) * gain + bias


def _resident(a):
    return pl.BlockSpec(a.shape, lambda *_: (0,) * a.ndim, pipeline_mode=pl.Buffered(1))


ATTN_KV_PAGE = 512
ATTN_TQ = MXU_WIDTH
ATTN_EXT_DIM = LANES
ATTN_V_ROWS = HEAD_DIM + 16
ATTN_BOUND_LIMIT = 60.0
PROJ_T_ROWS = Q_WIDTH + KV_WIDTH + MIX_WIDTH
S5_TILE = SSM_CHUNK * SSM_GROUP
S5_STATE_COLS = 4 * SSM_STATE
S5_SCAN_ROWS = SUBLANES
PROJ_N_COLS = KV_WIDTH + 3 * MIX_WIDTH + GATE_WIDTH


GROUPS_PER_TILE = LANES // SSM_GROUP
CHUNKS_PER_TILE = LANES // SSM_GROUP


def _to_chunk_layout(x_scr, u_ref, n_chunks):
    lane_grp = lax.broadcasted_iota(jnp.int32, (n_chunks, LANES), 1) // SSM_GROUP
    for half in range(SSM_CHUNK // CHUNKS_PER_TILE):
        for q in range(x_scr.shape[0]):
            steps = [x_scr[q, pl.ds(CHUNKS_PER_TILE * half + tp, n_chunks, stride=SSM_CHUNK), :]
                     for tp in range(CHUNKS_PER_TILE)]
            for gm in range(GROUPS_PER_TILE):
                tile = None
                for tp in range(CHUNKS_PER_TILE):
                    shift = (SSM_GROUP * (tp - gm)) % LANES
                    moved = pltpu.roll(steps[tp], shift, 1) if shift else steps[tp]
                    tile = moved if tile is None else jnp.where(lane_grp == tp, moved, tile)
                u_ref[GROUPS_PER_TILE * q + gm, :, LANES * half:LANES * (half + 1)] = tile.astype(u_ref.dtype)


def _from_chunk_layout(y_ref, y_scr, n_chunks):
    lane_grp = lax.broadcasted_iota(jnp.int32, (n_chunks, LANES), 1) // SSM_GROUP
    for half in range(SSM_CHUNK // CHUNKS_PER_TILE):
        for q in range(y_scr.shape[0]):
            groups = [y_ref[GROUPS_PER_TILE * q + gm, :, LANES * half:LANES * (half + 1)].astype(F32)
                      for gm in range(GROUPS_PER_TILE)]
            for tp in range(CHUNKS_PER_TILE):
                tile = None
                for gm in range(GROUPS_PER_TILE):
                    shift = (SSM_GROUP * (gm - tp)) % LANES
                    moved = pltpu.roll(groups[gm], shift, 1) if shift else groups[gm]
                    tile = moved if tile is None else jnp.where(lane_grp == gm, moved, tile)
                y_scr[q, pl.ds(CHUNKS_PER_TILE * half + tp, n_chunks, stride=SSM_CHUNK), :] = tile


def _proj_kernel(x_ref, wt_ref, wn_ref, qgain_ref, kgain_ref, cost_ref, sint_ref, cos_ref, sin_ref, seg_ref,
                 qext_ref, vext_ref, qt_ref, vt_ref, nkt_ref, k_ref, nq_ref, nv_ref, u_ref, g_ref, su_scr):
    tm = x_ref.shape[0]
    d = HEAD_DIM
    xb = x_ref[...].astype(BF16)

    yt = lax.dot_general(wt_ref[...], xb, NT_DIMS, preferred_element_type=F32)
    cost = cost_ref[...]
    sint = sint_ref[...]
    qgain = qgain_ref[...]
    for h in range(ATTN_Q_HEADS):
        blk = yt[d * h:d * (h + 1), :]
        ms = jnp.mean(blk * blk, axis=0, keepdims=True)
        yn = blk * lax.rsqrt(ms + RMS_EPS) * qgain
        partner = jnp.concatenate([yn[16:32], yn[0:16], yn[48:64], yn[32:48]], axis=0)
        qh = ((yn * cost + partner * sint) * (ATTN_SCALE * LOG2_E)).astype(BF16)
        j, g = divmod(h, ATTN_GROUP)
        for qb in range(tm // ATTN_TQ):
            qt_ref[j, qb, 0:d, ATTN_TQ * g:ATTN_TQ * (g + 1)] = qh[:, ATTN_TQ * qb:ATTN_TQ * (qb + 1)]
    for j in range(ATTN_KV_HEADS):
        for qb in range(tm // ATTN_TQ):
            qt_ref[j, qb, d:ATTN_EXT_DIM, :] = qext_ref[...]
    vt = yt[Q_WIDTH:Q_WIDTH + KV_WIDTH, :].astype(BF16)
    for pg in range(tm // ATTN_KV_PAGE):
        cols = slice(ATTN_KV_PAGE * pg, ATTN_KV_PAGE * (pg + 1))
        for j in range(ATTN_KV_HEADS):
            vt_ref[pg, ATTN_V_ROWS * j:ATTN_V_ROWS * j + d, :] = vt[d * j:d * (j + 1), cols]
            vt_ref[pg, ATTN_V_ROWS * j + d:ATTN_V_ROWS * (j + 1), :] = vext_ref[...]
    nkt_ref[...] = yt[Q_WIDTH + KV_WIDTH:, :].astype(BF16)

    yk = _dot(xb, wn_ref[:, 0:KV_WIDTH])
    y2 = yk * yk
    hi = y2.astype(BF16)
    lo = (y2 - hi.astype(F32)).astype(BF16)
    ms = (_dot(hi, seg_ref[...]) + _dot(lo, seg_ref[...])) * (1.0 / d)
    kn = yk * lax.rsqrt(ms + RMS_EPS) * kgain_ref[...]
    lane = lax.broadcasted_iota(jnp.int32, (tm, LANES), 1)
    partner = jnp.where((lane % 32) < 16, pltpu.roll(kn, LANES - 16, 1), pltpu.roll(kn, 16, 1))
    kk = kn * cos_ref[...] + partner * sin_ref[...]
    one_hot = (lane == d).astype(F32)
    for j in range(ATTN_KV_HEADS):
        kj = kk if j == 0 else pltpu.roll(kk, LANES - d * j, 1)
        k_ref[j] = jnp.where(lane < d, kj, one_hot).astype(BF16)

    c0 = KV_WIDTH
    ynq = _dot(xb, wn_ref[:, c0:c0 + MIX_WIDTH]) * ATTN_SCALE
    ynv = _dot(xb, wn_ref[:, c0 + MIX_WIDTH:c0 + 2 * MIX_WIDTH])
    for h in range(NAT_HEADS):
        nq_ref[h] = ynq[:, d * h:d * (h + 1)].astype(BF16)
        nv_ref[h] = ynv[:, d * h:d * (h + 1)].astype(BF16)
    su = _dot(xb, wn_ref[:, c0 + 2 * MIX_WIDTH:c0 + 3 * MIX_WIDTH])
    for q in range(MIX_WIDTH // LANES):
        su_scr[q] = su[:, LANES * q:LANES * (q + 1)]
    _to_chunk_layout(su_scr, u_ref, tm // SSM_CHUNK)
    c1 = c0 + 3 * MIX_WIDTH
    for n in range(N_BRANCHES):
        y = _dot(xb, wn_ref[:, c1 + D_MODEL * n:c1 + D_MODEL * (n + 1)])
        g_ref[:, D_MODEL * n:D_MODEL * (n + 1)] = jax.nn.sigmoid(y).astype(BF16)


def _rope_tables(seq_len):
    t = jnp.arange(seq_len)
    row = (t // GRID_W).astype(F32)
    col = (t % GRID_W).astype(F32)
    axis_dim = HEAD_DIM // 2
    inv_freq = 1.0 / (ROPE_THETA ** (jnp.arange(0, axis_dim, 2, dtype=F32) / axis_dim))
    ang_r = row[:, None] * inv_freq[None, :]
    ang_c = col[:, None] * inv_freq[None, :]
    cos_head = jnp.concatenate([jnp.cos(ang_r), jnp.cos(ang_r), jnp.cos(ang_c), jnp.cos(ang_c)], axis=1)
    sin_head = jnp.concatenate([-jnp.sin(ang_r), jnp.sin(ang_r), -jnp.sin(ang_c), jnp.sin(ang_c)], axis=1)
    reps = LANES // HEAD_DIM
    return (jnp.tile(cos_head, (1, reps)), jnp.tile(sin_head, (1, reps)), cos_head.T, sin_head.T)


def _input_projections(x2d, w_in, q_gain, k_gain, rope, bsz, seq_len, tm):
    m = x2d.shape[0]
    nl = seq_len // tm
    pages = tm // ATTN_KV_PAGE
    w = w_in.astype(BF16)
    o = [0, Q_WIDTH, Q_WIDTH + KV_WIDTH, Q_WIDTH + 2 * KV_WIDTH]
    o += [o[-1] + MIX_WIDTH * i for i in range(1, 5)]
    aq, ak, av, nq, nk, nv, su, gate = (w[:, a:b] for a, b in zip(o, o[1:] + [w.shape[1]]))
    wt = jnp.concatenate([aq, av, nk], axis=1).T
    wn = jnp.concatenate([ak, nq, nv, su, gate], axis=1)
    qgain = jnp.broadcast_to(q_gain.astype(F32)[:, None], (HEAD_DIM, tm))
    kgain = jnp.tile(k_gain.astype(F32), ATTN_KV_HEADS)[None, :]
    seg = (jnp.arange(KV_WIDTH)[:, None] // HEAD_DIM == jnp.arange(KV_WIDTH)[None, :] // HEAD_DIM).astype(BF16)
    bound = (HEAD_DIM * ATTN_SCALE * LOG2_E * 1.02) * jnp.max(jnp.abs(q_gain.astype(F32))) * jnp.max(jnp.abs(k_gain.astype(F32)))
    first_row = lambda rows, width: (jnp.arange(rows)[:, None] == 0) & (jnp.arange(width)[None, :] >= 0)
    qext = jnp.where(first_row(ATTN_EXT_DIM - HEAD_DIM, ATTN_GROUP * ATTN_TQ), -bound, 0.0).astype(BF16)
    vext = first_row(ATTN_V_ROWS - HEAD_DIM, ATTN_KV_PAGE).astype(BF16)
    cos, sin, cos_t, sin_t = rope
    tok = lambda width: pl.BlockSpec((tm, width), lambda i: (i, 0))
    heads = lambda n, width=HEAD_DIM: pl.BlockSpec((n, tm, width), lambda i: (0, i, 0))
    feat_t = lambda rows: pl.BlockSpec((None, rows, tm), lambda i: (i // nl, 0, i % nl))
    outs = pl.pallas_call(
        _proj_kernel,
        grid=(m // tm,),
        in_specs=[tok(D_MODEL), _resident(wt), _resident(wn), _resident(qgain), _resident(kgain),
                  pl.BlockSpec((HEAD_DIM, tm), lambda i: (0, i % nl)),
                  pl.BlockSpec((HEAD_DIM, tm), lambda i: (0, i % nl)),
                  pl.BlockSpec((tm, LANES), lambda i: (i % nl, 0)),
                  pl.BlockSpec((tm, LANES), lambda i: (i % nl, 0)),
                  _resident(seg), _resident(qext), _resident(vext)],
        out_specs=[pl.BlockSpec((None, ATTN_KV_HEADS, tm // ATTN_TQ, ATTN_EXT_DIM, ATTN_GROUP * ATTN_TQ),
                                lambda i: (i // nl, 0, i % nl, 0, 0)),
                   pl.BlockSpec((None, pages, ATTN_KV_HEADS * ATTN_V_ROWS, ATTN_KV_PAGE),
                                lambda i: (i // nl, i % nl, 0, 0)),
                   feat_t(MIX_WIDTH),
                   heads(ATTN_KV_HEADS, ATTN_EXT_DIM), heads(NAT_HEADS), heads(NAT_HEADS),
                   pl.BlockSpec((SSM_GROUPS, tm // SSM_CHUNK, S5_TILE), lambda i: (0, i % nl, i // nl)),
                   tok(GATE_WIDTH)],
        out_shape=[jax.ShapeDtypeStruct((bsz, ATTN_KV_HEADS, seq_len // ATTN_TQ, ATTN_EXT_DIM, ATTN_GROUP * ATTN_TQ), BF16),
                   jax.ShapeDtypeStruct((bsz, seq_len // ATTN_KV_PAGE, ATTN_KV_HEADS * ATTN_V_ROWS, ATTN_KV_PAGE), BF16),
                   jax.ShapeDtypeStruct((bsz, MIX_WIDTH, seq_len), BF16),
                   jax.ShapeDtypeStruct((ATTN_KV_HEADS, m, ATTN_EXT_DIM), BF16),
                   jax.ShapeDtypeStruct((NAT_HEADS, m, HEAD_DIM), BF16),
                   jax.ShapeDtypeStruct((NAT_HEADS, m, HEAD_DIM), BF16),
                   jax.ShapeDtypeStruct((SSM_GROUPS, seq_len // SSM_CHUNK, bsz * S5_TILE), BF16),
                   jax.ShapeDtypeStruct((m, GATE_WIDTH), BF16)],
        scratch_shapes=[pltpu.VMEM((MIX_WIDTH // LANES, tm, LANES), F32)],
        compiler_params=_params("parallel"),
        name="in_proj",
    )(x2d, wt, wn, qgain, kgain, cos_t, sin_t, cos, sin, seg, qext, vext)
    return outs, bound


def _finish_attention(o_ref, o):
    o_ref[...] = jnp.concatenate([o[:, ATTN_TQ * g:ATTN_TQ * (g + 1)].T for g in range(ATTN_GROUP)],
                                 axis=1).astype(o_ref.dtype)


def _flash_bounded_kernel(qg_ref, k_ref, vt_ref, o_ref, *, n_pages):
    d = HEAD_DIM
    qg = qg_ref[...]
    acc = jnp.zeros((ATTN_V_ROWS, qg.shape[1]), F32)
    for c in range(n_pages):
        s = _dot(k_ref[ATTN_KV_PAGE * c:ATTN_KV_PAGE * (c + 1), :], qg)
        acc = acc + _dot(vt_ref[c], jnp.exp2(s).astype(BF16))
    _finish_attention(o_ref, acc[:d] / acc[d:d + 1])


def _flash_online_kernel(qg_ref, k_ref, vt_ref, o_ref, acc_scr, *, n_pages):
    d = HEAD_DIM
    qg = qg_ref[...]
    nq = qg.shape[1]
    acc_scr[...] = jnp.zeros_like(acc_scr)

    def scores(c):
        return _dot(k_ref[ATTN_KV_PAGE * c:ATTN_KV_PAGE * (c + 1), :], qg)

    m = jnp.full((1, nq), -jnp.inf, F32)
    l = jnp.zeros((1, nq), F32)
    s_next = scores(0)
    for c in range(n_pages):
        s = s_next
        if c + 1 < n_pages:
            s_next = scores(c + 1)
        m_new = jnp.maximum(m, jnp.max(s, axis=0, keepdims=True))
        alpha = jnp.exp2(m - m_new)
        p = jnp.exp2(s - m_new)
        l = alpha * l + jnp.sum(p, axis=0, keepdims=True)
        acc_scr[...] = alpha * acc_scr[...] + _dot(vt_ref[c, 0:d, :], p.astype(BF16))
        m = m_new
    _finish_attention(o_ref, acc_scr[...] / l)


def _gqa_attention(qg, k, vt, bound, bsz, seq_len):
    d, hk, grp, tq = HEAD_DIM, ATTN_KV_HEADS, ATTN_GROUP, ATTN_TQ
    nq, n_pages = seq_len // tq, seq_len // ATTN_KV_PAGE

    def call(body, scratch):
        return pl.pallas_call(
            functools.partial(body, n_pages=n_pages),
            grid=(bsz, hk, nq),
            in_specs=[pl.BlockSpec((None, None, None, ATTN_EXT_DIM, grp * tq), lambda b, j, i: (b, j, i, 0, 0)),
                      pl.BlockSpec((None, seq_len, ATTN_EXT_DIM), lambda b, j, i: (j, b, 0)),
                      pl.BlockSpec((None, n_pages, ATTN_V_ROWS, ATTN_KV_PAGE), lambda b, j, i: (b, 0, j, 0))],
            out_specs=pl.BlockSpec((tq, grp * d), lambda b, j, i: (b * nq + i, j)),
            out_shape=jax.ShapeDtypeStruct((bsz * seq_len, ATTN_Q_HEADS * d), BF16),
            scratch_shapes=scratch,
            compiler_params=_params("parallel", "parallel", "parallel"),
            name=body.__name__.strip("_").replace("_kernel", ""),
        )(qg, k, vt)

    return lax.cond(bound <= ATTN_BOUND_LIMIT,
                    lambda: call(_flash_bounded_kernel, []),
                    lambda: call(_flash_online_kernel, [pltpu.VMEM((d, grp * tq), F32)]))


NAT_ROWS_PER_STEP = 8
NAT_STEP_TOKENS = NAT_ROWS_PER_STEP * GRID_W
NAT_WIN_TOKENS = NAT_WIN_ROWS * GRID_W


def _nat_kernel(q_ref, ktp_ref, ktc_ref, ktn_ref, vp_ref, vc_ref, vn_ref, bias_ref, o_ref, kt_scr, v_scr, oh_scr):
    g = pl.program_id(1)
    ng = pl.num_programs(1)
    st = NAT_STEP_TOKENS
    d = HEAD_DIM
    kt_scr[:, 0:st] = ktp_ref[...]
    kt_scr[:, st:2 * st] = ktc_ref[...]
    kt_scr[:, 2 * st:3 * st] = ktn_ref[...]
    v_scr[:, 0:st, :] = vp_ref[...]
    v_scr[:, st:2 * st, :] = vc_ref[...]
    v_scr[:, 2 * st:3 * st, :] = vn_ref[...]

    def all_heads(frame_rows, bias_index):
        starts = [fr * GRID_W for fr in frame_rows]

        def head(h, carry):
            q = q_ref[h]
            r0 = pl.multiple_of(h * d, d)
            s = jnp.concatenate(
                [_dot(q[GRID_W * r:GRID_W * (r + 1), :],
                      kt_scr[pl.ds(r0, d), starts[r]:starts[r] + NAT_WIN_TOKENS]) + bias_ref[bias_index[r], h]
                 for r in range(NAT_ROWS_PER_STEP)], axis=0)
            m = jnp.max(s, axis=1, keepdims=True)
            p = jnp.exp(s - m)
            l = jnp.sum(p, axis=1, keepdims=True)
            pb = p.astype(BF16)
            o = jnp.concatenate(
                [_dot(pb[GRID_W * r:GRID_W * (r + 1), :], v_scr[h, starts[r]:starts[r] + NAT_WIN_TOKENS, :])
                 for r in range(NAT_ROWS_PER_STEP)], axis=0)
            oh_scr[h] = o / l
            return carry

        lax.fori_loop(0, NAT_HEADS, head, 0)

    half = NAT_WIN_ROWS // 2
    n = NAT_ROWS_PER_STEP

    @pl.when(g == 0)
    def _():
        all_heads([n + max(r - half, 0) for r in range(n)], [min(r, half) for r in range(n)])

    @pl.when(jnp.logical_and(g > 0, g < ng - 1))
    def _():
        all_heads([r + half for r in range(n)], [half] * n)

    @pl.when(jnp.logical_and(g == ng - 1, g > 0))
    def _():
        all_heads([min(r + half, n) for r in range(n)], [max(r, half) for r in range(n)])

    o_ref[...] = jnp.concatenate([oh_scr[h] for h in range(NAT_HEADS)], axis=1).astype(o_ref.dtype)


def _nat_bias_table(rel_bias):
    cols = jnp.arange(GRID_W)
    col_start = jnp.clip(cols - NAT_WIN_COLS // 2, 0, GRID_W - NAT_WIN_COLS)
    kc = jnp.arange(GRID_W)
    in_win = (kc[None, :] >= col_start[:, None]) & (kc[None, :] < col_start[:, None] + NAT_WIN_COLS)
    col_off = kc[None, :] - cols[:, None] + (NAT_WIN_COLS - 1)
    v = jnp.arange(NAT_WIN_ROWS)
    i = jnp.arange(NAT_WIN_ROWS)
    row_off = i[None, :] - v[:, None] + (NAT_WIN_ROWS - 1)
    row_hot = (row_off[:, :, None] == jnp.arange(2 * NAT_WIN_ROWS - 1)[None, None, :]).astype(F32)
    col_hot = (col_off[:, :, None] == jnp.arange(2 * NAT_WIN_COLS - 1)[None, None, :]).astype(F32)
    b = jnp.einsum('hab,via,ckb->vhcik', rel_bias.astype(F32), row_hot, col_hot, precision=lax.Precision.HIGHEST)
    b = jnp.where(in_win[None, None, :, None, :], b, MASK_VALUE)
    return b.reshape(NAT_WIN_ROWS, NAT_HEADS, GRID_W, NAT_WIN_TOKENS)


def _neighbourhood_attention(nq, nkt, nv, rel_bias, bsz, seq_len):
    h, d, st = NAT_HEADS, HEAD_DIM, NAT_STEP_TOKENS
    ng = seq_len // st
    assert ng >= 2 and seq_len // GRID_W >= NAT_WIN_ROWS
    bias = _nat_bias_table(rel_bias)
    prev = lambda g: jnp.maximum(g - 1, 0)
    nxt = lambda g: jnp.minimum(g + 1, ng - 1)
    same = lambda g: g
    kt_spec = lambda f: pl.BlockSpec((None, h * d, st), lambda b, g: (b, 0, f(g)))
    v_spec = lambda f: pl.BlockSpec((h, st, d), lambda b, g: (0, b * ng + f(g), 0))
    return pl.pallas_call(
        _nat_kernel,
        grid=(bsz, ng),
        in_specs=[v_spec(same), kt_spec(prev), kt_spec(same), kt_spec(nxt),
                  v_spec(prev), v_spec(same), v_spec(nxt), _resident(bias)],
        out_specs=pl.BlockSpec((st, h * d), lambda b, g: (b * ng + g, 0)),
        out_shape=jax.ShapeDtypeStruct((bsz * seq_len, h * d), BF16),
        scratch_shapes=[pltpu.VMEM((h * d, 3 * st), BF16), pltpu.VMEM((h, 3 * st, d), BF16),
                        pltpu.VMEM((h, st, d), F32)],
        compiler_params=_params("parallel", "parallel"),
        name="nat",
    )(nq, nkt, nkt, nkt, nv, nv, nv, bias)


def _s5_matrices(a_re, a_im, log_dt, b_re, b_im, c_re, c_im, d_skip):
    t_len, hs = SSM_CHUNK, SSM_GROUP
    a_re = a_re.astype(F32)
    a_im = a_im.astype(F32)
    dt = jnp.exp(log_dt.astype(F32))[..., None]
    decay = jnp.exp(a_re * dt)
    phase = a_im * dt
    lam_re = decay * jnp.cos(phase)
    lam_im = decay * jnp.sin(phase)
    den = a_re * a_re + a_im * a_im
    num_re = lam_re - 1.0
    coef_re = (num_re * a_re + lam_im * a_im) / den
    coef_im = (lam_im * a_re - num_re * a_im) / den
    b_re = b_re.astype(F32)[None]
    b_im = b_im.astype(F32)[None]
    bbar_re = coef_re[..., None] * b_re - coef_im[..., None] * b_im
    bbar_im = coef_re[..., None] * b_im + coef_im[..., None] * b_re
    k = jnp.arange(t_len + 1, dtype=F32)[:, None, None, None]
    pow_mag = jnp.exp(k * (a_re * dt)[None])
    pow_re = pow_mag * jnp.cos(k * phase[None])
    pow_im = pow_mag * jnp.sin(k * phase[None])
    c_re = c_re.astype(F32)
    c_im = c_im.astype(F32)

    cl_re = jnp.einsum('dghp,tdgp->tdghp', c_re, pow_re) - jnp.einsum('dghp,tdgp->tdghp', c_im, pow_im)
    cl_im = jnp.einsum('dghp,tdgp->tdghp', c_re, pow_im) + jnp.einsum('dghp,tdgp->tdghp', c_im, pow_re)
    hp = lax.Precision.HIGHEST
    taps = (jnp.einsum('tdgop,dgpi->tdgoi', cl_re, bbar_re, precision=hp)
            - jnp.einsum('tdgop,dgpi->tdgoi', cl_im, bbar_im, precision=hp))
    taps = taps[:t_len]
    fwd = jnp.stack([jnp.concatenate([jnp.zeros_like(taps[:kk, 0]), taps[:t_len - kk, 0]], axis=0)
                     for kk in range(t_len)], axis=0)
    rev = jnp.stack([jnp.concatenate([taps[:kk + 1, 1][::-1], jnp.zeros_like(taps[:t_len - kk - 1, 1])], axis=0)
                     for kk in range(t_len)], axis=0)
    eye_t = jnp.eye(t_len, dtype=F32)[:, :, None, None, None]
    skip = eye_t * (jnp.eye(hs, dtype=F32)[None] * d_skip.astype(F32)[:, :, None])[None, None]
    toep = (fwd + rev + skip).transpose(2, 0, 4, 1, 3)
    toep = toep.reshape(SSM_GROUPS, S5_TILE, S5_TILE)

    def in_mat(pw_re, pw_im, d):
        re = pw_re[..., None] * bbar_re[d][None] - pw_im[..., None] * bbar_im[d][None]
        im = pw_re[..., None] * bbar_im[d][None] + pw_im[..., None] * bbar_re[d][None]
        to = lambda m: m.transpose(1, 0, 3, 2).reshape(SSM_GROUPS, S5_TILE, SSM_STATE)
        return to(re), to(im)
    f_re, f_im = in_mat(pow_re[:t_len, 0][::-1], pow_im[:t_len, 0][::-1], 0)
    r_re, r_im = in_mat(pow_re[:t_len, 1], pow_im[:t_len, 1], 1)
    b_mat = jnp.concatenate([f_re, r_re, f_im, r_im], axis=2)

    def out_mat(cre, cim):
        to = lambda m: m.transpose(1, 3, 0, 2).reshape(SSM_GROUPS, SSM_STATE, S5_TILE)
        return to(cre), to(-cim)
    mf_re, mf_im = out_mat(cl_re[1:t_len + 1, 0], cl_im[1:t_len + 1, 0])
    mr_re, mr_im = out_mat(cl_re[1:t_len + 1, 1][::-1], cl_im[1:t_len + 1, 1][::-1])
    m_mat = jnp.concatenate([mf_re, mr_re, mf_im, mr_im], axis=1)

    n = jnp.arange(1, S5_SCAN_ROWS + 1, dtype=F32)[:, None, None, None] * t_len
    cmag = jnp.exp(n * (a_re * dt)[None])
    c_pre = cmag * jnp.cos(n * phase[None])
    c_pim = cmag * jnp.sin(n * phase[None])
    cols = lambda f_idx, r_idx: jnp.concatenate(
        [c_pre[f_idx, 0], c_pre[r_idx, 1], c_pim[f_idx, 0], c_pim[r_idx, 1]], axis=-1)
    doubling = jnp.array([0, 1, 3])
    steps = cols(doubling, doubling)
    rows = jnp.arange(S5_SCAN_ROWS)
    carry = cols(rows, S5_SCAN_ROWS - 1 - rows)
    return toep.astype(BF16), b_mat.astype(BF16), m_mat.astype(BF16), steps, carry


def _s5_state_in_kernel(u_ref, b_ref, z_ref, *, bsz):
    for b in range(bsz):
        cols = slice(S5_TILE * b, S5_TILE * (b + 1))
        z_ref[:, cols] = _dot(u_ref[:, cols], b_ref[...])


def _s5_scan_kernel(z_ref, step_ref, carry_ref, s_ref, *, n_tiles, pairs):
    rows = S5_SCAN_ROWS
    lane = lax.broadcasted_iota(jnp.int32, (rows, LANES), 1)
    sub = lax.broadcasted_iota(jnp.int32, (rows, LANES), 0)
    is_fwd = lane < SSM_STATE
    is_rev = jnp.logical_not(is_fwd)
    both = lambda fwd_rows, rev_rows: jnp.logical_or(jnp.logical_and(is_fwd, fwd_rows),
                                                     jnp.logical_and(is_rev, rev_rows))
    edge = both(sub == 0, sub == rows - 1)

    def upstream(x, dist):
        valid = both(sub >= dist, sub < rows - dist)
        return jnp.where(valid, jnp.where(is_fwd, pltpu.roll(x, dist, 0), pltpu.roll(x, rows - dist, 0)), 0.0)

    def step(k, carry):
        rf = pl.multiple_of(k * rows, rows)
        rr = pl.multiple_of((n_tiles - 1 - k) * rows, rows)
        new = []
        for j in range(pairs):
            cre = slice(S5_STATE_COLS * j, S5_STATE_COLS * j + LANES)
            cim = slice(S5_STATE_COLS * j + LANES, S5_STATE_COLS * (j + 1))
            c_re, c_im = carry[j]
            x_re = jnp.where(is_fwd, z_ref[pl.ds(rf, rows), cre], z_ref[pl.ds(rr, rows), cre])
            x_im = jnp.where(is_fwd, z_ref[pl.ds(rf, rows), cim], z_ref[pl.ds(rr, rows), cim])
            for i, dist in enumerate((1, 2, 4)):
                lr = step_ref[i:i + 1, cre]
                li = step_ref[i:i + 1, cim]
                u_re, u_im = upstream(x_re, dist), upstream(x_im, dist)
                x_re, x_im = x_re + lr * u_re - li * u_im, x_im + lr * u_im + li * u_re
            pr, pi = carry_ref[:, cre], carry_ref[:, cim]
            a_re = x_re + pr * c_re - pi * c_im
            a_im = x_im + pr * c_im + pi * c_re
            e_re = jnp.where(edge, c_re, upstream(a_re, 1))
            e_im = jnp.where(edge, c_im, upstream(a_im, 1))
            pltpu.store(s_ref.at[pl.ds(rf, rows), cre], e_re, mask=is_fwd)
            pltpu.store(s_ref.at[pl.ds(rf, rows), cim], e_im, mask=is_fwd)
            pltpu.store(s_ref.at[pl.ds(rr, rows), cre], e_re, mask=is_rev)
            pltpu.store(s_ref.at[pl.ds(rr, rows), cim], e_im, mask=is_rev)
            last = lambda a: jnp.where(is_fwd, jnp.broadcast_to(a[rows - 1:rows], a.shape),
                                       jnp.broadcast_to(a[0:1], a.shape))
            new.append((last(a_re), last(a_im)))
        return tuple(new)

    zero = jnp.zeros((rows, LANES), F32)
    lax.fori_loop(0, n_tiles, step, tuple((zero, zero) for _ in range(pairs)))


def _s5_out_kernel(u_ref, s_ref, t_ref, m_ref, y_ref, *, bsz):
    for b in range(bsz):
        cols = slice(S5_TILE * b, S5_TILE * (b + 1))
        y = _dot(u_ref[:, cols], t_ref[...]) + _dot(s_ref[:, cols].astype(BF16), m_ref[...])
        y_ref[:, cols] = y.astype(y_ref.dtype)


def _s5_bidirectional(u2, mats, bsz, seq_len):
    toep, b_mat, m_mat, steps, carry = mats
    g = SSM_GROUPS
    n_chunks = seq_len // SSM_CHUNK
    assert n_chunks % S5_SCAN_ROWS == 0
    width = bsz * S5_TILE
    per_batch = lambda t: jnp.tile(t[:, :, None, :], (1, 1, bsz, 1)).reshape(t.shape[0], g * width)
    u_spec = pl.BlockSpec((None, n_chunks, width), lambda j: (j, 0, 0))
    w_spec = pl.BlockSpec((None, S5_TILE, S5_TILE), lambda j: (j, 0, 0))
    col_spec = pl.BlockSpec((n_chunks, width), lambda j: (0, j))
    z = pl.pallas_call(
        functools.partial(_s5_state_in_kernel, bsz=bsz),
        grid=(g,),
        in_specs=[u_spec, w_spec],
        out_specs=col_spec,
        out_shape=jax.ShapeDtypeStruct((n_chunks, g * width), F32),
        compiler_params=_params("parallel"),
        name="s5_state_in",
    )(u2, b_mat)
    s_prev = pl.pallas_call(
        functools.partial(_s5_scan_kernel, n_tiles=n_chunks // S5_SCAN_ROWS, pairs=bsz),
        grid=(g,),
        in_specs=[col_spec, pl.BlockSpec((3, width), lambda j: (0, j)),
                  pl.BlockSpec((S5_SCAN_ROWS, width), lambda j: (0, j))],
        out_specs=col_spec,
        out_shape=jax.ShapeDtypeStruct((n_chunks, g * width), F32),
        compiler_params=_params("parallel"),
        name="s5_scan",
    )(z, per_batch(steps), per_batch(carry))
    return pl.pallas_call(
        functools.partial(_s5_out_kernel, bsz=bsz),
        grid=(g,),
        in_specs=[u_spec, col_spec, w_spec, w_spec],
        out_specs=u_spec,
        out_shape=jax.ShapeDtypeStruct((g, n_chunks, width), BF16),
        compiler_params=_params("parallel"),
        name="s5_out",
    )(u2, s_prev, toep, m_mat)


def _merge_kernel(x_ref, attn_ref, nat_ref, y_ref, gate_ref, wglu_ref, wb_ref, wout_ref, gain_ref, bias_ref, o_ref,
                  y_scr):
    _from_chunk_layout(y_ref, y_scr, x_ref.shape[0] // SSM_CHUNK)
    z = jax.nn.gelu(jnp.concatenate([y_scr[q] for q in range(y_scr.shape[0])], axis=1))
    ssm = z * jax.nn.sigmoid(_dot(z.astype(BF16), wglu_ref[...]))
    d = D_MODEL
    merged = gate_ref[:, 0:d].astype(F32) * _dot(attn_ref[...], wb_ref[0])
    merged += gate_ref[:, d:2 * d].astype(F32) * _dot(nat_ref[...], wb_ref[1])
    merged += gate_ref[:, 2 * d:3 * d].astype(F32) * _dot(ssm.astype(BF16), wb_ref[2])
    mix = _dot(merged.astype(BF16), wout_ref[...])
    o_ref[...] = _layer_norm(DEEPNORM_ALPHA * x_ref[...] + mix, gain_ref[...], bias_ref[...])


def _merge(x2d, attn_o, nat_o, y_ssm, gates, w_glu, w_branch, w_out, gain, bias, seq_len, tm):
    m = x2d.shape[0]
    nl = seq_len // tm
    row = lambda width: pl.BlockSpec((tm, width), lambda i: (i, 0))
    w_glu, w_branch, w_out = w_glu.astype(BF16), w_branch.astype(BF16), w_out.astype(BF16)
    gain, bias = gain.astype(F32)[None, :], bias.astype(F32)[None, :]
    return pl.pallas_call(
        _merge_kernel,
        grid=(m // tm,),
        in_specs=[row(D_MODEL), row(MIX_WIDTH), row(MIX_WIDTH),
                  pl.BlockSpec((SSM_GROUPS, tm // SSM_CHUNK, S5_TILE), lambda i: (0, i % nl, i // nl)),
                  row(GATE_WIDTH), _resident(w_glu), _resident(w_branch), _resident(w_out), _resident(gain), _resident(bias)],
        out_specs=row(D_MODEL),
        out_shape=jax.ShapeDtypeStruct((m, D_MODEL), F32),
        scratch_shapes=[pltpu.VMEM((MIX_WIDTH // LANES, tm, LANES), F32)],
        compiler_params=_params("parallel"),
        name="merge",
    )(x2d, attn_o, nat_o, y_ssm, gates, w_glu, w_branch, w_out, gain, bias)


def _ffn_kernel(x_ref, wup_ref, wdown_ref, gain_ref, bias_ref, o_ref, xb_scr, acc_scr):
    f = pl.program_id(1)

    @pl.when(f == 0)
    def _():
        xb_scr[...] = x_ref[...].astype(BF16)
        acc_scr[...] = jnp.zeros_like(acc_scr)

    h = jnp.maximum(_dot(xb_scr[...], wup_ref[...]), 0.0)
    acc_scr[...] += _dot((h * h).astype(BF16), wdown_ref[...])

    @pl.when(f == pl.num_programs(1) - 1)
    def _():
        o_ref[...] = _layer_norm(DEEPNORM_ALPHA * x_ref[...] + acc_scr[...], gain_ref[...], bias_ref[...])


def _ffn(x2d, w_up, w_down, gain, bias, tm, tf):
    m = x2d.shape[0]
    w_up, w_down = w_up.astype(BF16), w_down.astype(BF16)
    gain, bias = gain.astype(F32)[None, :], bias.astype(F32)[None, :]
    vec = pl.BlockSpec((1, D_MODEL), lambda i, f: (0, 0))
    return pl.pallas_call(
        _ffn_kernel,
        grid=(m // tm, FFN_DIM // tf),
        in_specs=[pl.BlockSpec((tm, D_MODEL), lambda i, f: (i, 0)),
                  pl.BlockSpec((D_MODEL, tf), lambda i, f: (0, f)),
                  pl.BlockSpec((tf, D_MODEL), lambda i, f: (f, 0)),
                  vec, vec],
        out_specs=pl.BlockSpec((tm, D_MODEL), lambda i, f: (i, 0)),
        out_shape=jax.ShapeDtypeStruct((m, D_MODEL), F32),
        scratch_shapes=[pltpu.VMEM((tm, D_MODEL), BF16), pltpu.VMEM((tm, D_MODEL), F32)],
        compiler_params=_params("parallel", "arbitrary"),
        name="ffn",
    )(x2d, w_up, w_down, gain, bias)


def _tile_sizes(seq_len):
    proj_tm = min(512, seq_len)
    ffn_tm = min(1024, seq_len)
    ffn_tf = 512
    return proj_tm, ffn_tm, ffn_tf


def kernel(x, w_in, q_norm_gain, k_norm_gain, nat_rel_bias, ssm_a_re, ssm_a_im, ssm_log_dt, ssm_b_re, ssm_b_im, ssm_c_re, ssm_c_im, ssm_d, ssm_w_glu, w_branch, w_out, ln1_gain, ln1_bias, w_ffn_up, w_ffn_down, ln2_gain, ln2_bias):
    bsz, seq_len, _ = x.shape
    proj_tm, ffn_tm, ffn_tf = _tile_sizes(seq_len)
    rope = _rope_tables(seq_len)
    h = x.reshape(bsz * seq_len, D_MODEL)
    for layer in range(w_in.shape[0]):
        (qt, vt, nkt, k, nq, nv, su, gates), bound = _input_projections(
            h, w_in[layer], q_norm_gain[layer], k_norm_gain[layer], rope, bsz, seq_len, proj_tm)
        attn_o = _gqa_attention(qt, k, vt, bound, bsz, seq_len)
        nat_o = _neighbourhood_attention(nq, nkt, nv, nat_rel_bias[layer], bsz, seq_len)
        mats = _s5_matrices(ssm_a_re[layer], ssm_a_im[layer], ssm_log_dt[layer], ssm_b_re[layer], ssm_b_im[layer],
                            ssm_c_re[layer], ssm_c_im[layer], ssm_d[layer])
        y_ssm = _s5_bidirectional(su, mats, bsz, seq_len)
        h = _merge(h, attn_o, nat_o, y_ssm, gates, ssm_w_glu[layer], w_branch[layer], w_out[layer],
                   ln1_gain[layer], ln1_bias[layer], seq_len, proj_tm)
        h = _ffn(h, w_ffn_up[layer], w_ffn_down[layer], ln2_gain[layer], ln2_bias[layer], ffn_tm, ffn_tf)
    return h.reshape(bsz, seq_len, D_MODEL)
```

```python
import functools

import jax
import jax.numpy as jnp
from jax import lax
from jax.experimental import pallas as pl
from jax.experimental.pallas import tpu as pltpu

D_MODEL = 1024
DEPTH = 2
GRID_W = 64
HEAD_DIM = 64
MIX_WIDTH = 512
ATTN_Q_HEADS = 8
ATTN_KV_HEADS = 2
ATTN_GROUP = ATTN_Q_HEADS // ATTN_KV_HEADS
NAT_HEADS = 8
NAT_WIN_ROWS = 8
NAT_WIN_COLS = 16
SSM_GROUP = 16
SSM_GROUPS = 32
SSM_STATE = 64
SSM_CHUNK = 16
N_BRANCHES = 3
FFN_DIM = 4 * D_MODEL
ROPE_THETA = 10000.0
LN_EPS = 1e-5
RMS_EPS = 1e-6
DEEPNORM_ALPHA = (2 * DEPTH) ** 0.25
ATTN_SCALE = HEAD_DIM ** -0.5
LOG2_E = 1.4426950408889634
MASK_VALUE = -1e30

Q_WIDTH = ATTN_Q_HEADS * HEAD_DIM
KV_WIDTH = ATTN_KV_HEADS * HEAD_DIM
GATE_WIDTH = N_BRANCHES * D_MODEL

LANES = 128
SUBLANES = 8
MXU_WIDTH = 256
V7X_VMEM_BYTES = 64 * 1024 * 1024
VMEM_LIMIT = V7X_VMEM_BYTES - 8 * 1024 * 1024

F32 = jnp.float32
BF16 = jnp.bfloat16
NT_DIMS = (((1,), (1,)), ((), ()))


def _params(*semantics):
    return pltpu.CompilerParams(dimension_semantics=semantics, vmem_limit_bytes=VMEM_LIMIT)


def _dot(a, b):
    return jnp.dot(a, b, preferred_element_type=F32)


def _layer_norm(x, gain, bias):
    mu = jnp.mean(x, axis=-1, keepdims=True)
    xc = x - mu
    var = jnp.mean(xc * xc, axis=-1, keepdims=True)
    return xc * lax.rsqrt(var + LN_EPS) * gain + bias


def _resident(a):
    return pl.BlockSpec(a.shape, lambda *_: (0,) * a.ndim, pipeline_mode=pl.Buffered(1))


def _resident_layer(a, layer):
    return pl.BlockSpec((None,) + a.shape[1:], lambda *_: (layer,) + (0,) * (a.ndim - 1),
                        pipeline_mode=pl.Buffered(1))


W_IN_OFFSETS = (0, Q_WIDTH, Q_WIDTH + KV_WIDTH, Q_WIDTH + 2 * KV_WIDTH, Q_WIDTH + 2 * KV_WIDTH + MIX_WIDTH,
                Q_WIDTH + 2 * KV_WIDTH + 2 * MIX_WIDTH, Q_WIDTH + 2 * KV_WIDTH + 3 * MIX_WIDTH,
                Q_WIDTH + 2 * KV_WIDTH + 4 * MIX_WIDTH)


ATTN_KV_PAGE = 512
ATTN_TQ = MXU_WIDTH
ATTN_EXT_DIM = LANES
ATTN_V_ROWS = HEAD_DIM + 16
ATTN_BOUND_LIMIT = 60.0
PROJ_T_ROWS = Q_WIDTH + KV_WIDTH + MIX_WIDTH
S5_TILE = SSM_CHUNK * SSM_GROUP
S5_STATE_COLS = 4 * SSM_STATE
S5_SCAN_ROWS = SUBLANES


GROUPS_PER_TILE = LANES // SSM_GROUP
CHUNKS_PER_TILE = LANES // SSM_GROUP


def _to_chunk_layout(x_scr, u_ref, n_chunks):
    lane_grp = lax.broadcasted_iota(jnp.int32, (n_chunks, LANES), 1) // SSM_GROUP
    for half in range(SSM_CHUNK // CHUNKS_PER_TILE):
        for q in range(x_scr.shape[0]):
            steps = [x_scr[q, pl.ds(CHUNKS_PER_TILE * half + tp, n_chunks, stride=SSM_CHUNK), :]
                     for tp in range(CHUNKS_PER_TILE)]
            for gm in range(GROUPS_PER_TILE):
                tile = None
                for tp in range(CHUNKS_PER_TILE):
                    shift = (SSM_GROUP * (tp - gm)) % LANES
                    moved = pltpu.roll(steps[tp], shift, 1) if shift else steps[tp]
                    tile = moved if tile is None else jnp.where(lane_grp == tp, moved, tile)
                u_ref[GROUPS_PER_TILE * q + gm, :, LANES * half:LANES * (half + 1)] = tile.astype(u_ref.dtype)


def _from_chunk_layout(y_ref, y_scr, n_chunks):
    lane_grp = lax.broadcasted_iota(jnp.int32, (n_chunks, LANES), 1) // SSM_GROUP
    for half in range(SSM_CHUNK // CHUNKS_PER_TILE):
        for q in range(y_scr.shape[0]):
            groups = [y_ref[GROUPS_PER_TILE * q + gm, :, LANES * half:LANES * (half + 1)].astype(F32)
                      for gm in range(GROUPS_PER_TILE)]
            for tp in range(CHUNKS_PER_TILE):
                tile = None
                for gm in range(GROUPS_PER_TILE):
                    shift = (SSM_GROUP * (gm - tp)) % LANES
                    moved = pltpu.roll(groups[gm], shift, 1) if shift else groups[gm]
                    tile = moved if tile is None else jnp.where(lane_grp == gm, moved, tile)
                y_scr[q, pl.ds(CHUNKS_PER_TILE * half + tp, n_chunks, stride=SSM_CHUNK), :] = tile


def _proj_kernel(x_ref, wt_ref, w_ref, qgain_ref, kgain_ref, cost_ref, sint_ref, cos_ref, sin_ref, seg_ref,
                 qext_ref, vext_ref, qt_ref, vt_ref, nkt_ref, k_ref, nq_ref, nv_ref, u_ref, g_ref, su_scr):
    tm = x_ref.shape[0]
    d = HEAD_DIM
    xb = x_ref[...].astype(BF16)

    yt = lax.dot_general(wt_ref[...], xb, NT_DIMS, preferred_element_type=F32)
    cost = cost_ref[...]
    sint = sint_ref[...]
    qgain = qgain_ref[...]
    for h in range(ATTN_Q_HEADS):
        blk = yt[d * h:d * (h + 1), :]
        ms = jnp.mean(blk * blk, axis=0, keepdims=True)
        yn = blk * lax.rsqrt(ms + RMS_EPS) * qgain
        partner = jnp.concatenate([yn[16:32], yn[0:16], yn[48:64], yn[32:48]], axis=0)
        qh = ((yn * cost + partner * sint) * (ATTN_SCALE * LOG2_E)).astype(BF16)
        j, g = divmod(h, ATTN_GROUP)
        for qb in range(tm // ATTN_TQ):
            qt_ref[j, qb, 0:d, ATTN_TQ * g:ATTN_TQ * (g + 1)] = qh[:, ATTN_TQ * qb:ATTN_TQ * (qb + 1)]
    for j in range(ATTN_KV_HEADS):
        for qb in range(tm // ATTN_TQ):
            qt_ref[j, qb, d:ATTN_EXT_DIM, :] = qext_ref[...]
    vt = yt[Q_WIDTH:Q_WIDTH + KV_WIDTH, :].astype(BF16)
    for pg in range(tm // ATTN_KV_PAGE):
        cols = slice(ATTN_KV_PAGE * pg, ATTN_KV_PAGE * (pg + 1))
        for j in range(ATTN_KV_HEADS):
            vt_ref[pg, ATTN_V_ROWS * j:ATTN_V_ROWS * j + d, :] = vt[d * j:d * (j + 1), cols]
            vt_ref[pg, ATTN_V_ROWS * j + d:ATTN_V_ROWS * (j + 1), :] = vext_ref[...]
    nkt_ref[...] = yt[Q_WIDTH + KV_WIDTH:, :].astype(BF16)

    o_ak, o_nq, o_nv, o_su, o_gate = (W_IN_OFFSETS[i] for i in (1, 3, 5, 6, 7))
    yk = _dot(xb, w_ref[:, o_ak:o_ak + KV_WIDTH])
    y2 = yk * yk
    hi = y2.astype(BF16)
    lo = (y2 - hi.astype(F32)).astype(BF16)
    ms = (_dot(hi, seg_ref[...]) + _dot(lo, seg_ref[...])) * (1.0 / d)
    kn = yk * lax.rsqrt(ms + RMS_EPS) * kgain_ref[...]
    lane = lax.broadcasted_iota(jnp.int32, (tm, LANES), 1)
    partner = jnp.where((lane % 32) < 16, pltpu.roll(kn, LANES - 16, 1), pltpu.roll(kn, 16, 1))
    kk = kn * cos_ref[...] + partner * sin_ref[...]
    one_hot = (lane == d).astype(F32)
    for j in range(ATTN_KV_HEADS):
        kj = kk if j == 0 else pltpu.roll(kk, LANES - d * j, 1)
        k_ref[j] = jnp.where(lane < d, kj, one_hot).astype(BF16)

    ynq = _dot(xb, w_ref[:, o_nq:o_nq + MIX_WIDTH]) * ATTN_SCALE
    ynv = _dot(xb, w_ref[:, o_nv:o_nv + MIX_WIDTH])
    for h in range(NAT_HEADS):
        nq_ref[h] = ynq[:, d * h:d * (h + 1)].astype(BF16)
        nv_ref[h] = ynv[:, d * h:d * (h + 1)].astype(BF16)
    su = _dot(xb, w_ref[:, o_su:o_su + MIX_WIDTH])
    for q in range(MIX_WIDTH // LANES):
        su_scr[q] = su[:, LANES * q:LANES * (q + 1)]
    _to_chunk_layout(su_scr, u_ref, tm // SSM_CHUNK)
    for n in range(N_BRANCHES):
        y = _dot(xb, w_ref[:, o_gate + D_MODEL * n:o_gate + D_MODEL * (n + 1)])
        g_ref[:, D_MODEL * n:D_MODEL * (n + 1)] = jax.nn.sigmoid(y).astype(BF16)


def _rope_tables(seq_len):
    t = jnp.arange(seq_len)
    row = (t // GRID_W).astype(F32)
    col = (t % GRID_W).astype(F32)
    axis_dim = HEAD_DIM // 2
    inv_freq = 1.0 / (ROPE_THETA ** (jnp.arange(0, axis_dim, 2, dtype=F32) / axis_dim))
    ang_r = row[:, None] * inv_freq[None, :]
    ang_c = col[:, None] * inv_freq[None, :]
    cos_head = jnp.concatenate([jnp.cos(ang_r), jnp.cos(ang_r), jnp.cos(ang_c), jnp.cos(ang_c)], axis=1)
    sin_head = jnp.concatenate([-jnp.sin(ang_r), jnp.sin(ang_r), -jnp.sin(ang_c), jnp.sin(ang_c)], axis=1)
    reps = LANES // HEAD_DIM
    return (jnp.tile(cos_head, (1, reps)), jnp.tile(sin_head, (1, reps)), cos_head.T, sin_head.T)


def _transposed_proj_weights(w_bf):
    o = W_IN_OFFSETS
    parts = [w_bf[:, :, o[0]:o[1]], w_bf[:, :, o[2]:o[3]], w_bf[:, :, o[4]:o[5]]]
    return jnp.concatenate(parts, axis=2).transpose(0, 2, 1)


def _input_projections(x2d, w_bf, wt, layer, q_gain, k_gain, rope, bsz, seq_len, tm):
    m = x2d.shape[0]
    nl = seq_len // tm
    pages = tm // ATTN_KV_PAGE
    qgain = jnp.broadcast_to(q_gain.astype(F32)[:, None], (HEAD_DIM, tm))
    kgain = jnp.tile(k_gain.astype(F32), ATTN_KV_HEADS)[None, :]
    seg = (jnp.arange(KV_WIDTH)[:, None] // HEAD_DIM == jnp.arange(KV_WIDTH)[None, :] // HEAD_DIM).astype(BF16)
    bound = (HEAD_DIM * ATTN_SCALE * LOG2_E * 1.02) * jnp.max(jnp.abs(q_gain.astype(F32))) * jnp.max(jnp.abs(k_gain.astype(F32)))
    first_row = lambda rows, width: (jnp.arange(rows)[:, None] == 0) & (jnp.arange(width)[None, :] >= 0)
    qext = jnp.where(first_row(ATTN_EXT_DIM - HEAD_DIM, ATTN_GROUP * ATTN_TQ), -bound, 0.0).astype(BF16)
    vext = first_row(ATTN_V_ROWS - HEAD_DIM, ATTN_KV_PAGE).astype(BF16)
    cos, sin, cos_t, sin_t = rope
    tok = lambda width: pl.BlockSpec((tm, width), lambda i: (i, 0))
    heads = lambda n, width=HEAD_DIM: pl.BlockSpec((n, tm, width), lambda i: (0, i, 0))
    feat_t = lambda rows: pl.BlockSpec((None, rows, tm), lambda i: (i // nl, 0, i % nl))
    outs = pl.pallas_call(
        _proj_kernel,
        grid=(m // tm,),
        in_specs=[tok(D_MODEL), _resident_layer(wt, layer), _resident_layer(w_bf, layer), _resident(qgain), _resident(kgain),
                  pl.BlockSpec((HEAD_DIM, tm), lambda i: (0, i % nl)),
                  pl.BlockSpec((HEAD_DIM, tm), lambda i: (0, i % nl)),
                  pl.BlockSpec((tm, LANES), lambda i: (i % nl, 0)),
                  pl.BlockSpec((tm, LANES), lambda i: (i % nl, 0)),
                  _resident(seg), _resident(qext), _resident(vext)],
        out_specs=[pl.BlockSpec((None, ATTN_KV_HEADS, tm // ATTN_TQ, ATTN_EXT_DIM, ATTN_GROUP * ATTN_TQ),
                                lambda i: (i // nl, 0, i % nl, 0, 0)),
                   pl.BlockSpec((None, pages, ATTN_KV_HEADS * ATTN_V_ROWS, ATTN_KV_PAGE),
                                lambda i: (i // nl, i % nl, 0, 0)),
                   feat_t(MIX_WIDTH),
                   heads(ATTN_KV_HEADS, ATTN_EXT_DIM), heads(NAT_HEADS), heads(NAT_HEADS),
                   pl.BlockSpec((SSM_GROUPS, tm // SSM_CHUNK, S5_TILE), lambda i: (0, i % nl, i // nl)),
                   tok(GATE_WIDTH)],
        out_shape=[jax.ShapeDtypeStruct((bsz, ATTN_KV_HEADS, seq_len // ATTN_TQ, ATTN_EXT_DIM, ATTN_GROUP * ATTN_TQ), BF16),
                   jax.ShapeDtypeStruct((bsz, seq_len // ATTN_KV_PAGE, ATTN_KV_HEADS * ATTN_V_ROWS, ATTN_KV_PAGE), BF16),
                   jax.ShapeDtypeStruct((bsz, MIX_WIDTH, seq_len), BF16),
                   jax.ShapeDtypeStruct((ATTN_KV_HEADS, m, ATTN_EXT_DIM), BF16),
                   jax.ShapeDtypeStruct((NAT_HEADS, m, HEAD_DIM), BF16),
                   jax.ShapeDtypeStruct((NAT_HEADS, m, HEAD_DIM), BF16),
                   jax.ShapeDtypeStruct((SSM_GROUPS, seq_len // SSM_CHUNK, bsz * S5_TILE), BF16),
                   jax.ShapeDtypeStruct((m, GATE_WIDTH), BF16)],
        scratch_shapes=[pltpu.VMEM((MIX_WIDTH // LANES, tm, LANES), F32)],
        compiler_params=_params("parallel"),
        name="in_proj",
    )(x2d, wt, w_bf, qgain, kgain, cos_t, sin_t, cos, sin, seg, qext, vext)
    return outs, bound


def _finish_attention(o_ref, o):
    o_ref[...] = jnp.concatenate([o[:, ATTN_TQ * g:ATTN_TQ * (g + 1)].T for g in range(ATTN_GROUP)],
                                 axis=1).astype(o_ref.dtype)


def _flash_bounded_kernel(qg_ref, k_ref, vt_ref, o_ref, *, n_pages):
    d = HEAD_DIM
    qg = qg_ref[...]
    acc = jnp.zeros((ATTN_V_ROWS, qg.shape[1]), F32)
    for c in range(n_pages):
        s = _dot(k_ref[ATTN_KV_PAGE * c:ATTN_KV_PAGE * (c + 1), :], qg)
        acc = acc + _dot(vt_ref[c], jnp.exp2(s).astype(BF16))
    _finish_attention(o_ref, acc[:d] / acc[d:d + 1])


def _flash_online_kernel(qg_ref, k_ref, vt_ref, o_ref, acc_scr, *, n_pages):
    d = HEAD_DIM
    qg = qg_ref[...]
    nq = qg.shape[1]
    acc_scr[...] = jnp.zeros_like(acc_scr)

    def scores(c):
        return _dot(k_ref[ATTN_KV_PAGE * c:ATTN_KV_PAGE * (c + 1), :], qg)

    m = jnp.full((1, nq), -jnp.inf, F32)
    l = jnp.zeros((1, nq), F32)
    s_next = scores(0)
    for c in range(n_pages):
        s = s_next
        if c + 1 < n_pages:
            s_next = scores(c + 1)
        m_new = jnp.maximum(m, jnp.max(s, axis=0, keepdims=True))
        alpha = jnp.exp2(m - m_new)
        p = jnp.exp2(s - m_new)
        l = alpha * l + jnp.sum(p, axis=0, keepdims=True)
        acc_scr[...] = alpha * acc_scr[...] + _dot(vt_ref[c, 0:d, :], p.astype(BF16))
        m = m_new
    _finish_attention(o_ref, acc_scr[...] / l)


def _gqa_attention(qg, k, vt, bound, bsz, seq_len):
    d, hk, grp, tq = HEAD_DIM, ATTN_KV_HEADS, ATTN_GROUP, ATTN_TQ
    nq, n_pages = seq_len // tq, seq_len // ATTN_KV_PAGE

    def call(body, scratch):
        return pl.pallas_call(
            functools.partial(body, n_pages=n_pages),
            grid=(bsz, hk, nq),
            in_specs=[pl.BlockSpec((None, None, None, ATTN_EXT_DIM, grp * tq), lambda b, j, i: (b, j, i, 0, 0)),
                      pl.BlockSpec((None, seq_len, ATTN_EXT_DIM), lambda b, j, i: (j, b, 0)),
                      pl.BlockSpec((None, n_pages, ATTN_V_ROWS, ATTN_KV_PAGE), lambda b, j, i: (b, 0, j, 0))],
            out_specs=pl.BlockSpec((tq, grp * d), lambda b, j, i: (b * nq + i, j)),
            out_shape=jax.ShapeDtypeStruct((bsz * seq_len, ATTN_Q_HEADS * d), BF16),
            scratch_shapes=scratch,
            compiler_params=_params("parallel", "parallel", "parallel"),
            name=body.__name__.strip("_").replace("_kernel", ""),
        )(qg, k, vt)

    return lax.cond(bound <= ATTN_BOUND_LIMIT,
                    lambda: call(_flash_bounded_kernel, []),
                    lambda: call(_flash_online_kernel, [pltpu.VMEM((d, grp * tq), F32)]))


NAT_ROWS_PER_STEP = 8
NAT_STEP_TOKENS = NAT_ROWS_PER_STEP * GRID_W
NAT_WIN_TOKENS = NAT_WIN_ROWS * GRID_W


def _nat_kernel(q_ref, ktp_ref, ktc_ref, ktn_ref, vp_ref, vc_ref, vn_ref, bias_ref, o_ref, kt_scr, v_scr, oh_scr):
    g = pl.program_id(1)
    ng = pl.num_programs(1)
    st = NAT_STEP_TOKENS
    d = HEAD_DIM
    kt_scr[:, 0:st] = ktp_ref[...]
    kt_scr[:, st:2 * st] = ktc_ref[...]
    kt_scr[:, 2 * st:3 * st] = ktn_ref[...]
    v_scr[:, 0:st, :] = vp_ref[...]
    v_scr[:, st:2 * st, :] = vc_ref[...]
    v_scr[:, 2 * st:3 * st, :] = vn_ref[...]

    def all_heads(frame_rows, bias_index):
        starts = [fr * GRID_W for fr in frame_rows]

        def head(h, carry):
            q = q_ref[h]
            r0 = pl.multiple_of(h * d, d)
            s = jnp.concatenate(
                [_dot(q[GRID_W * r:GRID_W * (r + 1), :],
                      kt_scr[pl.ds(r0, d), starts[r]:starts[r] + NAT_WIN_TOKENS]) + bias_ref[bias_index[r], h]
                 for r in range(NAT_ROWS_PER_STEP)], axis=0)
            m = jnp.max(s, axis=1, keepdims=True)
            p = jnp.exp(s - m)
            l = jnp.sum(p, axis=1, keepdims=True)
            pb = p.astype(BF16)
            o = jnp.concatenate(
                [_dot(pb[GRID_W * r:GRID_W * (r + 1), :], v_scr[h, starts[r]:starts[r] + NAT_WIN_TOKENS, :])
                 for r in range(NAT_ROWS_PER_STEP)], axis=0)
            oh_scr[h] = o / l
            return carry

        lax.fori_loop(0, NAT_HEADS, head, 0)

    half = NAT_WIN_ROWS // 2
    n = NAT_ROWS_PER_STEP

    @pl.when(g == 0)
    def _():
        all_heads([n + max(r - half, 0) for r in range(n)], [min(r, half) for r in range(n)])

    @pl.when(jnp.logical_and(g > 0, g < ng - 1))
    def _():
        all_heads([r + half for r in range(n)], [half] * n)

    @pl.when(jnp.logical_and(g == ng - 1, g > 0))
    def _():
        all_heads([min(r + half, n) for r in range(n)], [max(r, half) for r in range(n)])

    o_ref[...] = jnp.concatenate([oh_scr[h] for h in range(NAT_HEADS)], axis=1).astype(o_ref.dtype)


def _nat_bias_table(rel_bias):
    cols = jnp.arange(GRID_W)
    col_start = jnp.clip(cols - NAT_WIN_COLS // 2, 0, GRID_W - NAT_WIN_COLS)
    kc = jnp.arange(GRID_W)
    in_win = (kc[None, :] >= col_start[:, None]) & (kc[None, :] < col_start[:, None] + NAT_WIN_COLS)
    col_off = kc[None, :] - cols[:, None] + (NAT_WIN_COLS - 1)
    v = jnp.arange(NAT_WIN_ROWS)
    i = jnp.arange(NAT_WIN_ROWS)
    row_off = i[None, :] - v[:, None] + (NAT_WIN_ROWS - 1)
    row_hot = (row_off[:, :, None] == jnp.arange(2 * NAT_WIN_ROWS - 1)[None, None, :]).astype(F32)
    col_hot = (col_off[:, :, None] == jnp.arange(2 * NAT_WIN_COLS - 1)[None, None, :]).astype(F32)
    b = jnp.einsum('hab,via,ckb->vhcik', rel_bias.astype(F32), row_hot, col_hot, precision=lax.Precision.HIGHEST)
    b = jnp.where(in_win[None, None, :, None, :], b, MASK_VALUE)
    return b.reshape(NAT_WIN_ROWS, NAT_HEADS, GRID_W, NAT_WIN_TOKENS)


def _neighbourhood_attention(nq, nkt, nv, bias, layer, bsz, seq_len):
    h, d, st = NAT_HEADS, HEAD_DIM, NAT_STEP_TOKENS
    ng = seq_len // st
    assert ng >= 2 and seq_len // GRID_W >= NAT_WIN_ROWS
    prev = lambda g: jnp.maximum(g - 1, 0)
    nxt = lambda g: jnp.minimum(g + 1, ng - 1)
    same = lambda g: g
    kt_spec = lambda f: pl.BlockSpec((None, h * d, st), lambda b, g: (b, 0, f(g)))
    v_spec = lambda f: pl.BlockSpec((h, st, d), lambda b, g: (0, b * ng + f(g), 0))
    return pl.pallas_call(
        _nat_kernel,
        grid=(bsz, ng),
        in_specs=[v_spec(same), kt_spec(prev), kt_spec(same), kt_spec(nxt),
                  v_spec(prev), v_spec(same), v_spec(nxt), _resident_layer(bias, layer)],
        out_specs=pl.BlockSpec((st, h * d), lambda b, g: (b * ng + g, 0)),
        out_shape=jax.ShapeDtypeStruct((bsz * seq_len, h * d), BF16),
        scratch_shapes=[pltpu.VMEM((h * d, 3 * st), BF16), pltpu.VMEM((h, 3 * st, d), BF16),
                        pltpu.VMEM((h, st, d), F32)],
        compiler_params=_params("parallel", "parallel"),
        name="nat",
    )(nq, nkt, nkt, nkt, nv, nv, nv, bias)


def _s5_matrices(a_re, a_im, log_dt, b_re, b_im, c_re, c_im, d_skip):
    t_len, hs = SSM_CHUNK, SSM_GROUP
    hp = lax.Precision.HIGHEST
    a_re = a_re.astype(F32)
    a_im = a_im.astype(F32)
    dt = jnp.exp(log_dt.astype(F32))[..., None]
    decay = jnp.exp(a_re * dt)
    phase = a_im * dt
    lam_re = decay * jnp.cos(phase)
    lam_im = decay * jnp.sin(phase)
    den = a_re * a_re + a_im * a_im
    num_re = lam_re - 1.0
    coef_re = (num_re * a_re + lam_im * a_im) / den
    coef_im = (lam_im * a_re - num_re * a_im) / den
    b_re = b_re.astype(F32)[None]
    b_im = b_im.astype(F32)[None]
    bbar_re = coef_re[..., None] * b_re - coef_im[..., None] * b_im
    bbar_im = coef_re[..., None] * b_im + coef_im[..., None] * b_re
    c_re = c_re.astype(F32)
    c_im = c_im.astype(F32)

    def powers(exponents):
        e = jnp.asarray(exponents, F32)[:, None, None, None]
        mag = jnp.exp(e * (a_re * dt)[None])
        return mag * jnp.cos(e * phase[None]), mag * jnp.sin(e * phase[None])

    def c_times(p_re, p_im):
        return (c_re[None] * p_re[:, :, :, None, :] - c_im[None] * p_im[:, :, :, None, :],
                c_re[None] * p_im[:, :, :, None, :] + c_im[None] * p_re[:, :, :, None, :])

    def times_bbar(p_re, p_im):
        return (p_re[..., None] * bbar_re[None] - p_im[..., None] * bbar_im[None],
                p_re[..., None] * bbar_im[None] + p_im[..., None] * bbar_re[None])

    tk = jnp.arange(t_len)
    cl_re, cl_im = c_times(*powers(tk))
    taps = (jnp.einsum('tdgop,dgpi->tdgoi', cl_re, bbar_re, precision=hp)
            - jnp.einsum('tdgop,dgpi->tdgoi', cl_im, bbar_im, precision=hp))
    lag = tk[None, :] - tk[:, None]
    hot_f = (lag[:, :, None] == tk[None, None, :]).astype(F32)
    hot_r = (-lag[:, :, None] == tk[None, None, :]).astype(F32)
    eye_t = jnp.eye(t_len, dtype=F32)[:, :, None, None, None]
    skip = eye_t * (jnp.eye(hs, dtype=F32)[None] * d_skip.astype(F32)[:, :, None])[None, None]
    toep = (jnp.einsum('ktx,xgoi->ktgoi', hot_f, taps[:, 0], precision=hp)
            + jnp.einsum('ktx,xgoi->ktgoi', hot_r, taps[:, 1], precision=hp) + skip)
    toep = toep.transpose(2, 0, 4, 1, 3).reshape(SSM_GROUPS, S5_TILE, S5_TILE)

    to_in = lambda m: m.transpose(1, 0, 3, 2).reshape(SSM_GROUPS, S5_TILE, SSM_STATE)
    f_re, f_im = times_bbar(*powers(t_len - 1 - tk))
    r_re, r_im = times_bbar(*powers(tk))
    b_mat = jnp.concatenate([to_in(f_re[:, 0]), to_in(r_re[:, 1]), to_in(f_im[:, 0]), to_in(r_im[:, 1])], axis=2)

    to_out = lambda m: m.transpose(1, 3, 0, 2).reshape(SSM_GROUPS, SSM_STATE, S5_TILE)
    of_re, of_im = c_times(*powers(tk + 1))
    or_re, or_im = c_times(*powers(t_len - tk))
    m_mat = jnp.concatenate([to_out(of_re[:, 0]), to_out(or_re[:, 1]), to_out(-of_im[:, 0]), to_out(-or_im[:, 1])],
                            axis=1)

    def chunk_powers(n_fwd, n_rev):
        (fr, fi), (rr, ri) = powers(n_fwd * t_len), powers(n_rev * t_len)
        return jnp.concatenate([fr[:, 0], rr[:, 1], fi[:, 0], ri[:, 1]], axis=-1)
    doubling = jnp.array([1, 2, 4])
    steps = chunk_powers(doubling, doubling)
    rows = jnp.arange(S5_SCAN_ROWS)
    carry = chunk_powers(rows + 1, S5_SCAN_ROWS - rows)
    return toep.astype(BF16), b_mat.astype(BF16), m_mat.astype(BF16), steps, carry


def _s5_state_in_kernel(u_ref, b_ref, z_ref, *, bsz):
    for b in range(bsz):
        cols = slice(S5_TILE * b, S5_TILE * (b + 1))
        z_ref[:, cols] = _dot(u_ref[:, cols], b_ref[...])


def _s5_scan_kernel(z_ref, step_ref, carry_ref, s_ref, *, n_tiles, pairs):
    rows = S5_SCAN_ROWS
    lane = lax.broadcasted_iota(jnp.int32, (rows, LANES), 1)
    sub = lax.broadcasted_iota(jnp.int32, (rows, LANES), 0)
    is_fwd = lane < SSM_STATE
    is_rev = jnp.logical_not(is_fwd)
    both = lambda fwd_rows, rev_rows: jnp.logical_or(jnp.logical_and(is_fwd, fwd_rows),
                                                     jnp.logical_and(is_rev, rev_rows))
    edge = both(sub == 0, sub == rows - 1)

    def upstream(x, dist):
        valid = both(sub >= dist, sub < rows - dist)
        return jnp.where(valid, jnp.where(is_fwd, pltpu.roll(x, dist, 0), pltpu.roll(x, rows - dist, 0)), 0.0)

    def step(k, carry):
        rf = pl.multiple_of(k * rows, rows)
        rr = pl.multiple_of((n_tiles - 1 - k) * rows, rows)
        new = []
        for j in range(pairs):
            cre = slice(S5_STATE_COLS * j, S5_STATE_COLS * j + LANES)
            cim = slice(S5_STATE_COLS * j + LANES, S5_STATE_COLS * (j + 1))
            c_re, c_im = carry[j]
            x_re = jnp.where(is_fwd, z_ref[pl.ds(rf, rows), cre], z_ref[pl.ds(rr, rows), cre])
            x_im = jnp.where(is_fwd, z_ref[pl.ds(rf, rows), cim], z_ref[pl.ds(rr, rows), cim])
            for i, dist in enumerate((1, 2, 4)):
                lr = step_ref[i:i + 1, cre]
                li = step_ref[i:i + 1, cim]
                u_re, u_im = upstream(x_re, dist), upstream(x_im, dist)
                x_re, x_im = x_re + lr * u_re - li * u_im, x_im + lr * u_im + li * u_re
            pr, pi = carry_ref[:, cre], carry_ref[:, cim]
            a_re = x_re + pr * c_re - pi * c_im
            a_im = x_im + pr * c_im + pi * c_re
            e_re = jnp.where(edge, c_re, upstream(a_re, 1))
            e_im = jnp.where(edge, c_im, upstream(a_im, 1))
            pltpu.store(s_ref.at[pl.ds(rf, rows), cre], e_re, mask=is_fwd)
            pltpu.store(s_ref.at[pl.ds(rf, rows), cim], e_im, mask=is_fwd)
            pltpu.store(s_ref.at[pl.ds(rr, rows), cre], e_re, mask=is_rev)
            pltpu.store(s_ref.at[pl.ds(rr, rows), cim], e_im, mask=is_rev)
            last = lambda a: jnp.where(is_fwd, jnp.broadcast_to(a[rows - 1:rows], a.shape),
                                       jnp.broadcast_to(a[0:1], a.shape))
            new.append((last(a_re), last(a_im)))
        return tuple(new)

    zero = jnp.zeros((rows, LANES), F32)
    lax.fori_loop(0, n_tiles, step, tuple((zero, zero) for _ in range(pairs)))


def _s5_out_kernel(u_ref, s_ref, t_ref, m_ref, y_ref, *, bsz):
    for b in range(bsz):
        cols = slice(S5_TILE * b, S5_TILE * (b + 1))
        y = _dot(u_ref[:, cols], t_ref[...]) + _dot(s_ref[:, cols].astype(BF16), m_ref[...])
        y_ref[:, cols] = y.astype(y_ref.dtype)


def _s5_bidirectional(u2, mats, layer, bsz, seq_len):
    toep, b_mat, m_mat, steps, carry = mats
    g = SSM_GROUPS
    n_chunks = seq_len // SSM_CHUNK
    assert n_chunks % S5_SCAN_ROWS == 0
    width = bsz * S5_TILE
    u_spec = pl.BlockSpec((None, n_chunks, width), lambda j: (j, 0, 0))
    w_spec = pl.BlockSpec((None, None, S5_TILE, S5_TILE), lambda j: (layer, j, 0, 0))
    col_spec = pl.BlockSpec((n_chunks, width), lambda j: (0, j))
    z = pl.pallas_call(
        functools.partial(_s5_state_in_kernel, bsz=bsz),
        grid=(g,),
        in_specs=[u_spec, w_spec],
        out_specs=col_spec,
        out_shape=jax.ShapeDtypeStruct((n_chunks, g * width), F32),
        compiler_params=_params("parallel"),
        name="s5_state_in",
    )(u2, b_mat)
    s_prev = pl.pallas_call(
        functools.partial(_s5_scan_kernel, n_tiles=n_chunks // S5_SCAN_ROWS, pairs=bsz),
        grid=(g,),
        in_specs=[col_spec, pl.BlockSpec((None, steps.shape[1], width), lambda j: (layer, 0, j)),
                  pl.BlockSpec((None, S5_SCAN_ROWS, width), lambda j: (layer, 0, j))],
        out_specs=col_spec,
        out_shape=jax.ShapeDtypeStruct((n_chunks, g * width), F32),
        compiler_params=_params("parallel"),
        name="s5_scan",
    )(z, steps, carry)
    return pl.pallas_call(
        functools.partial(_s5_out_kernel, bsz=bsz),
        grid=(g,),
        in_specs=[u_spec, col_spec, w_spec, w_spec],
        out_specs=u_spec,
        out_shape=jax.ShapeDtypeStruct((g, n_chunks, width), BF16),
        compiler_params=_params("parallel"),
        name="s5_out",
    )(u2, s_prev, toep, m_mat)


def _merge_kernel(x_ref, attn_ref, nat_ref, y_ref, gate_ref, wglu_ref, wb_ref, wout_ref, gain_ref, bias_ref, o_ref,
                  y_scr):
    _from_chunk_layout(y_ref, y_scr, x_ref.shape[0] // SSM_CHUNK)
    z = jax.nn.gelu(jnp.concatenate([y_scr[q] for q in range(y_scr.shape[0])], axis=1))
    ssm = z * jax.nn.sigmoid(_dot(z.astype(BF16), wglu_ref[...]))
    d = D_MODEL
    merged = gate_ref[:, 0:d].astype(F32) * _dot(attn_ref[...], wb_ref[0])
    merged += gate_ref[:, d:2 * d].astype(F32) * _dot(nat_ref[...], wb_ref[1])
    merged += gate_ref[:, 2 * d:3 * d].astype(F32) * _dot(ssm.astype(BF16), wb_ref[2])
    mix = _dot(merged.astype(BF16), wout_ref[...])
    o_ref[...] = _layer_norm(DEEPNORM_ALPHA * x_ref[...] + mix, gain_ref[...], bias_ref[...])


def _merge(x2d, attn_o, nat_o, y_ssm, gates, w_glu, w_branch, w_out, layer, gain, bias, seq_len, tm):
    m = x2d.shape[0]
    nl = seq_len // tm
    row = lambda width: pl.BlockSpec((tm, width), lambda i: (i, 0))
    gain, bias = gain.astype(F32)[None, :], bias.astype(F32)[None, :]
    return pl.pallas_call(
        _merge_kernel,
        grid=(m // tm,),
        in_specs=[row(D_MODEL), row(MIX_WIDTH), row(MIX_WIDTH),
                  pl.BlockSpec((SSM_GROUPS, tm // SSM_CHUNK, S5_TILE), lambda i: (0, i % nl, i // nl)),
                  row(GATE_WIDTH), _resident_layer(w_glu, layer), _resident_layer(w_branch, layer),
                  _resident_layer(w_out, layer), _resident(gain), _resident(bias)],
        out_specs=row(D_MODEL),
        out_shape=jax.ShapeDtypeStruct((m, D_MODEL), F32),
        scratch_shapes=[pltpu.VMEM((MIX_WIDTH // LANES, tm, LANES), F32)],
        compiler_params=_params("parallel"),
        name="merge",
    )(x2d, attn_o, nat_o, y_ssm, gates, w_glu, w_branch, w_out, gain, bias)


def _ffn_kernel(x_ref, wup_ref, wdown_ref, gain_ref, bias_ref, o_ref, xb_scr, acc_scr):
    f = pl.program_id(1)

    @pl.when(f == 0)
    def _():
        xb_scr[...] = x_ref[...].astype(BF16)
        acc_scr[...] = jnp.zeros_like(acc_scr)

    h = jnp.maximum(_dot(xb_scr[...], wup_ref[...]), 0.0)
    acc_scr[...] += _dot((h * h).astype(BF16), wdown_ref[...])

    @pl.when(f == pl.num_programs(1) - 1)
    def _():
        o_ref[...] = _layer_norm(DEEPNORM_ALPHA * x_ref[...] + acc_scr[...], gain_ref[...], bias_ref[...])


def _ffn(x2d, w_up, w_down, layer, gain, bias, tm, tf):
    m = x2d.shape[0]
    gain, bias = gain.astype(F32)[None, :], bias.astype(F32)[None, :]
    vec = pl.BlockSpec((1, D_MODEL), lambda i, f: (0, 0))
    return pl.pallas_call(
        _ffn_kernel,
        grid=(m // tm, FFN_DIM // tf),
        in_specs=[pl.BlockSpec((tm, D_MODEL), lambda i, f: (i, 0)),
                  pl.BlockSpec((None, D_MODEL, tf), lambda i, f: (layer, 0, f)),
                  pl.BlockSpec((None, tf, D_MODEL), lambda i, f: (layer, f, 0)),
                  vec, vec],
        out_specs=pl.BlockSpec((tm, D_MODEL), lambda i, f: (i, 0)),
        out_shape=jax.ShapeDtypeStruct((m, D_MODEL), F32),
        scratch_shapes=[pltpu.VMEM((tm, D_MODEL), BF16), pltpu.VMEM((tm, D_MODEL), F32)],
        compiler_params=_params("parallel", "arbitrary"),
        name="ffn",
    )(x2d, w_up, w_down, gain, bias)


def _tile_sizes(seq_len):
    proj_tm = min(512, seq_len)
    ffn_tm = min(1024, seq_len)
    ffn_tf = 512
    return proj_tm, ffn_tm, ffn_tf


def kernel(x, w_in, q_norm_gain, k_norm_gain, nat_rel_bias, ssm_a_re, ssm_a_im, ssm_log_dt, ssm_b_re, ssm_b_im, ssm_c_re, ssm_c_im, ssm_d, ssm_w_glu, w_branch, w_out, ln1_gain, ln1_bias, w_ffn_up, w_ffn_down, ln2_gain, ln2_bias):
    bsz, seq_len, _ = x.shape
    proj_tm, ffn_tm, ffn_tf = _tile_sizes(seq_len)
    rope = _rope_tables(seq_len)
    w_bf = w_in.astype(BF16)
    wt = _transposed_proj_weights(w_bf)
    toep, b_mat, m_mat, steps, carry = jax.vmap(_s5_matrices)(
        ssm_a_re, ssm_a_im, ssm_log_dt, ssm_b_re, ssm_b_im, ssm_c_re, ssm_c_im, ssm_d)
    per_batch = lambda t: jnp.tile(t[:, :, :, None, :], (1, 1, 1, bsz, 1)).reshape(t.shape[0], t.shape[1], -1)
    mats = (toep, b_mat, m_mat, per_batch(steps), per_batch(carry))
    nat_bias = jax.vmap(_nat_bias_table)(nat_rel_bias)
    w_glu, w_br, w_o = ssm_w_glu.astype(BF16), w_branch.astype(BF16), w_out.astype(BF16)
    w_up, w_down = w_ffn_up.astype(BF16), w_ffn_down.astype(BF16)
    h = x.reshape(bsz * seq_len, D_MODEL)
    for layer in range(w_in.shape[0]):
        (qt, vt, nkt, k, nq, nv, u2, gates), bound = _input_projections(
            h, w_bf, wt, layer, q_norm_gain[layer], k_norm_gain[layer], rope, bsz, seq_len, proj_tm)
        attn_o = _gqa_attention(qt, k, vt, bound, bsz, seq_len)
        nat_o = _neighbourhood_attention(nq, nkt, nv, nat_bias, layer, bsz, seq_len)
        y2 = _s5_bidirectional(u2, mats, layer, bsz, seq_len)
        h = _merge(h, attn_o, nat_o, y2, gates, w_glu, w_br, w_o, layer, ln1_gain[layer], ln1_bias[layer],
                   seq_len, proj_tm)
        h = _ffn(h, w_up, w_down, layer, ln2_gain[layer], ln2_bias[layer], ffn_tm, ffn_tf)
    return h.reshape(bsz, seq_len, D_MODEL)
```

```python
import functools

import jax
import jax.numpy as jnp
from jax import lax
from jax.experimental import pallas as pl
from jax.experimental.pallas import tpu as pltpu

D_MODEL = 1024
DEPTH = 2
GRID_W = 64
HEAD_DIM = 64
MIX_WIDTH = 512
ATTN_Q_HEADS = 8
ATTN_KV_HEADS = 2
ATTN_GROUP = ATTN_Q_HEADS // ATTN_KV_HEADS
NAT_HEADS = 8
NAT_WIN_ROWS = 8
NAT_WIN_COLS = 16
SSM_GROUP = 16
SSM_GROUPS = 32
SSM_STATE = 64
SSM_CHUNK = 16
N_BRANCHES = 3
FFN_DIM = 4 * D_MODEL
ROPE_THETA = 10000.0
LN_EPS = 1e-5
RMS_EPS = 1e-6
DEEPNORM_ALPHA = (2 * DEPTH) ** 0.25
ATTN_SCALE = HEAD_DIM ** -0.5
LOG2_E = 1.4426950408889634
MASK_VALUE = -1e30

Q_WIDTH = ATTN_Q_HEADS * HEAD_DIM
KV_WIDTH = ATTN_KV_HEADS * HEAD_DIM
GATE_WIDTH = N_BRANCHES * D_MODEL

LANES = 128
SUBLANES = 8
MXU_WIDTH = 256
V7X_VMEM_BYTES = 64 * 1024 * 1024
VMEM_LIMIT = V7X_VMEM_BYTES - 8 * 1024 * 1024

F32 = jnp.float32
BF16 = jnp.bfloat16
NT_DIMS = (((1,), (1,)), ((), ()))


def _params(*semantics):
    return pltpu.CompilerParams(dimension_semantics=semantics, vmem_limit_bytes=VMEM_LIMIT)


def _dot(a, b):
    return jnp.dot(a, b, preferred_element_type=F32)


def _layer_norm(x, gain, bias):
    mu = jnp.mean(x, axis=-1, keepdims=True)
    xc = x - mu
    var = jnp.mean(xc * xc, axis=-1, keepdims=True)
    return xc * lax.rsqrt(var + LN_EPS) * gain + bias


def _resident(a):
    return pl.BlockSpec(a.shape, lambda *_: (0,) * a.ndim, pipeline_mode=pl.Buffered(1))


def _resident_layer(a, layer):
    return pl.BlockSpec((None,) + a.shape[1:], lambda *_: (layer,) + (0,) * (a.ndim - 1),
                        pipeline_mode=pl.Buffered(1))


W_IN_OFFSETS = (0, Q_WIDTH, Q_WIDTH + KV_WIDTH, Q_WIDTH + 2 * KV_WIDTH, Q_WIDTH + 2 * KV_WIDTH + MIX_WIDTH,
                Q_WIDTH + 2 * KV_WIDTH + 2 * MIX_WIDTH, Q_WIDTH + 2 * KV_WIDTH + 3 * MIX_WIDTH,
                Q_WIDTH + 2 * KV_WIDTH + 4 * MIX_WIDTH)


ATTN_KV_PAGE = 512
ATTN_TQ = MXU_WIDTH
ATTN_Q_BLOCKS_PER_STEP = 2
ATTN_EXT_DIM = LANES
ATTN_V_ROWS = HEAD_DIM + 16
ATTN_BOUND_LIMIT = 60.0
PROJ_T_ROWS = Q_WIDTH + KV_WIDTH + MIX_WIDTH
S5_TILE = SSM_CHUNK * SSM_GROUP
S5_STATE_COLS = 4 * SSM_STATE
S5_SCAN_ROWS = SUBLANES


GROUPS_PER_TILE = LANES // SSM_GROUP
CHUNKS_PER_TILE = LANES // SSM_GROUP


def _to_chunk_layout(x_scr, u_ref, n_chunks):
    lane_grp = lax.broadcasted_iota(jnp.int32, (n_chunks, LANES), 1) // SSM_GROUP
    for half in range(SSM_CHUNK // CHUNKS_PER_TILE):
        for q in range(x_scr.shape[0]):
            steps = [x_scr[q, pl.ds(CHUNKS_PER_TILE * half + tp, n_chunks, stride=SSM_CHUNK), :]
                     for tp in range(CHUNKS_PER_TILE)]
            for gm in range(GROUPS_PER_TILE):
                tile = None
                for tp in range(CHUNKS_PER_TILE):
                    shift = (SSM_GROUP * (tp - gm)) % LANES
                    moved = pltpu.roll(steps[tp], shift, 1) if shift else steps[tp]
                    tile = moved if tile is None else jnp.where(lane_grp == tp, moved, tile)
                u_ref[GROUPS_PER_TILE * q + gm, :, LANES * half:LANES * (half + 1)] = tile.astype(u_ref.dtype)


def _from_chunk_layout(y_ref, y_scr, n_chunks):
    lane_grp = lax.broadcasted_iota(jnp.int32, (n_chunks, LANES), 1) // SSM_GROUP
    for half in range(SSM_CHUNK // CHUNKS_PER_TILE):
        for q in range(y_scr.shape[0]):
            groups = [y_ref[GROUPS_PER_TILE * q + gm, :, LANES * half:LANES * (half + 1)].astype(F32)
                      for gm in range(GROUPS_PER_TILE)]
            for tp in range(CHUNKS_PER_TILE):
                tile = None
                for gm in range(GROUPS_PER_TILE):
                    shift = (SSM_GROUP * (gm - tp)) % LANES
                    moved = pltpu.roll(groups[gm], shift, 1) if shift else groups[gm]
                    tile = moved if tile is None else jnp.where(lane_grp == gm, moved, tile)
                y_scr[q, pl.ds(CHUNKS_PER_TILE * half + tp, n_chunks, stride=SSM_CHUNK), :] = tile


def _proj_kernel(x_ref, wt_ref, w_ref, qgain_ref, kgain_ref, cost_ref, sint_ref, cos_ref, sin_ref, seg_ref,
                 qext_ref, vext_ref, qt_ref, vt_ref, nkt_ref, k_ref, nq_ref, nv_ref, u_ref, g_ref, su_scr):
    tm = x_ref.shape[0]
    d = HEAD_DIM
    xb = x_ref[...].astype(BF16)

    yt = lax.dot_general(wt_ref[...], xb, NT_DIMS, preferred_element_type=F32)
    cost = cost_ref[...]
    sint = sint_ref[...]
    qgain = qgain_ref[...]
    for h in range(ATTN_Q_HEADS):
        blk = yt[d * h:d * (h + 1), :]
        ms = jnp.mean(blk * blk, axis=0, keepdims=True)
        yn = blk * lax.rsqrt(ms + RMS_EPS) * qgain
        partner = jnp.concatenate([yn[16:32], yn[0:16], yn[48:64], yn[32:48]], axis=0)
        qh = ((yn * cost + partner * sint) * (ATTN_SCALE * LOG2_E)).astype(BF16)
        j, g = divmod(h, ATTN_GROUP)
        for qb in range(tm // ATTN_TQ):
            qt_ref[j, qb, 0:d, ATTN_TQ * g:ATTN_TQ * (g + 1)] = qh[:, ATTN_TQ * qb:ATTN_TQ * (qb + 1)]
    for j in range(ATTN_KV_HEADS):
        for qb in range(tm // ATTN_TQ):
            qt_ref[j, qb, d:ATTN_EXT_DIM, :] = qext_ref[...]
    vt = yt[Q_WIDTH:Q_WIDTH + KV_WIDTH, :].astype(BF16)
    for pg in range(tm // ATTN_KV_PAGE):
        cols = slice(ATTN_KV_PAGE * pg, ATTN_KV_PAGE * (pg + 1))
        for j in range(ATTN_KV_HEADS):
            vt_ref[pg, ATTN_V_ROWS * j:ATTN_V_ROWS * j + d, :] = vt[d * j:d * (j + 1), cols]
            vt_ref[pg, ATTN_V_ROWS * j + d:ATTN_V_ROWS * (j + 1), :] = vext_ref[...]
    nkt_ref[...] = yt[Q_WIDTH + KV_WIDTH:, :].astype(BF16)

    o_ak, o_nq, o_nv, o_su, o_gate = (W_IN_OFFSETS[i] for i in (1, 3, 5, 6, 7))
    yk = _dot(xb, w_ref[:, o_ak:o_ak + KV_WIDTH])
    y2 = yk * yk
    hi = y2.astype(BF16)
    lo = (y2 - hi.astype(F32)).astype(BF16)
    ms = (_dot(hi, seg_ref[...]) + _dot(lo, seg_ref[...])) * (1.0 / d)
    kn = yk * lax.rsqrt(ms + RMS_EPS) * kgain_ref[...]
    lane = lax.broadcasted_iota(jnp.int32, (tm, LANES), 1)
    partner = jnp.where((lane % 32) < 16, pltpu.roll(kn, LANES - 16, 1), pltpu.roll(kn, 16, 1))
    kk = kn * cos_ref[...] + partner * sin_ref[...]
    one_hot = (lane == d).astype(F32)
    for j in range(ATTN_KV_HEADS):
        kj = kk if j == 0 else pltpu.roll(kk, LANES - d * j, 1)
        k_ref[j] = jnp.where(lane < d, kj, one_hot).astype(BF16)

    ynq = _dot(xb, w_ref[:, o_nq:o_nq + MIX_WIDTH]) * ATTN_SCALE
    ynv = _dot(xb, w_ref[:, o_nv:o_nv + MIX_WIDTH])
    for h in range(NAT_HEADS):
        nq_ref[h] = ynq[:, d * h:d * (h + 1)].astype(BF16)
        nv_ref[h] = ynv[:, d * h:d * (h + 1)].astype(BF16)
    su = _dot(xb, w_ref[:, o_su:o_su + MIX_WIDTH])
    for q in range(MIX_WIDTH // LANES):
        su_scr[q] = su[:, LANES * q:LANES * (q + 1)]
    _to_chunk_layout(su_scr, u_ref, tm // SSM_CHUNK)
    for n in range(N_BRANCHES):
        y = _dot(xb, w_ref[:, o_gate + D_MODEL * n:o_gate + D_MODEL * (n + 1)])
        g_ref[:, D_MODEL * n:D_MODEL * (n + 1)] = jax.nn.sigmoid(y).astype(BF16)


def _rope_tables(seq_len):
    t = jnp.arange(seq_len)
    row = (t // GRID_W).astype(F32)
    col = (t % GRID_W).astype(F32)
    axis_dim = HEAD_DIM // 2
    inv_freq = 1.0 / (ROPE_THETA ** (jnp.arange(0, axis_dim, 2, dtype=F32) / axis_dim))
    ang_r = row[:, None] * inv_freq[None, :]
    ang_c = col[:, None] * inv_freq[None, :]
    cos_head = jnp.concatenate([jnp.cos(ang_r), jnp.cos(ang_r), jnp.cos(ang_c), jnp.cos(ang_c)], axis=1)
    sin_head = jnp.concatenate([-jnp.sin(ang_r), jnp.sin(ang_r), -jnp.sin(ang_c), jnp.sin(ang_c)], axis=1)
    reps = LANES // HEAD_DIM
    return (jnp.tile(cos_head, (1, reps)), jnp.tile(sin_head, (1, reps)), cos_head.T, sin_head.T)


def _transposed_proj_weights(w_bf):
    o = W_IN_OFFSETS
    parts = [w_bf[:, :, o[0]:o[1]], w_bf[:, :, o[2]:o[3]], w_bf[:, :, o[4]:o[5]]]
    return jnp.concatenate(parts, axis=2).transpose(0, 2, 1)


def _input_projections(x2d, w_bf, wt, layer, q_gain, k_gain, rope, bsz, seq_len, tm):
    m = x2d.shape[0]
    nl = seq_len // tm
    pages = tm // ATTN_KV_PAGE
    qgain = jnp.broadcast_to(q_gain.astype(F32)[:, None], (HEAD_DIM, tm))
    kgain = jnp.tile(k_gain.astype(F32), ATTN_KV_HEADS)[None, :]
    seg = (jnp.arange(KV_WIDTH)[:, None] // HEAD_DIM == jnp.arange(KV_WIDTH)[None, :] // HEAD_DIM).astype(BF16)
    bound = (HEAD_DIM * ATTN_SCALE * LOG2_E * 1.02) * jnp.max(jnp.abs(q_gain.astype(F32))) * jnp.max(jnp.abs(k_gain.astype(F32)))
    first_row = lambda rows, width: (jnp.arange(rows)[:, None] == 0) & (jnp.arange(width)[None, :] >= 0)
    qext = jnp.where(first_row(ATTN_EXT_DIM - HEAD_DIM, ATTN_GROUP * ATTN_TQ), -bound, 0.0).astype(BF16)
    vext = first_row(ATTN_V_ROWS - HEAD_DIM, ATTN_KV_PAGE).astype(BF16)
    cos, sin, cos_t, sin_t = rope
    tok = lambda width: pl.BlockSpec((tm, width), lambda i: (i, 0))
    heads = lambda n, width=HEAD_DIM: pl.BlockSpec((n, tm, width), lambda i: (0, i, 0))
    feat_t = lambda rows: pl.BlockSpec((None, rows, tm), lambda i: (i // nl, 0, i % nl))
    outs = pl.pallas_call(
        _proj_kernel,
        grid=(m // tm,),
        in_specs=[tok(D_MODEL), _resident_layer(wt, layer), _resident_layer(w_bf, layer), _resident(qgain), _resident(kgain),
                  pl.BlockSpec((HEAD_DIM, tm), lambda i: (0, i % nl)),
                  pl.BlockSpec((HEAD_DIM, tm), lambda i: (0, i % nl)),
                  pl.BlockSpec((tm, LANES), lambda i: (i % nl, 0)),
                  pl.BlockSpec((tm, LANES), lambda i: (i % nl, 0)),
                  _resident(seg), _resident(qext), _resident(vext)],
        out_specs=[pl.BlockSpec((None, ATTN_KV_HEADS, tm // ATTN_TQ, ATTN_EXT_DIM, ATTN_GROUP * ATTN_TQ),
                                lambda i: (i // nl, 0, i % nl, 0, 0)),
                   pl.BlockSpec((None, pages, ATTN_KV_HEADS * ATTN_V_ROWS, ATTN_KV_PAGE),
                                lambda i: (i // nl, i % nl, 0, 0)),
                   feat_t(MIX_WIDTH),
                   heads(ATTN_KV_HEADS, ATTN_EXT_DIM), heads(NAT_HEADS), heads(NAT_HEADS),
                   pl.BlockSpec((SSM_GROUPS, tm // SSM_CHUNK, S5_TILE), lambda i: (0, i % nl, i // nl)),
                   tok(GATE_WIDTH)],
        out_shape=[jax.ShapeDtypeStruct((bsz, ATTN_KV_HEADS, seq_len // ATTN_TQ, ATTN_EXT_DIM, ATTN_GROUP * ATTN_TQ), BF16),
                   jax.ShapeDtypeStruct((bsz, seq_len // ATTN_KV_PAGE, ATTN_KV_HEADS * ATTN_V_ROWS, ATTN_KV_PAGE), BF16),
                   jax.ShapeDtypeStruct((bsz, MIX_WIDTH, seq_len), BF16),
                   jax.ShapeDtypeStruct((ATTN_KV_HEADS, m, ATTN_EXT_DIM), BF16),
                   jax.ShapeDtypeStruct((NAT_HEADS, m, HEAD_DIM), BF16),
                   jax.ShapeDtypeStruct((NAT_HEADS, m, HEAD_DIM), BF16),
                   jax.ShapeDtypeStruct((SSM_GROUPS, seq_len // SSM_CHUNK, bsz * S5_TILE), BF16),
                   jax.ShapeDtypeStruct((m, GATE_WIDTH), BF16)],
        scratch_shapes=[pltpu.VMEM((MIX_WIDTH // LANES, tm, LANES), F32)],
        compiler_params=_params("parallel"),
        name="in_proj",
    )(x2d, wt, w_bf, qgain, kgain, cos_t, sin_t, cos, sin, seg, qext, vext)
    return outs, bound


def _finish_attention(o_ref, qb, o):
    o_ref[ATTN_TQ * qb:ATTN_TQ * (qb + 1), :] = jnp.concatenate(
        [o[:, ATTN_TQ * g:ATTN_TQ * (g + 1)].T for g in range(ATTN_GROUP)], axis=1).astype(o_ref.dtype)


def _flash_bounded_kernel(qg_ref, k_ref, vt_ref, o_ref, *, n_pages):
    d = HEAD_DIM
    for qb in range(qg_ref.shape[0]):
        qg = qg_ref[qb]
        acc = jnp.zeros((ATTN_V_ROWS, qg.shape[1]), F32)
        for c in range(n_pages):
            s = _dot(k_ref[ATTN_KV_PAGE * c:ATTN_KV_PAGE * (c + 1), :], qg)
            acc = acc + _dot(vt_ref[c], jnp.exp2(s).astype(BF16))
        _finish_attention(o_ref, qb, acc[:d] / acc[d:d + 1])


def _flash_online_kernel(qg_ref, k_ref, vt_ref, o_ref, acc_scr, *, n_pages):
    d = HEAD_DIM
    for qb in range(qg_ref.shape[0]):
        qg = qg_ref[qb]
        nq = qg.shape[1]
        acc_scr[...] = jnp.zeros_like(acc_scr)

        def scores(c):
            return _dot(k_ref[ATTN_KV_PAGE * c:ATTN_KV_PAGE * (c + 1), :], qg)

        m = jnp.full((1, nq), -jnp.inf, F32)
        l = jnp.zeros((1, nq), F32)
        s_next = scores(0)
        for c in range(n_pages):
            s = s_next
            if c + 1 < n_pages:
                s_next = scores(c + 1)
            m_new = jnp.maximum(m, jnp.max(s, axis=0, keepdims=True))
            alpha = jnp.exp2(m - m_new)
            p = jnp.exp2(s - m_new)
            l = alpha * l + jnp.sum(p, axis=0, keepdims=True)
            acc_scr[...] = alpha * acc_scr[...] + _dot(vt_ref[c, 0:d, :], p.astype(BF16))
            m = m_new
        _finish_attention(o_ref, qb, acc_scr[...] / l)


def _gqa_attention(qg, k, vt, bound, bsz, seq_len):
    d, hk, grp, tq = HEAD_DIM, ATTN_KV_HEADS, ATTN_GROUP, ATTN_TQ
    nqb = ATTN_Q_BLOCKS_PER_STEP
    nq, n_pages = seq_len // (tq * nqb), seq_len // ATTN_KV_PAGE

    def call(body, scratch):
        return pl.pallas_call(
            functools.partial(body, n_pages=n_pages),
            grid=(bsz, hk, nq),
            in_specs=[pl.BlockSpec((None, None, nqb, ATTN_EXT_DIM, grp * tq), lambda b, j, i: (b, j, i, 0, 0)),
                      pl.BlockSpec((None, seq_len, ATTN_EXT_DIM), lambda b, j, i: (j, b, 0)),
                      pl.BlockSpec((None, n_pages, ATTN_V_ROWS, ATTN_KV_PAGE), lambda b, j, i: (b, 0, j, 0))],
            out_specs=pl.BlockSpec((tq * nqb, grp * d), lambda b, j, i: (b * nq + i, j)),
            out_shape=jax.ShapeDtypeStruct((bsz * seq_len, ATTN_Q_HEADS * d), BF16),
            scratch_shapes=scratch,
            compiler_params=_params("parallel", "parallel", "parallel"),
            name=body.__name__.strip("_").replace("_kernel", ""),
        )(qg, k, vt)

    return lax.cond(bound <= ATTN_BOUND_LIMIT,
                    lambda: call(_flash_bounded_kernel, []),
                    lambda: call(_flash_online_kernel, [pltpu.VMEM((d, grp * tq), F32)]))


NAT_ROWS_PER_STEP = 8
NAT_HEAD_UNROLL = 4
NAT_STEP_TOKENS = NAT_ROWS_PER_STEP * GRID_W
NAT_WIN_TOKENS = NAT_WIN_ROWS * GRID_W


def _nat_kernel(q_ref, ktp_ref, ktc_ref, ktn_ref, vp_ref, vc_ref, vn_ref, bias_ref, o_ref, kt_scr, v_scr, oh_scr):
    g = pl.program_id(1)
    ng = pl.num_programs(1)
    st = NAT_STEP_TOKENS
    d = HEAD_DIM
    kt_scr[:, 0:st] = ktp_ref[...]
    kt_scr[:, st:2 * st] = ktc_ref[...]
    kt_scr[:, 2 * st:3 * st] = ktn_ref[...]
    v_scr[:, 0:st, :] = vp_ref[...]
    v_scr[:, st:2 * st, :] = vc_ref[...]
    v_scr[:, 2 * st:3 * st, :] = vn_ref[...]

    def all_heads(frame_rows, bias_index):
        starts = [fr * GRID_W for fr in frame_rows]

        def head(h, carry):
            q = q_ref[h]
            r0 = pl.multiple_of(h * d, d)
            s = jnp.concatenate(
                [_dot(q[GRID_W * r:GRID_W * (r + 1), :],
                      kt_scr[pl.ds(r0, d), starts[r]:starts[r] + NAT_WIN_TOKENS]) + bias_ref[bias_index[r], h]
                 for r in range(NAT_ROWS_PER_STEP)], axis=0)
            m = jnp.max(s, axis=1, keepdims=True)
            p = jnp.exp(s - m)
            l = jnp.sum(p, axis=1, keepdims=True)
            pb = p.astype(BF16)
            o = jnp.concatenate(
                [_dot(pb[GRID_W * r:GRID_W * (r + 1), :], v_scr[h, starts[r]:starts[r] + NAT_WIN_TOKENS, :])
                 for r in range(NAT_ROWS_PER_STEP)], axis=0)
            oh_scr[h] = o / l
            return carry

        lax.fori_loop(0, NAT_HEADS, head, 0, unroll=NAT_HEAD_UNROLL)

    half = NAT_WIN_ROWS // 2
    n = NAT_ROWS_PER_STEP

    @pl.when(g == 0)
    def _():
        all_heads([n + max(r - half, 0) for r in range(n)], [min(r, half) for r in range(n)])

    @pl.when(jnp.logical_and(g > 0, g < ng - 1))
    def _():
        all_heads([r + half for r in range(n)], [half] * n)

    @pl.when(jnp.logical_and(g == ng - 1, g > 0))
    def _():
        all_heads([min(r + half, n) for r in range(n)], [max(r, half) for r in range(n)])

    o_ref[...] = jnp.concatenate([oh_scr[h] for h in range(NAT_HEADS)], axis=1).astype(o_ref.dtype)


def _nat_bias_table(rel_bias):
    cols = jnp.arange(GRID_W)
    col_start = jnp.clip(cols - NAT_WIN_COLS // 2, 0, GRID_W - NAT_WIN_COLS)
    kc = jnp.arange(GRID_W)
    in_win = (kc[None, :] >= col_start[:, None]) & (kc[None, :] < col_start[:, None] + NAT_WIN_COLS)
    col_off = kc[None, :] - cols[:, None] + (NAT_WIN_COLS - 1)
    v = jnp.arange(NAT_WIN_ROWS)
    i = jnp.arange(NAT_WIN_ROWS)
    row_off = i[None, :] - v[:, None] + (NAT_WIN_ROWS - 1)
    row_hot = (row_off[:, :, None] == jnp.arange(2 * NAT_WIN_ROWS - 1)[None, None, :]).astype(F32)
    col_hot = (col_off[:, :, None] == jnp.arange(2 * NAT_WIN_COLS - 1)[None, None, :]).astype(F32)
    rows_sel = jnp.sum(rel_bias.astype(F32)[:, None, None] * row_hot[None, :, :, :, None], axis=3)
    b = jnp.sum(rows_sel.transpose(1, 0, 2, 3)[:, :, None, :, None, :] * col_hot[None, None, :, None], axis=5)
    b = jnp.where(in_win[None, None, :, None, :], b, MASK_VALUE)
    return b.reshape(NAT_WIN_ROWS, NAT_HEADS, GRID_W, NAT_WIN_TOKENS)


def _neighbourhood_attention(nq, nkt, nv, bias, layer, bsz, seq_len):
    h, d, st = NAT_HEADS, HEAD_DIM, NAT_STEP_TOKENS
    ng = seq_len // st
    assert ng >= 2 and seq_len // GRID_W >= NAT_WIN_ROWS
    prev = lambda g: jnp.maximum(g - 1, 0)
    nxt = lambda g: jnp.minimum(g + 1, ng - 1)
    same = lambda g: g
    kt_spec = lambda f: pl.BlockSpec((None, h * d, st), lambda b, g: (b, 0, f(g)))
    v_spec = lambda f: pl.BlockSpec((h, st, d), lambda b, g: (0, b * ng + f(g), 0))
    return pl.pallas_call(
        _nat_kernel,
        grid=(bsz, ng),
        in_specs=[v_spec(same), kt_spec(prev), kt_spec(same), kt_spec(nxt),
                  v_spec(prev), v_spec(same), v_spec(nxt), _resident_layer(bias, layer)],
        out_specs=pl.BlockSpec((st, h * d), lambda b, g: (b * ng + g, 0)),
        out_shape=jax.ShapeDtypeStruct((bsz * seq_len, h * d), BF16),
        scratch_shapes=[pltpu.VMEM((h * d, 3 * st), BF16), pltpu.VMEM((h, 3 * st, d), BF16),
                        pltpu.VMEM((h, st, d), F32)],
        compiler_params=_params("parallel", "parallel"),
        name="nat",
    )(nq, nkt, nkt, nkt, nv, nv, nv, bias)


def _s5_matrices(a_re, a_im, log_dt, b_re, b_im, c_re, c_im, d_skip):
    t_len, hs = SSM_CHUNK, SSM_GROUP
    a_re = a_re.astype(F32)
    a_im = a_im.astype(F32)
    dt = jnp.exp(log_dt.astype(F32))[..., None]
    decay = jnp.exp(a_re * dt)
    phase = a_im * dt
    lam_re = decay * jnp.cos(phase)
    lam_im = decay * jnp.sin(phase)
    den = a_re * a_re + a_im * a_im
    num_re = lam_re - 1.0
    coef_re = (num_re * a_re + lam_im * a_im) / den
    coef_im = (lam_im * a_re - num_re * a_im) / den
    b_re = b_re.astype(F32)[None]
    b_im = b_im.astype(F32)[None]
    bbar_re = coef_re[..., None] * b_re - coef_im[..., None] * b_im
    bbar_im = coef_re[..., None] * b_im + coef_im[..., None] * b_re
    c_re = c_re.astype(F32)
    c_im = c_im.astype(F32)

    def powers(exponents):
        e = jnp.asarray(exponents, F32)[:, None, None, None]
        mag = jnp.exp(e * (a_re * dt)[None])
        return mag * jnp.cos(e * phase[None]), mag * jnp.sin(e * phase[None])

    def c_times(p_re, p_im):
        return (c_re[None] * p_re[:, :, :, None, :] - c_im[None] * p_im[:, :, :, None, :],
                c_re[None] * p_im[:, :, :, None, :] + c_im[None] * p_re[:, :, :, None, :])

    def times_bbar(p_re, p_im):
        return (p_re[..., None] * bbar_re[None] - p_im[..., None] * bbar_im[None],
                p_re[..., None] * bbar_im[None] + p_im[..., None] * bbar_re[None])

    tk = jnp.arange(t_len)
    cl_re, cl_im = c_times(*powers(tk))
    taps = jnp.sum(cl_re[..., None] * bbar_re[None, :, :, None] - cl_im[..., None] * bbar_im[None, :, :, None], axis=4)
    lag = tk[None, :] - tk[:, None]
    hot_f = (lag[:, :, None] == tk[None, None, :]).astype(F32)
    hot_r = (-lag[:, :, None] == tk[None, None, :]).astype(F32)
    eye_t = jnp.eye(t_len, dtype=F32)[:, :, None, None, None]
    skip = eye_t * (jnp.eye(hs, dtype=F32)[None] * d_skip.astype(F32)[:, :, None])[None, None]
    pick = lambda hot, tp: jnp.sum(hot[:, :, :, None, None, None] * tp[None, None], axis=2)
    toep = pick(hot_f, taps[:, 0]) + pick(hot_r, taps[:, 1]) + skip
    toep = toep.transpose(2, 0, 4, 1, 3).reshape(SSM_GROUPS, S5_TILE, S5_TILE)

    to_in = lambda m: m.transpose(1, 0, 3, 2).reshape(SSM_GROUPS, S5_TILE, SSM_STATE)
    f_re, f_im = times_bbar(*powers(t_len - 1 - tk))
    r_re, r_im = times_bbar(*powers(tk))
    b_mat = jnp.concatenate([to_in(f_re[:, 0]), to_in(r_re[:, 1]), to_in(f_im[:, 0]), to_in(r_im[:, 1])], axis=2)

    to_out = lambda m: m.transpose(1, 3, 0, 2).reshape(SSM_GROUPS, SSM_STATE, S5_TILE)
    of_re, of_im = c_times(*powers(tk + 1))
    or_re, or_im = c_times(*powers(t_len - tk))
    m_mat = jnp.concatenate([to_out(of_re[:, 0]), to_out(or_re[:, 1]), to_out(-of_im[:, 0]), to_out(-or_im[:, 1])],
                            axis=1)

    def chunk_powers(n_fwd, n_rev):
        (fr, fi), (rr, ri) = powers(n_fwd * t_len), powers(n_rev * t_len)
        return jnp.concatenate([fr[:, 0], rr[:, 1], fi[:, 0], ri[:, 1]], axis=-1)
    doubling = jnp.array([1, 2, 4])
    steps = chunk_powers(doubling, doubling)
    rows = jnp.arange(S5_SCAN_ROWS)
    carry = chunk_powers(rows + 1, S5_SCAN_ROWS - rows)
    return toep.astype(BF16), b_mat.astype(BF16), m_mat.astype(BF16), steps, carry


def _s5_state_in_kernel(u_ref, b_ref, z_ref, *, bsz):
    for b in range(bsz):
        cols = slice(S5_TILE * b, S5_TILE * (b + 1))
        z_ref[:, cols] = _dot(u_ref[:, cols], b_ref[...])


def _s5_scan_kernel(z_ref, step_ref, carry_ref, s_ref, *, n_tiles, pairs):
    rows = S5_SCAN_ROWS
    lane = lax.broadcasted_iota(jnp.int32, (rows, LANES), 1)
    sub = lax.broadcasted_iota(jnp.int32, (rows, LANES), 0)
    is_fwd = lane < SSM_STATE
    is_rev = jnp.logical_not(is_fwd)
    both = lambda fwd_rows, rev_rows: jnp.logical_or(jnp.logical_and(is_fwd, fwd_rows),
                                                     jnp.logical_and(is_rev, rev_rows))
    edge = both(sub == 0, sub == rows - 1)

    def upstream(x, dist):
        valid = both(sub >= dist, sub < rows - dist)
        return jnp.where(valid, jnp.where(is_fwd, pltpu.roll(x, dist, 0), pltpu.roll(x, rows - dist, 0)), 0.0)

    def step(k, carry):
        rf = pl.multiple_of(k * rows, rows)
        rr = pl.multiple_of((n_tiles - 1 - k) * rows, rows)
        new = []
        for j in range(pairs):
            cre = slice(S5_STATE_COLS * j, S5_STATE_COLS * j + LANES)
            cim = slice(S5_STATE_COLS * j + LANES, S5_STATE_COLS * (j + 1))
            c_re, c_im = carry[j]
            x_re = jnp.where(is_fwd, z_ref[pl.ds(rf, rows), cre], z_ref[pl.ds(rr, rows), cre])
            x_im = jnp.where(is_fwd, z_ref[pl.ds(rf, rows), cim], z_ref[pl.ds(rr, rows), cim])
            for i, dist in enumerate((1, 2, 4)):
                lr = step_ref[i:i + 1, cre]
                li = step_ref[i:i + 1, cim]
                u_re, u_im = upstream(x_re, dist), upstream(x_im, dist)
                x_re, x_im = x_re + lr * u_re - li * u_im, x_im + lr * u_im + li * u_re
            pr, pi = carry_ref[:, cre], carry_ref[:, cim]
            a_re = x_re + pr * c_re - pi * c_im
            a_im = x_im + pr * c_im + pi * c_re
            e_re = jnp.where(edge, c_re, upstream(a_re, 1))
            e_im = jnp.where(edge, c_im, upstream(a_im, 1))
            pltpu.store(s_ref.at[pl.ds(rf, rows), cre], e_re, mask=is_fwd)
            pltpu.store(s_ref.at[pl.ds(rf, rows), cim], e_im, mask=is_fwd)
            pltpu.store(s_ref.at[pl.ds(rr, rows), cre], e_re, mask=is_rev)
            pltpu.store(s_ref.at[pl.ds(rr, rows), cim], e_im, mask=is_rev)
            last = lambda a: jnp.where(is_fwd, jnp.broadcast_to(a[rows - 1:rows], a.shape),
                                       jnp.broadcast_to(a[0:1], a.shape))
            new.append((last(a_re), last(a_im)))
        return tuple(new)

    zero = jnp.zeros((rows, LANES), F32)
    lax.fori_loop(0, n_tiles, step, tuple((zero, zero) for _ in range(pairs)))


def _s5_out_kernel(u_ref, s_ref, t_ref, m_ref, y_ref, *, bsz):
    for b in range(bsz):
        cols = slice(S5_TILE * b, S5_TILE * (b + 1))
        y = _dot(u_ref[:, cols], t_ref[...]) + _dot(s_ref[:, cols].astype(BF16), m_ref[...])
        y_ref[:, cols] = y.astype(y_ref.dtype)


def _s5_bidirectional(u2, mats, layer, bsz, seq_len):
    toep, b_mat, m_mat, steps, carry = mats
    g = SSM_GROUPS
    n_chunks = seq_len // SSM_CHUNK
    assert n_chunks % S5_SCAN_ROWS == 0
    width = bsz * S5_TILE
    u_spec = pl.BlockSpec((None, n_chunks, width), lambda j: (j, 0, 0))
    w_spec = pl.BlockSpec((None, None, S5_TILE, S5_TILE), lambda j: (layer, j, 0, 0))
    col_spec = pl.BlockSpec((n_chunks, width), lambda j: (0, j))
    z = pl.pallas_call(
        functools.partial(_s5_state_in_kernel, bsz=bsz),
        grid=(g,),
        in_specs=[u_spec, w_spec],
        out_specs=col_spec,
        out_shape=jax.ShapeDtypeStruct((n_chunks, g * width), F32),
        compiler_params=_params("parallel"),
        name="s5_state_in",
    )(u2, b_mat)
    s_prev = pl.pallas_call(
        functools.partial(_s5_scan_kernel, n_tiles=n_chunks // S5_SCAN_ROWS, pairs=bsz),
        grid=(g,),
        in_specs=[col_spec, pl.BlockSpec((None, steps.shape[1], width), lambda j: (layer, 0, j)),
                  pl.BlockSpec((None, S5_SCAN_ROWS, width), lambda j: (layer, 0, j))],
        out_specs=col_spec,
        out_shape=jax.ShapeDtypeStruct((n_chunks, g * width), F32),
        compiler_params=_params("parallel"),
        name="s5_scan",
    )(z, steps, carry)
    return pl.pallas_call(
        functools.partial(_s5_out_kernel, bsz=bsz),
        grid=(g,),
        in_specs=[u_spec, col_spec, w_spec, w_spec],
        out_specs=u_spec,
        out_shape=jax.ShapeDtypeStruct((g, n_chunks, width), BF16),
        compiler_params=_params("parallel"),
        name="s5_out",
    )(u2, s_prev, toep, m_mat)


def _merge_kernel(x_ref, attn_ref, nat_ref, y_ref, gate_ref, wglu_ref, wb_ref, wout_ref, gain_ref, bias_ref, o_ref,
                  y_scr):
    _from_chunk_layout(y_ref, y_scr, x_ref.shape[0] // SSM_CHUNK)
    z = jax.nn.gelu(jnp.concatenate([y_scr[q] for q in range(y_scr.shape[0])], axis=1))
    ssm = z * jax.nn.sigmoid(_dot(z.astype(BF16), wglu_ref[...]))
    d = D_MODEL
    merged = gate_ref[:, 0:d].astype(F32) * _dot(attn_ref[...], wb_ref[0])
    merged += gate_ref[:, d:2 * d].astype(F32) * _dot(nat_ref[...], wb_ref[1])
    merged += gate_ref[:, 2 * d:3 * d].astype(F32) * _dot(ssm.astype(BF16), wb_ref[2])
    mix = _dot(merged.astype(BF16), wout_ref[...])
    o_ref[...] = _layer_norm(DEEPNORM_ALPHA * x_ref[...] + mix, gain_ref[...], bias_ref[...])


def _merge(x2d, attn_o, nat_o, y_ssm, gates, w_glu, w_branch, w_out, layer, gain, bias, seq_len, tm):
    m = x2d.shape[0]
    nl = seq_len // tm
    row = lambda width: pl.BlockSpec((tm, width), lambda i: (i, 0))
    gain, bias = gain.astype(F32)[None, :], bias.astype(F32)[None, :]
    return pl.pallas_call(
        _merge_kernel,
        grid=(m // tm,),
        in_specs=[row(D_MODEL), row(MIX_WIDTH), row(MIX_WIDTH),
                  pl.BlockSpec((SSM_GROUPS, tm // SSM_CHUNK, S5_TILE), lambda i: (0, i % nl, i // nl)),
                  row(GATE_WIDTH), _resident_layer(w_glu, layer), _resident_layer(w_branch, layer),
                  _resident_layer(w_out, layer), _resident(gain), _resident(bias)],
        out_specs=row(D_MODEL),
        out_shape=jax.ShapeDtypeStruct((m, D_MODEL), F32),
        scratch_shapes=[pltpu.VMEM((MIX_WIDTH // LANES, tm, LANES), F32)],
        compiler_params=_params("parallel"),
        name="merge",
    )(x2d, attn_o, nat_o, y_ssm, gates, w_glu, w_branch, w_out, gain, bias)


def _ffn_kernel(x_ref, wup_ref, wdown_ref, gain_ref, bias_ref, o_ref, xb_scr, acc_scr):
    f = pl.program_id(1)

    @pl.when(f == 0)
    def _():
        xb_scr[...] = x_ref[...].astype(BF16)
        acc_scr[...] = jnp.zeros_like(acc_scr)

    h = jnp.maximum(_dot(xb_scr[...], wup_ref[...]), 0.0)
    acc_scr[...] += _dot((h * h).astype(BF16), wdown_ref[...])

    @pl.when(f == pl.num_programs(1) - 1)
    def _():
        o_ref[...] = _layer_norm(DEEPNORM_ALPHA * x_ref[...] + acc_scr[...], gain_ref[...], bias_ref[...])


def _ffn(x2d, w_up, w_down, layer, gain, bias, tm, tf):
    m = x2d.shape[0]
    gain, bias = gain.astype(F32)[None, :], bias.astype(F32)[None, :]
    vec = pl.BlockSpec((1, D_MODEL), lambda i, f: (0, 0))
    return pl.pallas_call(
        _ffn_kernel,
        grid=(m // tm, FFN_DIM // tf),
        in_specs=[pl.BlockSpec((tm, D_MODEL), lambda i, f: (i, 0)),
                  pl.BlockSpec((None, D_MODEL, tf), lambda i, f: (layer, 0, f)),
                  pl.BlockSpec((None, tf, D_MODEL), lambda i, f: (layer, f, 0)),
                  vec, vec],
        out_specs=pl.BlockSpec((tm, D_MODEL), lambda i, f: (i, 0)),
        out_shape=jax.ShapeDtypeStruct((m, D_MODEL), F32),
        scratch_shapes=[pltpu.VMEM((tm, D_MODEL), BF16), pltpu.VMEM((tm, D_MODEL), F32)],
        compiler_params=_params("parallel", "arbitrary"),
        name="ffn",
    )(x2d, w_up, w_down, gain, bias)


def _tile_sizes(seq_len):
    proj_tm = min(512, seq_len)
    ffn_tm = min(1024, seq_len)
    ffn_tf = 1024
    return proj_tm, ffn_tm, ffn_tf


def kernel(x, w_in, q_norm_gain, k_norm_gain, nat_rel_bias, ssm_a_re, ssm_a_im, ssm_log_dt, ssm_b_re, ssm_b_im, ssm_c_re, ssm_c_im, ssm_d, ssm_w_glu, w_branch, w_out, ln1_gain, ln1_bias, w_ffn_up, w_ffn_down, ln2_gain, ln2_bias):
    bsz, seq_len, _ = x.shape
    proj_tm, ffn_tm, ffn_tf = _tile_sizes(seq_len)
    rope = _rope_tables(seq_len)
    w_bf = w_in.astype(BF16)
    wt = _transposed_proj_weights(w_bf)
    toep, b_mat, m_mat, steps, carry = jax.vmap(_s5_matrices)(
        ssm_a_re, ssm_a_im, ssm_log_dt, ssm_b_re, ssm_b_im, ssm_c_re, ssm_c_im, ssm_d)
    per_batch = lambda t: jnp.tile(t[:, :, :, None, :], (1, 1, 1, bsz, 1)).reshape(t.shape[0], t.shape[1], -1)
    mats = (toep, b_mat, m_mat, per_batch(steps), per_batch(carry))
    nat_bias = jax.vmap(_nat_bias_table)(nat_rel_bias)
    w_glu, w_br, w_o = ssm_w_glu.astype(BF16), w_branch.astype(BF16), w_out.astype(BF16)
    w_up, w_down = w_ffn_up.astype(BF16), w_ffn_down.astype(BF16)
    h = x.reshape(bsz * seq_len, D_MODEL)
    for layer in range(w_in.shape[0]):
        (qt, vt, nkt, k, nq, nv, u2, gates), bound = _input_projections(
            h, w_bf, wt, layer, q_norm_gain[layer], k_norm_gain[layer], rope, bsz, seq_len, proj_tm)
        attn_o = _gqa_attention(qt, k, vt, bound, bsz, seq_len)
        nat_o = _neighbourhood_attention(nq, nkt, nv, nat_bias, layer, bsz, seq_len)
        y2 = _s5_bidirectional(u2, mats, layer, bsz, seq_len)
        h = _merge(h, attn_o, nat_o, y2, gates, w_glu, w_br, w_o, layer, ln1_gain[layer], ln1_bias[layer],
                   seq_len, proj_tm)
        h = _ffn(h, w_up, w_down, layer, ln2_gain[layer], ln2_bias[layer], ffn_tm, ffn_tf)
    return h.reshape(bsz, seq_len, D_MODEL)
```

```python
import functools

import jax
import jax.numpy as jnp
from jax import lax
from jax.experimental import pallas as pl
from jax.experimental.pallas import tpu as pltpu

D_MODEL = 1024
DEPTH = 2
GRID_W = 64
HEAD_DIM = 64
MIX_WIDTH = 512
ATTN_Q_HEADS = 8
ATTN_KV_HEADS = 2
ATTN_GROUP = ATTN_Q_HEADS // ATTN_KV_HEADS
NAT_HEADS = 8
NAT_WIN_ROWS = 8
NAT_WIN_COLS = 16
SSM_GROUP = 16
SSM_GROUPS = 32
SSM_STATE = 64
SSM_CHUNK = 16
N_BRANCHES = 3
FFN_DIM = 4 * D_MODEL
ROPE_THETA = 10000.0
LN_EPS = 1e-5
RMS_EPS = 1e-6
DEEPNORM_ALPHA = (2 * DEPTH) ** 0.25
ATTN_SCALE = HEAD_DIM ** -0.5
LOG2_E = 1.4426950408889634
MASK_VALUE = -1e30

Q_WIDTH = ATTN_Q_HEADS * HEAD_DIM
KV_WIDTH = ATTN_KV_HEADS * HEAD_DIM
GATE_WIDTH = N_BRANCHES * D_MODEL

LANES = 128
SUBLANES = 8
MXU_WIDTH = 256
V7X_VMEM_BYTES = 64 * 1024 * 1024
VMEM_LIMIT = V7X_VMEM_BYTES - 8 * 1024 * 1024

F32 = jnp.float32
BF16 = jnp.bfloat16
NT_DIMS = (((1,), (1,)), ((), ()))


def _params(*semantics):
    return pltpu.CompilerParams(dimension_semantics=semantics, vmem_limit_bytes=VMEM_LIMIT)


def _dot(a, b):
    return jnp.dot(a, b, preferred_element_type=F32)


def _layer_norm(x, gain, bias):
    mu = jnp.mean(x, axis=-1, keepdims=True)
    xc = x - mu
    var = jnp.mean(xc * xc, axis=-1, keepdims=True)
    return xc * lax.rsqrt(var + LN_EPS) * gain + bias


def _resident(a):
    return pl.BlockSpec(a.shape, lambda *_: (0,) * a.ndim, pipeline_mode=pl.Buffered(1))


def _resident_layer(a, layer):
    return pl.BlockSpec((None,) + a.shape[1:], lambda *_: (layer,) + (0,) * (a.ndim - 1),
                        pipeline_mode=pl.Buffered(1))


W_IN_OFFSETS = (0, Q_WIDTH, Q_WIDTH + KV_WIDTH, Q_WIDTH + 2 * KV_WIDTH, Q_WIDTH + 2 * KV_WIDTH + MIX_WIDTH,
                Q_WIDTH + 2 * KV_WIDTH + 2 * MIX_WIDTH, Q_WIDTH + 2 * KV_WIDTH + 3 * MIX_WIDTH,
                Q_WIDTH + 2 * KV_WIDTH + 4 * MIX_WIDTH)


ATTN_KV_PAGE = 512
ATTN_TQ = MXU_WIDTH
ATTN_Q_BLOCKS_PER_STEP = 2
ATTN_EXT_DIM = LANES
ATTN_V_ROWS = HEAD_DIM + 16
ATTN_BOUND_LIMIT = 60.0
PROJ_T_ROWS = Q_WIDTH + KV_WIDTH + MIX_WIDTH
S5_TILE = SSM_CHUNK * SSM_GROUP
S5_STATE_COLS = 4 * SSM_STATE
S5_SCAN_ROWS = SUBLANES


GROUPS_PER_TILE = LANES // SSM_GROUP
CHUNKS_PER_TILE = LANES // SSM_GROUP


def _to_chunk_layout(x_scr, u_ref, n_chunks):
    lane_grp = lax.broadcasted_iota(jnp.int32, (n_chunks, LANES), 1) // SSM_GROUP
    for half in range(SSM_CHUNK // CHUNKS_PER_TILE):
        for q in range(x_scr.shape[0]):
            steps = [x_scr[q, pl.ds(CHUNKS_PER_TILE * half + tp, n_chunks, stride=SSM_CHUNK), :]
                     for tp in range(CHUNKS_PER_TILE)]
            for gm in range(GROUPS_PER_TILE):
                tile = None
                for tp in range(CHUNKS_PER_TILE):
                    shift = (SSM_GROUP * (tp - gm)) % LANES
                    moved = pltpu.roll(steps[tp], shift, 1) if shift else steps[tp]
                    tile = moved if tile is None else jnp.where(lane_grp == tp, moved, tile)
                u_ref[GROUPS_PER_TILE * q + gm, :, LANES * half:LANES * (half + 1)] = tile.astype(u_ref.dtype)


def _from_chunk_layout(y_ref, y_scr, n_chunks):
    lane_grp = lax.broadcasted_iota(jnp.int32, (n_chunks, LANES), 1) // SSM_GROUP
    for half in range(SSM_CHUNK // CHUNKS_PER_TILE):
        for q in range(y_scr.shape[0]):
            groups = [y_ref[GROUPS_PER_TILE * q + gm, :, LANES * half:LANES * (half + 1)].astype(F32)
                      for gm in range(GROUPS_PER_TILE)]
            for tp in range(CHUNKS_PER_TILE):
                tile = None
                for gm in range(GROUPS_PER_TILE):
                    shift = (SSM_GROUP * (gm - tp)) % LANES
                    moved = pltpu.roll(groups[gm], shift, 1) if shift else groups[gm]
                    tile = moved if tile is None else jnp.where(lane_grp == gm, moved, tile)
                y_scr[q, pl.ds(CHUNKS_PER_TILE * half + tp, n_chunks, stride=SSM_CHUNK), :] = tile


def _proj_kernel(x_ref, wt_ref, w_ref, qgain_ref, kgain_ref, cost_ref, sint_ref, cos_ref, sin_ref, seg_ref,
                 qext_ref, vext_ref, qt_ref, vt_ref, nkt_ref, k_ref, nq_ref, nv_ref, u_ref, g_ref, su_scr):
    tm = x_ref.shape[0]
    d = HEAD_DIM
    xb = x_ref[...].astype(BF16)

    yt = lax.dot_general(wt_ref[...], xb, NT_DIMS, preferred_element_type=F32)
    cost = cost_ref[...]
    sint = sint_ref[...]
    qgain = qgain_ref[...]
    for h in range(ATTN_Q_HEADS):
        blk = yt[d * h:d * (h + 1), :]
        ms = jnp.mean(blk * blk, axis=0, keepdims=True)
        yn = blk * lax.rsqrt(ms + RMS_EPS) * qgain
        partner = jnp.concatenate([yn[16:32], yn[0:16], yn[48:64], yn[32:48]], axis=0)
        qh = ((yn * cost + partner * sint) * (ATTN_SCALE * LOG2_E)).astype(BF16)
        j, g = divmod(h, ATTN_GROUP)
        for qb in range(tm // ATTN_TQ):
            qt_ref[j, qb, 0:d, ATTN_TQ * g:ATTN_TQ * (g + 1)] = qh[:, ATTN_TQ * qb:ATTN_TQ * (qb + 1)]
    for j in range(ATTN_KV_HEADS):
        for qb in range(tm // ATTN_TQ):
            qt_ref[j, qb, d:ATTN_EXT_DIM, :] = qext_ref[...]
    vt = yt[Q_WIDTH:Q_WIDTH + KV_WIDTH, :].astype(BF16)
    for pg in range(tm // ATTN_KV_PAGE):
        cols = slice(ATTN_KV_PAGE * pg, ATTN_KV_PAGE * (pg + 1))
        for j in range(ATTN_KV_HEADS):
            vt_ref[pg, ATTN_V_ROWS * j:ATTN_V_ROWS * j + d, :] = vt[d * j:d * (j + 1), cols]
            vt_ref[pg, ATTN_V_ROWS * j + d:ATTN_V_ROWS * (j + 1), :] = vext_ref[...]
    nkt_ref[...] = yt[Q_WIDTH + KV_WIDTH:, :].astype(BF16)

    o_ak, o_nq, o_nv, o_su, o_gate = (W_IN_OFFSETS[i] for i in (1, 3, 5, 6, 7))
    yk = _dot(xb, w_ref[:, o_ak:o_ak + KV_WIDTH])
    y2 = yk * yk
    hi = y2.astype(BF16)
    lo = (y2 - hi.astype(F32)).astype(BF16)
    ms = (_dot(hi, seg_ref[...]) + _dot(lo, seg_ref[...])) * (1.0 / d)
    kn = yk * lax.rsqrt(ms + RMS_EPS) * kgain_ref[...]
    lane = lax.broadcasted_iota(jnp.int32, (tm, LANES), 1)
    partner = jnp.where((lane % 32) < 16, pltpu.roll(kn, LANES - 16, 1), pltpu.roll(kn, 16, 1))
    kk = kn * cos_ref[...] + partner * sin_ref[...]
    one_hot = (lane == d).astype(F32)
    for j in range(ATTN_KV_HEADS):
        kj = kk if j == 0 else pltpu.roll(kk, LANES - d * j, 1)
        k_ref[j] = jnp.where(lane < d, kj, one_hot).astype(BF16)

    ynq = _dot(xb, w_ref[:, o_nq:o_nq + MIX_WIDTH]) * ATTN_SCALE
    ynv = _dot(xb, w_ref[:, o_nv:o_nv + MIX_WIDTH])
    for h in range(NAT_HEADS):
        nq_ref[h] = ynq[:, d * h:d * (h + 1)].astype(BF16)
        nv_ref[h] = ynv[:, d * h:d * (h + 1)].astype(BF16)
    su = _dot(xb, w_ref[:, o_su:o_su + MIX_WIDTH])
    for q in range(MIX_WIDTH // LANES):
        su_scr[q] = su[:, LANES * q:LANES * (q + 1)]
    _to_chunk_layout(su_scr, u_ref, tm // SSM_CHUNK)
    for n in range(N_BRANCHES):
        y = _dot(xb, w_ref[:, o_gate + D_MODEL * n:o_gate + D_MODEL * (n + 1)])
        g_ref[:, D_MODEL * n:D_MODEL * (n + 1)] = jax.nn.sigmoid(y).astype(BF16)


def _rope_tables(seq_len):
    t = jnp.arange(seq_len)
    row = (t // GRID_W).astype(F32)
    col = (t % GRID_W).astype(F32)
    axis_dim = HEAD_DIM // 2
    inv_freq = 1.0 / (ROPE_THETA ** (jnp.arange(0, axis_dim, 2, dtype=F32) / axis_dim))
    ang_r = row[:, None] * inv_freq[None, :]
    ang_c = col[:, None] * inv_freq[None, :]
    cos_head = jnp.concatenate([jnp.cos(ang_r), jnp.cos(ang_r), jnp.cos(ang_c), jnp.cos(ang_c)], axis=1)
    sin_head = jnp.concatenate([-jnp.sin(ang_r), jnp.sin(ang_r), -jnp.sin(ang_c), jnp.sin(ang_c)], axis=1)
    reps = LANES // HEAD_DIM
    return (jnp.tile(cos_head, (1, reps)), jnp.tile(sin_head, (1, reps)), cos_head.T, sin_head.T)


def _transposed_proj_weights(w_bf):
    o = W_IN_OFFSETS
    parts = [w_bf[:, :, o[0]:o[1]], w_bf[:, :, o[2]:o[3]], w_bf[:, :, o[4]:o[5]]]
    return jnp.concatenate(parts, axis=2).transpose(0, 2, 1)


def _input_projections(x2d, w_bf, wt, layer, q_gain, k_gain, rope, bsz, seq_len, tm):
    m = x2d.shape[0]
    nl = seq_len // tm
    pages = tm // ATTN_KV_PAGE
    qgain = jnp.broadcast_to(q_gain.astype(F32)[:, None], (HEAD_DIM, tm))
    kgain = jnp.tile(k_gain.astype(F32), ATTN_KV_HEADS)[None, :]
    seg = (jnp.arange(KV_WIDTH)[:, None] // HEAD_DIM == jnp.arange(KV_WIDTH)[None, :] // HEAD_DIM).astype(BF16)
    bound = (HEAD_DIM * ATTN_SCALE * LOG2_E * 1.02) * jnp.max(jnp.abs(q_gain.astype(F32))) * jnp.max(jnp.abs(k_gain.astype(F32)))
    first_row = lambda rows, width: (jnp.arange(rows)[:, None] == 0) & (jnp.arange(width)[None, :] >= 0)
    qext = jnp.where(first_row(ATTN_EXT_DIM - HEAD_DIM, ATTN_GROUP * ATTN_TQ), -bound, 0.0).astype(BF16)
    vext = first_row(ATTN_V_ROWS - HEAD_DIM, ATTN_KV_PAGE).astype(BF16)
    cos, sin, cos_t, sin_t = rope
    tok = lambda width: pl.BlockSpec((tm, width), lambda i: (i, 0))
    heads = lambda n, width=HEAD_DIM: pl.BlockSpec((n, tm, width), lambda i: (0, i, 0))
    feat_t = lambda rows: pl.BlockSpec((None, rows, tm), lambda i: (i // nl, 0, i % nl))
    outs = pl.pallas_call(
        _proj_kernel,
        grid=(m // tm,),
        in_specs=[tok(D_MODEL), _resident_layer(wt, layer), _resident_layer(w_bf, layer), _resident(qgain), _resident(kgain),
                  pl.BlockSpec((HEAD_DIM, tm), lambda i: (0, i % nl)),
                  pl.BlockSpec((HEAD_DIM, tm), lambda i: (0, i % nl)),
                  pl.BlockSpec((tm, LANES), lambda i: (i % nl, 0)),
                  pl.BlockSpec((tm, LANES), lambda i: (i % nl, 0)),
                  _resident(seg), _resident(qext), _resident(vext)],
        out_specs=[pl.BlockSpec((None, ATTN_KV_HEADS, tm // ATTN_TQ, ATTN_EXT_DIM, ATTN_GROUP * ATTN_TQ),
                                lambda i: (i // nl, 0, i % nl, 0, 0)),
                   pl.BlockSpec((None, pages, ATTN_KV_HEADS * ATTN_V_ROWS, ATTN_KV_PAGE),
                                lambda i: (i // nl, i % nl, 0, 0)),
                   feat_t(MIX_WIDTH),
                   heads(ATTN_KV_HEADS, ATTN_EXT_DIM), heads(NAT_HEADS), heads(NAT_HEADS),
                   pl.BlockSpec((SSM_GROUPS, tm // SSM_CHUNK, S5_TILE), lambda i: (0, i % nl, i // nl)),
                   tok(GATE_WIDTH)],
        out_shape=[jax.ShapeDtypeStruct((bsz, ATTN_KV_HEADS, seq_len // ATTN_TQ, ATTN_EXT_DIM, ATTN_GROUP * ATTN_TQ), BF16),
                   jax.ShapeDtypeStruct((bsz, seq_len // ATTN_KV_PAGE, ATTN_KV_HEADS * ATTN_V_ROWS, ATTN_KV_PAGE), BF16),
                   jax.ShapeDtypeStruct((bsz, MIX_WIDTH, seq_len), BF16),
                   jax.ShapeDtypeStruct((ATTN_KV_HEADS, m, ATTN_EXT_DIM), BF16),
                   jax.ShapeDtypeStruct((NAT_HEADS, m, HEAD_DIM), BF16),
                   jax.ShapeDtypeStruct((NAT_HEADS, m, HEAD_DIM), BF16),
                   jax.ShapeDtypeStruct((SSM_GROUPS, seq_len // SSM_CHUNK, bsz * S5_TILE), BF16),
                   jax.ShapeDtypeStruct((m, GATE_WIDTH), BF16)],
        scratch_shapes=[pltpu.VMEM((MIX_WIDTH // LANES, tm, LANES), F32)],
        compiler_params=_params("parallel"),
        name="in_proj",
    )(x2d, wt, w_bf, qgain, kgain, cos_t, sin_t, cos, sin, seg, qext, vext)
    return outs, bound


def _finish_attention(o_ref, qb, o):
    o_ref[ATTN_TQ * qb:ATTN_TQ * (qb + 1), :] = jnp.concatenate(
        [o[:, ATTN_TQ * g:ATTN_TQ * (g + 1)].T for g in range(ATTN_GROUP)], axis=1).astype(o_ref.dtype)


def _flash_bounded_kernel(qg_ref, k_ref, vt_ref, o_ref, *, n_pages):
    d = HEAD_DIM
    for qb in range(qg_ref.shape[0]):
        qg = qg_ref[qb]
        acc = jnp.zeros((ATTN_V_ROWS, qg.shape[1]), F32)
        for c in range(n_pages):
            s = _dot(k_ref[ATTN_KV_PAGE * c:ATTN_KV_PAGE * (c + 1), :], qg)
            acc = acc + _dot(vt_ref[c], jnp.exp2(s).astype(BF16))
        _finish_attention(o_ref, qb, acc[:d] / acc[d:d + 1])


def _flash_online_kernel(qg_ref, k_ref, vt_ref, o_ref, acc_scr, *, n_pages):
    d = HEAD_DIM
    for qb in range(qg_ref.shape[0]):
        qg = qg_ref[qb]
        nq = qg.shape[1]
        acc_scr[...] = jnp.zeros_like(acc_scr)

        def scores(c):
            return _dot(k_ref[ATTN_KV_PAGE * c:ATTN_KV_PAGE * (c + 1), :], qg)

        m = jnp.full((1, nq), -jnp.inf, F32)
        l = jnp.zeros((1, nq), F32)
        s_next = scores(0)
        for c in range(n_pages):
            s = s_next
            if c + 1 < n_pages:
                s_next = scores(c + 1)
            m_new = jnp.maximum(m, jnp.max(s, axis=0, keepdims=True))
            alpha = jnp.exp2(m - m_new)
            p = jnp.exp2(s - m_new)
            l = alpha * l + jnp.sum(p, axis=0, keepdims=True)
            acc_scr[...] = alpha * acc_scr[...] + _dot(vt_ref[c, 0:d, :], p.astype(BF16))
            m = m_new
        _finish_attention(o_ref, qb, acc_scr[...] / l)


def _gqa_attention(qg, k, vt, bound, bsz, seq_len):
    d, hk, grp, tq = HEAD_DIM, ATTN_KV_HEADS, ATTN_GROUP, ATTN_TQ
    nqb = ATTN_Q_BLOCKS_PER_STEP
    nq, n_pages = seq_len // (tq * nqb), seq_len // ATTN_KV_PAGE

    def call(body, scratch):
        return pl.pallas_call(
            functools.partial(body, n_pages=n_pages),
            grid=(bsz, hk, nq),
            in_specs=[pl.BlockSpec((None, None, nqb, ATTN_EXT_DIM, grp * tq), lambda b, j, i: (b, j, i, 0, 0)),
                      pl.BlockSpec((None, seq_len, ATTN_EXT_DIM), lambda b, j, i: (j, b, 0)),
                      pl.BlockSpec((None, n_pages, ATTN_V_ROWS, ATTN_KV_PAGE), lambda b, j, i: (b, 0, j, 0))],
            out_specs=pl.BlockSpec((tq * nqb, grp * d), lambda b, j, i: (b * nq + i, j)),
            out_shape=jax.ShapeDtypeStruct((bsz * seq_len, ATTN_Q_HEADS * d), BF16),
            scratch_shapes=scratch,
            compiler_params=_params("parallel", "parallel", "parallel"),
            name=body.__name__.strip("_").replace("_kernel", ""),
        )(qg, k, vt)

    return lax.cond(bound <= ATTN_BOUND_LIMIT,
                    lambda: call(_flash_bounded_kernel, []),
                    lambda: call(_flash_online_kernel, [pltpu.VMEM((d, grp * tq), F32)]))


NAT_ROWS_PER_STEP = 8
NAT_HEAD_UNROLL = 4
NAT_STEP_TOKENS = NAT_ROWS_PER_STEP * GRID_W
NAT_WIN_TOKENS = NAT_WIN_ROWS * GRID_W


def _nat_kernel(q_ref, ktp_ref, ktc_ref, ktn_ref, vp_ref, vc_ref, vn_ref, bias_ref, o_ref, kt_scr, v_scr, oh_scr):
    g = pl.program_id(1)
    ng = pl.num_programs(1)
    st = NAT_STEP_TOKENS
    d = HEAD_DIM
    kt_scr[:, 0:st] = ktp_ref[...]
    kt_scr[:, st:2 * st] = ktc_ref[...]
    kt_scr[:, 2 * st:3 * st] = ktn_ref[...]
    v_scr[:, 0:st, :] = vp_ref[...]
    v_scr[:, st:2 * st, :] = vc_ref[...]
    v_scr[:, 2 * st:3 * st, :] = vn_ref[...]

    def all_heads(frame_rows, bias_index):
        starts = [fr * GRID_W for fr in frame_rows]

        def head(h, carry):
            q = q_ref[h]
            r0 = pl.multiple_of(h * d, d)
            s = jnp.concatenate(
                [_dot(q[GRID_W * r:GRID_W * (r + 1), :],
                      kt_scr[pl.ds(r0, d), starts[r]:starts[r] + NAT_WIN_TOKENS]) + bias_ref[bias_index[r], h]
                 for r in range(NAT_ROWS_PER_STEP)], axis=0)
            m = jnp.max(s, axis=1, keepdims=True)
            p = jnp.exp(s - m)
            l = jnp.sum(p, axis=1, keepdims=True)
            pb = p.astype(BF16)
            o = jnp.concatenate(
                [_dot(pb[GRID_W * r:GRID_W * (r + 1), :], v_scr[h, starts[r]:starts[r] + NAT_WIN_TOKENS, :])
                 for r in range(NAT_ROWS_PER_STEP)], axis=0)
            oh_scr[h] = o / l
            return carry

        lax.fori_loop(0, NAT_HEADS, head, 0, unroll=NAT_HEAD_UNROLL)

    half = NAT_WIN_ROWS // 2
    n = NAT_ROWS_PER_STEP

    @pl.when(g == 0)
    def _():
        all_heads([n + max(r - half, 0) for r in range(n)], [min(r, half) for r in range(n)])

    @pl.when(jnp.logical_and(g > 0, g < ng - 1))
    def _():
        all_heads([r + half for r in range(n)], [half] * n)

    @pl.when(jnp.logical_and(g == ng - 1, g > 0))
    def _():
        all_heads([min(r + half, n) for r in range(n)], [max(r, half) for r in range(n)])

    o_ref[...] = jnp.concatenate([oh_scr[h] for h in range(NAT_HEADS)], axis=1).astype(o_ref.dtype)


def _nat_bias_table(rel_bias):
    w, nb = GRID_W, 2 * NAT_WIN_COLS - 1
    cols = jnp.arange(w)
    col_start = jnp.clip(cols - NAT_WIN_COLS // 2, 0, w - NAT_WIN_COLS)
    in_win = (cols[None, :] >= col_start[:, None]) & (cols[None, :] < col_start[:, None] + NAT_WIN_COLS)
    v = jnp.arange(NAT_WIN_ROWS)
    row_off = v[None, :] - v[:, None] + (NAT_WIN_ROWS - 1)
    row_hot = (row_off[:, :, None] == jnp.arange(2 * NAT_WIN_ROWS - 1)[None, None, :]).astype(F32)
    by_row = jnp.sum(rel_bias.astype(F32)[:, None, None] * row_hot[None, :, :, :, None], axis=3)
    period = nb + w + 1
    padded = jnp.pad(by_row, ((0, 0), (0, 0), (0, 0), (0, period - nb)))
    skewed = jnp.tile(padded, (1, 1, 1, w))[..., :w * (period - 1)].reshape(by_row.shape[:3] + (w, period - 1))
    b = skewed[..., NAT_WIN_COLS - 1:NAT_WIN_COLS - 1 + w]
    b = jnp.where(in_win[None, None, None], b, MASK_VALUE).transpose(1, 0, 3, 2, 4)
    return b.reshape(NAT_WIN_ROWS, NAT_HEADS, w, NAT_WIN_TOKENS)


def _neighbourhood_attention(nq, nkt, nv, bias, layer, bsz, seq_len):
    h, d, st = NAT_HEADS, HEAD_DIM, NAT_STEP_TOKENS
    ng = seq_len // st
    assert ng >= 2 and seq_len // GRID_W >= NAT_WIN_ROWS
    prev = lambda g: jnp.maximum(g - 1, 0)
    nxt = lambda g: jnp.minimum(g + 1, ng - 1)
    same = lambda g: g
    kt_spec = lambda f: pl.BlockSpec((None, h * d, st), lambda b, g: (b, 0, f(g)))
    v_spec = lambda f: pl.BlockSpec((h, st, d), lambda b, g: (0, b * ng + f(g), 0))
    return pl.pallas_call(
        _nat_kernel,
        grid=(bsz, ng),
        in_specs=[v_spec(same), kt_spec(prev), kt_spec(same), kt_spec(nxt),
                  v_spec(prev), v_spec(same), v_spec(nxt), _resident_layer(bias, layer)],
        out_specs=pl.BlockSpec((st, h * d), lambda b, g: (b * ng + g, 0)),
        out_shape=jax.ShapeDtypeStruct((bsz * seq_len, h * d), BF16),
        scratch_shapes=[pltpu.VMEM((h * d, 3 * st), BF16), pltpu.VMEM((h, 3 * st, d), BF16),
                        pltpu.VMEM((h, st, d), F32)],
        compiler_params=_params("parallel", "parallel"),
        name="nat",
    )(nq, nkt, nkt, nkt, nv, nv, nv, bias)


def _s5_matrices(a_re, a_im, log_dt, b_re, b_im, c_re, c_im, d_skip):
    t_len, hs = SSM_CHUNK, SSM_GROUP
    a_re = a_re.astype(F32)
    a_im = a_im.astype(F32)
    dt = jnp.exp(log_dt.astype(F32))[..., None]
    decay = jnp.exp(a_re * dt)
    phase = a_im * dt
    lam_re = decay * jnp.cos(phase)
    lam_im = decay * jnp.sin(phase)
    den = a_re * a_re + a_im * a_im
    num_re = lam_re - 1.0
    coef_re = (num_re * a_re + lam_im * a_im) / den
    coef_im = (lam_im * a_re - num_re * a_im) / den
    b_re = b_re.astype(F32)[None]
    b_im = b_im.astype(F32)[None]
    bbar_re = coef_re[..., None] * b_re - coef_im[..., None] * b_im
    bbar_im = coef_re[..., None] * b_im + coef_im[..., None] * b_re
    c_re = c_re.astype(F32)
    c_im = c_im.astype(F32)

    def powers(exponents):
        e = jnp.asarray(exponents, F32)[:, None, None, None]
        mag = jnp.exp(e * (a_re * dt)[None])
        return mag * jnp.cos(e * phase[None]), mag * jnp.sin(e * phase[None])

    def c_times(p_re, p_im):
        return (c_re[None] * p_re[:, :, :, None, :] - c_im[None] * p_im[:, :, :, None, :],
                c_re[None] * p_im[:, :, :, None, :] + c_im[None] * p_re[:, :, :, None, :])

    def times_bbar(p_re, p_im):
        return (p_re[..., None] * bbar_re[None] - p_im[..., None] * bbar_im[None],
                p_re[..., None] * bbar_im[None] + p_im[..., None] * bbar_re[None])

    tk = jnp.arange(t_len)
    cl_re, cl_im = c_times(*powers(tk))
    taps = jnp.sum(cl_re[..., None] * bbar_re[None, :, :, None] - cl_im[..., None] * bbar_im[None, :, :, None], axis=4)
    lag = tk[None, :] - tk[:, None]
    hot_f = (lag[:, :, None] == tk[None, None, :]).astype(F32)
    hot_r = (-lag[:, :, None] == tk[None, None, :]).astype(F32)
    eye_t = jnp.eye(t_len, dtype=F32)[:, :, None, None, None]
    skip = eye_t * (jnp.eye(hs, dtype=F32)[None] * d_skip.astype(F32)[:, :, None])[None, None]
    hp = lax.Precision.HIGHEST
    toep = (jnp.einsum('ktx,xgoi->ktgoi', hot_f, taps[:, 0], precision=hp)
            + jnp.einsum('ktx,xgoi->ktgoi', hot_r, taps[:, 1], precision=hp) + skip)
    toep = toep.transpose(2, 0, 4, 1, 3).reshape(SSM_GROUPS, S5_TILE, S5_TILE)

    to_in = lambda m: m.transpose(1, 0, 3, 2).reshape(SSM_GROUPS, S5_TILE, SSM_STATE)
    f_re, f_im = times_bbar(*powers(t_len - 1 - tk))
    r_re, r_im = times_bbar(*powers(tk))
    b_mat = jnp.concatenate([to_in(f_re[:, 0]), to_in(r_re[:, 1]), to_in(f_im[:, 0]), to_in(r_im[:, 1])], axis=2)

    to_out = lambda m: m.transpose(1, 3, 0, 2).reshape(SSM_GROUPS, SSM_STATE, S5_TILE)
    of_re, of_im = c_times(*powers(tk + 1))
    or_re, or_im = c_times(*powers(t_len - tk))
    m_mat = jnp.concatenate([to_out(of_re[:, 0]), to_out(or_re[:, 1]), to_out(-of_im[:, 0]), to_out(-or_im[:, 1])],
                            axis=1)

    def chunk_powers(n_fwd, n_rev):
        (fr, fi), (rr, ri) = powers(n_fwd * t_len), powers(n_rev * t_len)
        return jnp.concatenate([fr[:, 0], rr[:, 1], fi[:, 0], ri[:, 1]], axis=-1)
    doubling = jnp.array([1, 2, 4])
    steps = chunk_powers(doubling, doubling)
    rows = jnp.arange(S5_SCAN_ROWS)
    carry = chunk_powers(rows + 1, S5_SCAN_ROWS - rows)
    return toep.astype(BF16), b_mat.astype(BF16), m_mat.astype(BF16), steps, carry


def _s5_state_in_kernel(u_ref, b_ref, z_ref, *, bsz):
    for b in range(bsz):
        cols = slice(S5_TILE * b, S5_TILE * (b + 1))
        z_ref[:, cols] = _dot(u_ref[:, cols], b_ref[...])


def _s5_scan_kernel(z_ref, step_ref, carry_ref, s_ref, *, n_tiles, pairs):
    rows = S5_SCAN_ROWS
    lane = lax.broadcasted_iota(jnp.int32, (rows, LANES), 1)
    sub = lax.broadcasted_iota(jnp.int32, (rows, LANES), 0)
    is_fwd = lane < SSM_STATE
    is_rev = jnp.logical_not(is_fwd)
    both = lambda fwd_rows, rev_rows: jnp.logical_or(jnp.logical_and(is_fwd, fwd_rows),
                                                     jnp.logical_and(is_rev, rev_rows))
    edge = both(sub == 0, sub == rows - 1)

    def upstream(x, dist):
        valid = both(sub >= dist, sub < rows - dist)
        return jnp.where(valid, jnp.where(is_fwd, pltpu.roll(x, dist, 0), pltpu.roll(x, rows - dist, 0)), 0.0)

    def step(k, carry):
        rf = pl.multiple_of(k * rows, rows)
        rr = pl.multiple_of((n_tiles - 1 - k) * rows, rows)
        new = []
        for j in range(pairs):
            cre = slice(S5_STATE_COLS * j, S5_STATE_COLS * j + LANES)
            cim = slice(S5_STATE_COLS * j + LANES, S5_STATE_COLS * (j + 1))
            c_re, c_im = carry[j]
            x_re = jnp.where(is_fwd, z_ref[pl.ds(rf, rows), cre], z_ref[pl.ds(rr, rows), cre])
            x_im = jnp.where(is_fwd, z_ref[pl.ds(rf, rows), cim], z_ref[pl.ds(rr, rows), cim])
            for i, dist in enumerate((1, 2, 4)):
                lr = step_ref[i:i + 1, cre]
                li = step_ref[i:i + 1, cim]
                u_re, u_im = upstream(x_re, dist), upstream(x_im, dist)
                x_re, x_im = x_re + lr * u_re - li * u_im, x_im + lr * u_im + li * u_re
            pr, pi = carry_ref[:, cre], carry_ref[:, cim]
            a_re = x_re + pr * c_re - pi * c_im
            a_im = x_im + pr * c_im + pi * c_re
            e_re = jnp.where(edge, c_re, upstream(a_re, 1))
            e_im = jnp.where(edge, c_im, upstream(a_im, 1))
            pltpu.store(s_ref.at[pl.ds(rf, rows), cre], e_re, mask=is_fwd)
            pltpu.store(s_ref.at[pl.ds(rf, rows), cim], e_im, mask=is_fwd)
            pltpu.store(s_ref.at[pl.ds(rr, rows), cre], e_re, mask=is_rev)
            pltpu.store(s_ref.at[pl.ds(rr, rows), cim], e_im, mask=is_rev)
            last = lambda a: jnp.where(is_fwd, jnp.broadcast_to(a[rows - 1:rows], a.shape),
                                       jnp.broadcast_to(a[0:1], a.shape))
            new.append((last(a_re), last(a_im)))
        return tuple(new)

    zero = jnp.zeros((rows, LANES), F32)
    lax.fori_loop(0, n_tiles, step, tuple((zero, zero) for _ in range(pairs)))


def _s5_out_kernel(u_ref, s_ref, t_ref, m_ref, y_ref, *, bsz):
    for b in range(bsz):
        cols = slice(S5_TILE * b, S5_TILE * (b + 1))
        y = _dot(u_ref[:, cols], t_ref[...]) + _dot(s_ref[:, cols].astype(BF16), m_ref[...])
        y_ref[:, cols] = y.astype(y_ref.dtype)


def _s5_bidirectional(u2, mats, layer, bsz, seq_len):
    toep, b_mat, m_mat, steps, carry = mats
    g = SSM_GROUPS
    n_chunks = seq_len // SSM_CHUNK
    assert n_chunks % S5_SCAN_ROWS == 0
    width = bsz * S5_TILE
    u_spec = pl.BlockSpec((None, n_chunks, width), lambda j: (j, 0, 0))
    w_spec = pl.BlockSpec((None, None, S5_TILE, S5_TILE), lambda j: (layer, j, 0, 0))
    col_spec = pl.BlockSpec((n_chunks, width), lambda j: (0, j))
    z = pl.pallas_call(
        functools.partial(_s5_state_in_kernel, bsz=bsz),
        grid=(g,),
        in_specs=[u_spec, w_spec],
        out_specs=col_spec,
        out_shape=jax.ShapeDtypeStruct((n_chunks, g * width), F32),
        compiler_params=_params("parallel"),
        name="s5_state_in",
    )(u2, b_mat)
    s_prev = pl.pallas_call(
        functools.partial(_s5_scan_kernel, n_tiles=n_chunks // S5_SCAN_ROWS, pairs=bsz),
        grid=(g,),
        in_specs=[col_spec, pl.BlockSpec((None, steps.shape[1], width), lambda j: (layer, 0, j)),
                  pl.BlockSpec((None, S5_SCAN_ROWS, width), lambda j: (layer, 0, j))],
        out_specs=col_spec,
        out_shape=jax.ShapeDtypeStruct((n_chunks, g * width), F32),
        compiler_params=_params("parallel"),
        name="s5_scan",
    )(z, steps, carry)
    return pl.pallas_call(
        functools.partial(_s5_out_kernel, bsz=bsz),
        grid=(g,),
        in_specs=[u_spec, col_spec, w_spec, w_spec],
        out_specs=u_spec,
        out_shape=jax.ShapeDtypeStruct((g, n_chunks, width), BF16),
        compiler_params=_params("parallel"),
        name="s5_out",
    )(u2, s_prev, toep, m_mat)


def _merge_kernel(x_ref, attn_ref, nat_ref, y_ref, gate_ref, wglu_ref, wb_ref, wout_ref, gain_ref, bias_ref, o_ref,
                  y_scr):
    _from_chunk_layout(y_ref, y_scr, x_ref.shape[0] // SSM_CHUNK)
    z = jax.nn.gelu(jnp.concatenate([y_scr[q] for q in range(y_scr.shape[0])], axis=1))
    ssm = z * jax.nn.sigmoid(_dot(z.astype(BF16), wglu_ref[...]))
    d = D_MODEL
    merged = gate_ref[:, 0:d].astype(F32) * _dot(attn_ref[...], wb_ref[0])
    merged += gate_ref[:, d:2 * d].astype(F32) * _dot(nat_ref[...], wb_ref[1])
    merged += gate_ref[:, 2 * d:3 * d].astype(F32) * _dot(ssm.astype(BF16), wb_ref[2])
    mix = _dot(merged.astype(BF16), wout_ref[...])
    o_ref[...] = _layer_norm(DEEPNORM_ALPHA * x_ref[...] + mix, gain_ref[...], bias_ref[...])


def _merge(x2d, attn_o, nat_o, y_ssm, gates, w_glu, w_branch, w_out, layer, gain, bias, seq_len, tm):
    m = x2d.shape[0]
    nl = seq_len // tm
    row = lambda width: pl.BlockSpec((tm, width), lambda i: (i, 0))
    gain, bias = gain.astype(F32)[None, :], bias.astype(F32)[None, :]
    return pl.pallas_call(
        _merge_kernel,
        grid=(m // tm,),
        in_specs=[row(D_MODEL), row(MIX_WIDTH), row(MIX_WIDTH),
                  pl.BlockSpec((SSM_GROUPS, tm // SSM_CHUNK, S5_TILE), lambda i: (0, i % nl, i // nl)),
                  row(GATE_WIDTH), _resident_layer(w_glu, layer), _resident_layer(w_branch, layer),
                  _resident_layer(w_out, layer), _resident(gain), _resident(bias)],
        out_specs=row(D_MODEL),
        out_shape=jax.ShapeDtypeStruct((m, D_MODEL), F32),
        scratch_shapes=[pltpu.VMEM((MIX_WIDTH // LANES, tm, LANES), F32)],
        compiler_params=_params("parallel"),
        name="merge",
    )(x2d, attn_o, nat_o, y_ssm, gates, w_glu, w_branch, w_out, gain, bias)


def _ffn_kernel(x_ref, wup_ref, wdown_ref, gain_ref, bias_ref, o_ref, xb_scr, acc_scr):
    f = pl.program_id(1)

    @pl.when(f == 0)
    def _():
        xb_scr[...] = x_ref[...].astype(BF16)
        acc_scr[...] = jnp.zeros_like(acc_scr)

    h = jnp.maximum(_dot(xb_scr[...], wup_ref[...]), 0.0)
    acc_scr[...] += _dot((h * h).astype(BF16), wdown_ref[...])

    @pl.when(f == pl.num_programs(1) - 1)
    def _():
        o_ref[...] = _layer_norm(DEEPNORM_ALPHA * x_ref[...] + acc_scr[...], gain_ref[...], bias_ref[...])


def _ffn(x2d, w_up, w_down, layer, gain, bias, tm, tf):
    m = x2d.shape[0]
    gain, bias = gain.astype(F32)[None, :], bias.astype(F32)[None, :]
    vec = pl.BlockSpec((1, D_MODEL), lambda i, f: (0, 0))
    return pl.pallas_call(
        _ffn_kernel,
        grid=(m // tm, FFN_DIM // tf),
        in_specs=[pl.BlockSpec((tm, D_MODEL), lambda i, f: (i, 0)),
                  pl.BlockSpec((None, D_MODEL, tf), lambda i, f: (layer, 0, f)),
                  pl.BlockSpec((None, tf, D_MODEL), lambda i, f: (layer, f, 0)),
                  vec, vec],
        out_specs=pl.BlockSpec((tm, D_MODEL), lambda i, f: (i, 0)),
        out_shape=jax.ShapeDtypeStruct((m, D_MODEL), F32),
        scratch_shapes=[pltpu.VMEM((tm, D_MODEL), BF16), pltpu.VMEM((tm, D_MODEL), F32)],
        compiler_params=_params("parallel", "arbitrary"),
        name="ffn",
    )(x2d, w_up, w_down, gain, bias)


def _tile_sizes(seq_len):
    proj_tm = min(512, seq_len)
    ffn_tm = min(1024, seq_len)
    ffn_tf = 1024
    return proj_tm, ffn_tm, ffn_tf


def kernel(x, w_in, q_norm_gain, k_norm_gain, nat_rel_bias, ssm_a_re, ssm_a_im, ssm_log_dt, ssm_b_re, ssm_b_im, ssm_c_re, ssm_c_im, ssm_d, ssm_w_glu, w_branch, w_out, ln1_gain, ln1_bias, w_ffn_up, w_ffn_down, ln2_gain, ln2_bias):
    bsz, seq_len, _ = x.shape
    proj_tm, ffn_tm, ffn_tf = _tile_sizes(seq_len)
    rope = _rope_tables(seq_len)
    w_bf = w_in.astype(BF16)
    wt = _transposed_proj_weights(w_bf)
    toep, b_mat, m_mat, steps, carry = jax.vmap(_s5_matrices)(
        ssm_a_re, ssm_a_im, ssm_log_dt, ssm_b_re, ssm_b_im, ssm_c_re, ssm_c_im, ssm_d)
    per_batch = lambda t: jnp.tile(t[:, :, :, None, :], (1, 1, 1, bsz, 1)).reshape(t.shape[0], t.shape[1], -1)
    mats = (toep, b_mat, m_mat, per_batch(steps), per_batch(carry))
    nat_bias = jax.vmap(_nat_bias_table)(nat_rel_bias)
    w_glu, w_br, w_o = ssm_w_glu.astype(BF16), w_branch.astype(BF16), w_out.astype(BF16)
    w_up, w_down = w_ffn_up.astype(BF16), w_ffn_down.astype(BF16)
    h = x.reshape(bsz * seq_len, D_MODEL)
    for layer in range(w_in.shape[0]):
        (qt, vt, nkt, k, nq, nv, u2, gates), bound = _input_projections(
            h, w_bf, wt, layer, q_norm_gain[layer], k_norm_gain[layer], rope, bsz, seq_len, proj_tm)
        attn_o = _gqa_attention(qt, k, vt, bound, bsz, seq_len)
        nat_o = _neighbourhood_attention(nq, nkt, nv, nat_bias, layer, bsz, seq_len)
        y2 = _s5_bidirectional(u2, mats, layer, bsz, seq_len)
        h = _merge(h, attn_o, nat_o, y2, gates, w_glu, w_br, w_o, layer, ln1_gain[layer], ln1_bias[layer],
                   seq_len, proj_tm)
        h = _ffn(h, w_up, w_down, layer, ln2_gain[layer], ln2_bias[layer], ffn_tm, ffn_tf)
    return h.reshape(bsz, seq_len, D_MODEL)
```

```python
import functools

import jax
import jax.numpy as jnp
from jax import lax
from jax.experimental import pallas as pl
from jax.experimental.pallas import tpu as pltpu

D_MODEL = 1024
DEPTH = 2
GRID_W = 64
HEAD_DIM = 64
MIX_WIDTH = 512
ATTN_Q_HEADS = 8
ATTN_KV_HEADS = 2
ATTN_GROUP = ATTN_Q_HEADS // ATTN_KV_HEADS
NAT_HEADS = 8
NAT_WIN_ROWS = 8
NAT_WIN_COLS = 16
SSM_GROUP = 16
SSM_GROUPS = 32
SSM_STATE = 64
SSM_CHUNK = 16
N_BRANCHES = 3
FFN_DIM = 4 * D_MODEL
ROPE_THETA = 10000.0
LN_EPS = 1e-5
RMS_EPS = 1e-6
DEEPNORM_ALPHA = (2 * DEPTH) ** 0.25
ATTN_SCALE = HEAD_DIM ** -0.5
LOG2_E = 1.4426950408889634
MASK_VALUE = -1e30

Q_WIDTH = ATTN_Q_HEADS * HEAD_DIM
KV_WIDTH = ATTN_KV_HEADS * HEAD_DIM
GATE_WIDTH = N_BRANCHES * D_MODEL

LANES = 128
SUBLANES = 8
MXU_WIDTH = 256
V7X_VMEM_BYTES = 64 * 1024 * 1024
VMEM_LIMIT = V7X_VMEM_BYTES - 8 * 1024 * 1024

F32 = jnp.float32
BF16 = jnp.bfloat16
NT_DIMS = (((1,), (1,)), ((), ()))


def _params(*semantics):
    return pltpu.CompilerParams(dimension_semantics=semantics, vmem_limit_bytes=VMEM_LIMIT)


def _dot(a, b):
    return jnp.dot(a, b, preferred_element_type=F32)


def _layer_norm(x, gain, bias):
    mu = jnp.mean(x, axis=-1, keepdims=True)
    xc = x - mu
    var = jnp.mean(xc * xc, axis=-1, keepdims=True)
    return xc * lax.rsqrt(var + LN_EPS) * gain + bias


def _resident(a):
    return pl.BlockSpec(a.shape, lambda *_: (0,) * a.ndim, pipeline_mode=pl.Buffered(1))


def _resident_layer(a, layer):
    return pl.BlockSpec((None,) + a.shape[1:], lambda *_: (layer,) + (0,) * (a.ndim - 1),
                        pipeline_mode=pl.Buffered(1))


W_IN_OFFSETS = (0, Q_WIDTH, Q_WIDTH + KV_WIDTH, Q_WIDTH + 2 * KV_WIDTH, Q_WIDTH + 2 * KV_WIDTH + MIX_WIDTH,
                Q_WIDTH + 2 * KV_WIDTH + 2 * MIX_WIDTH, Q_WIDTH + 2 * KV_WIDTH + 3 * MIX_WIDTH,
                Q_WIDTH + 2 * KV_WIDTH + 4 * MIX_WIDTH)


ATTN_KV_PAGE = 512
ATTN_TQ = MXU_WIDTH
ATTN_Q_BLOCKS_PER_STEP = 2
ATTN_EXT_DIM = LANES
ATTN_V_ROWS = HEAD_DIM + 16
ATTN_BOUND_LIMIT = 60.0
PROJ_T_ROWS = Q_WIDTH + KV_WIDTH + MIX_WIDTH
S5_TILE = SSM_CHUNK * SSM_GROUP
S5_STATE_COLS = 4 * SSM_STATE
S5_SCAN_ROWS = SUBLANES


GROUPS_PER_TILE = LANES // SSM_GROUP
CHUNKS_PER_TILE = LANES // SSM_GROUP


def _to_chunk_layout(x_scr, u_ref, n_chunks):
    lane_grp = lax.broadcasted_iota(jnp.int32, (n_chunks, LANES), 1) // SSM_GROUP
    for half in range(SSM_CHUNK // CHUNKS_PER_TILE):
        for q in range(x_scr.shape[0]):
            steps = [x_scr[q, pl.ds(CHUNKS_PER_TILE * half + tp, n_chunks, stride=SSM_CHUNK), :]
                     for tp in range(CHUNKS_PER_TILE)]
            for gm in range(GROUPS_PER_TILE):
                tile = None
                for tp in range(CHUNKS_PER_TILE):
                    shift = (SSM_GROUP * (tp - gm)) % LANES
                    moved = pltpu.roll(steps[tp], shift, 1) if shift else steps[tp]
                    tile = moved if tile is None else jnp.where(lane_grp == tp, moved, tile)
                u_ref[GROUPS_PER_TILE * q + gm, :, LANES * half:LANES * (half + 1)] = tile.astype(u_ref.dtype)


def _from_chunk_layout(y_ref, y_scr, n_chunks):
    lane_grp = lax.broadcasted_iota(jnp.int32, (n_chunks, LANES), 1) // SSM_GROUP
    for half in range(SSM_CHUNK // CHUNKS_PER_TILE):
        for q in range(y_scr.shape[0]):
            groups = [y_ref[GROUPS_PER_TILE * q + gm, :, LANES * half:LANES * (half + 1)].astype(F32)
                      for gm in range(GROUPS_PER_TILE)]
            for tp in range(CHUNKS_PER_TILE):
                tile = None
                for gm in range(GROUPS_PER_TILE):
                    shift = (SSM_GROUP * (gm - tp)) % LANES
                    moved = pltpu.roll(groups[gm], shift, 1) if shift else groups[gm]
                    tile = moved if tile is None else jnp.where(lane_grp == gm, moved, tile)
                y_scr[q, pl.ds(CHUNKS_PER_TILE * half + tp, n_chunks, stride=SSM_CHUNK), :] = tile


def _proj_kernel(x_ref, wt_ref, w_ref, qgain_ref, kgain_ref, cost_ref, sint_ref, cos_ref, sin_ref, seg_ref,
                 qext_ref, vext_ref, qt_ref, vt_ref, nkt_ref, k_ref, nq_ref, nv_ref, u_ref, g_ref, su_scr):
    tm = x_ref.shape[0]
    d = HEAD_DIM
    xb = x_ref[...].astype(BF16)

    yt = lax.dot_general(wt_ref[...], xb, NT_DIMS, preferred_element_type=F32)
    cost = cost_ref[...]
    sint = sint_ref[...]
    qgain = qgain_ref[...]
    for h in range(ATTN_Q_HEADS):
        blk = yt[d * h:d * (h + 1), :]
        ms = jnp.mean(blk * blk, axis=0, keepdims=True)
        yn = blk * lax.rsqrt(ms + RMS_EPS) * qgain
        partner = jnp.concatenate([yn[16:32], yn[0:16], yn[48:64], yn[32:48]], axis=0)
        qh = ((yn * cost + partner * sint) * (ATTN_SCALE * LOG2_E)).astype(BF16)
        j, g = divmod(h, ATTN_GROUP)
        for qb in range(tm // ATTN_TQ):
            qt_ref[j, qb, 0:d, ATTN_TQ * g:ATTN_TQ * (g + 1)] = qh[:, ATTN_TQ * qb:ATTN_TQ * (qb + 1)]
    for j in range(ATTN_KV_HEADS):
        for qb in range(tm // ATTN_TQ):
            qt_ref[j, qb, d:ATTN_EXT_DIM, :] = qext_ref[...]
    vt = yt[Q_WIDTH:Q_WIDTH + KV_WIDTH, :].astype(BF16)
    for pg in range(tm // ATTN_KV_PAGE):
        cols = slice(ATTN_KV_PAGE * pg, ATTN_KV_PAGE * (pg + 1))
        for j in range(ATTN_KV_HEADS):
            vt_ref[pg, ATTN_V_ROWS * j:ATTN_V_ROWS * j + d, :] = vt[d * j:d * (j + 1), cols]
            vt_ref[pg, ATTN_V_ROWS * j + d:ATTN_V_ROWS * (j + 1), :] = vext_ref[...]
    nkt_ref[...] = yt[Q_WIDTH + KV_WIDTH:, :].astype(BF16)

    o_ak, o_nq, o_nv, o_su, o_gate = (W_IN_OFFSETS[i] for i in (1, 3, 5, 6, 7))
    yk = _dot(xb, w_ref[:, o_ak:o_ak + KV_WIDTH])
    y2 = yk * yk
    hi = y2.astype(BF16)
    lo = (y2 - hi.astype(F32)).astype(BF16)
    ms = (_dot(hi, seg_ref[...]) + _dot(lo, seg_ref[...])) * (1.0 / d)
    kn = yk * lax.rsqrt(ms + RMS_EPS) * kgain_ref[...]
    lane = lax.broadcasted_iota(jnp.int32, (tm, LANES), 1)
    partner = jnp.where((lane % 32) < 16, pltpu.roll(kn, LANES - 16, 1), pltpu.roll(kn, 16, 1))
    kk = kn * cos_ref[...] + partner * sin_ref[...]
    one_hot = (lane == d).astype(F32)
    for j in range(ATTN_KV_HEADS):
        kj = kk if j == 0 else pltpu.roll(kk, LANES - d * j, 1)
        k_ref[j] = jnp.where(lane < d, kj, one_hot).astype(BF16)

    ynq = _dot(xb, w_ref[:, o_nq:o_nq + MIX_WIDTH]) * ATTN_SCALE
    ynv = _dot(xb, w_ref[:, o_nv:o_nv + MIX_WIDTH])
    for h in range(NAT_HEADS):
        nq_ref[h] = ynq[:, d * h:d * (h + 1)].astype(BF16)
        nv_ref[h] = ynv[:, d * h:d * (h + 1)].astype(BF16)
    su = _dot(xb, w_ref[:, o_su:o_su + MIX_WIDTH])
    for q in range(MIX_WIDTH // LANES):
        su_scr[q] = su[:, LANES * q:LANES * (q + 1)]
    _to_chunk_layout(su_scr, u_ref, tm // SSM_CHUNK)
    for n in range(N_BRANCHES):
        y = _dot(xb, w_ref[:, o_gate + D_MODEL * n:o_gate + D_MODEL * (n + 1)])
        g_ref[:, D_MODEL * n:D_MODEL * (n + 1)] = jax.nn.sigmoid(y).astype(BF16)


def _rope_tables(seq_len):
    t = jnp.arange(seq_len)
    row = (t // GRID_W).astype(F32)
    col = (t % GRID_W).astype(F32)
    axis_dim = HEAD_DIM // 2
    inv_freq = 1.0 / (ROPE_THETA ** (jnp.arange(0, axis_dim, 2, dtype=F32) / axis_dim))
    ang_r = row[:, None] * inv_freq[None, :]
    ang_c = col[:, None] * inv_freq[None, :]
    cos_head = jnp.concatenate([jnp.cos(ang_r), jnp.cos(ang_r), jnp.cos(ang_c), jnp.cos(ang_c)], axis=1)
    sin_head = jnp.concatenate([-jnp.sin(ang_r), jnp.sin(ang_r), -jnp.sin(ang_c), jnp.sin(ang_c)], axis=1)
    reps = LANES // HEAD_DIM
    return (jnp.tile(cos_head, (1, reps)), jnp.tile(sin_head, (1, reps)), cos_head.T, sin_head.T)


def _transposed_proj_weights(w_bf):
    o = W_IN_OFFSETS
    parts = [w_bf[:, :, o[0]:o[1]], w_bf[:, :, o[2]:o[3]], w_bf[:, :, o[4]:o[5]]]
    return jnp.concatenate(parts, axis=2).transpose(0, 2, 1)


def _input_projections(x2d, w_bf, wt, layer, q_gain, k_gain, rope, bsz, seq_len, tm):
    m = x2d.shape[0]
    nl = seq_len // tm
    pages = tm // ATTN_KV_PAGE
    qgain = jnp.broadcast_to(q_gain.astype(F32)[:, None], (HEAD_DIM, tm))
    kgain = jnp.tile(k_gain.astype(F32), ATTN_KV_HEADS)[None, :]
    seg = (jnp.arange(KV_WIDTH)[:, None] // HEAD_DIM == jnp.arange(KV_WIDTH)[None, :] // HEAD_DIM).astype(BF16)
    bound = (HEAD_DIM * ATTN_SCALE * LOG2_E * 1.02) * jnp.max(jnp.abs(q_gain.astype(F32))) * jnp.max(jnp.abs(k_gain.astype(F32)))
    first_row = lambda rows, width: (jnp.arange(rows)[:, None] == 0) & (jnp.arange(width)[None, :] >= 0)
    qext = jnp.where(first_row(ATTN_EXT_DIM - HEAD_DIM, ATTN_GROUP * ATTN_TQ), -bound, 0.0).astype(BF16)
    vext = first_row(ATTN_V_ROWS - HEAD_DIM, ATTN_KV_PAGE).astype(BF16)
    cos, sin, cos_t, sin_t = rope
    tok = lambda width: pl.BlockSpec((tm, width), lambda i: (i, 0))
    heads = lambda n, width=HEAD_DIM: pl.BlockSpec((n, tm, width), lambda i: (0, i, 0))
    feat_t = lambda rows: pl.BlockSpec((None, rows, tm), lambda i: (i // nl, 0, i % nl))
    outs = pl.pallas_call(
        _proj_kernel,
        grid=(m // tm,),
        in_specs=[tok(D_MODEL), _resident_layer(wt, layer), _resident_layer(w_bf, layer), _resident(qgain), _resident(kgain),
                  pl.BlockSpec((HEAD_DIM, tm), lambda i: (0, i % nl)),
                  pl.BlockSpec((HEAD_DIM, tm), lambda i: (0, i % nl)),
                  pl.BlockSpec((tm, LANES), lambda i: (i % nl, 0)),
                  pl.BlockSpec((tm, LANES), lambda i: (i % nl, 0)),
                  _resident(seg), _resident(qext), _resident(vext)],
        out_specs=[pl.BlockSpec((None, ATTN_KV_HEADS, tm // ATTN_TQ, ATTN_EXT_DIM, ATTN_GROUP * ATTN_TQ),
                                lambda i: (i // nl, 0, i % nl, 0, 0)),
                   pl.BlockSpec((None, pages, ATTN_KV_HEADS * ATTN_V_ROWS, ATTN_KV_PAGE),
                                lambda i: (i // nl, i % nl, 0, 0)),
                   feat_t(MIX_WIDTH),
                   heads(ATTN_KV_HEADS, ATTN_EXT_DIM), heads(NAT_HEADS), heads(NAT_HEADS),
                   pl.BlockSpec((SSM_GROUPS, tm // SSM_CHUNK, S5_TILE), lambda i: (0, i % nl, i // nl)),
                   tok(GATE_WIDTH)],
        out_shape=[jax.ShapeDtypeStruct((bsz, ATTN_KV_HEADS, seq_len // ATTN_TQ, ATTN_EXT_DIM, ATTN_GROUP * ATTN_TQ), BF16),
                   jax.ShapeDtypeStruct((bsz, seq_len // ATTN_KV_PAGE, ATTN_KV_HEADS * ATTN_V_ROWS, ATTN_KV_PAGE), BF16),
                   jax.ShapeDtypeStruct((bsz, MIX_WIDTH, seq_len), BF16),
                   jax.ShapeDtypeStruct((ATTN_KV_HEADS, m, ATTN_EXT_DIM), BF16),
                   jax.ShapeDtypeStruct((NAT_HEADS, m, HEAD_DIM), BF16),
                   jax.ShapeDtypeStruct((NAT_HEADS, m, HEAD_DIM), BF16),
                   jax.ShapeDtypeStruct((SSM_GROUPS, seq_len // SSM_CHUNK, bsz * S5_TILE), BF16),
                   jax.ShapeDtypeStruct((m, GATE_WIDTH), BF16)],
        scratch_shapes=[pltpu.VMEM((MIX_WIDTH // LANES, tm, LANES), F32)],
        compiler_params=_params("parallel"),
        name="in_proj",
    )(x2d, wt, w_bf, qgain, kgain, cos_t, sin_t, cos, sin, seg, qext, vext)
    return outs, bound


def _finish_attention(o_ref, qb, o):
    o_ref[ATTN_TQ * qb:ATTN_TQ * (qb + 1), :] = jnp.concatenate(
        [o[:, ATTN_TQ * g:ATTN_TQ * (g + 1)].T for g in range(ATTN_GROUP)], axis=1).astype(o_ref.dtype)


def _flash_bounded_kernel(qg_ref, k_ref, vt_ref, o_ref, *, n_pages):
    d = HEAD_DIM
    for qb in range(qg_ref.shape[0]):
        qg = qg_ref[qb]
        acc = jnp.zeros((ATTN_V_ROWS, qg.shape[1]), F32)
        for c in range(n_pages):
            s = _dot(k_ref[ATTN_KV_PAGE * c:ATTN_KV_PAGE * (c + 1), :], qg)
            acc = acc + _dot(vt_ref[c], jnp.exp2(s).astype(BF16))
        _finish_attention(o_ref, qb, acc[:d] / acc[d:d + 1])


def _flash_online_kernel(qg_ref, k_ref, vt_ref, o_ref, acc_scr, *, n_pages):
    d = HEAD_DIM
    for qb in range(qg_ref.shape[0]):
        qg = qg_ref[qb]
        nq = qg.shape[1]
        acc_scr[...] = jnp.zeros_like(acc_scr)

        def scores(c):
            return _dot(k_ref[ATTN_KV_PAGE * c:ATTN_KV_PAGE * (c + 1), :], qg)

        m = jnp.full((1, nq), -jnp.inf, F32)
        l = jnp.zeros((1, nq), F32)
        s_next = scores(0)
        for c in range(n_pages):
            s = s_next
            if c + 1 < n_pages:
                s_next = scores(c + 1)
            m_new = jnp.maximum(m, jnp.max(s, axis=0, keepdims=True))
            alpha = jnp.exp2(m - m_new)
            p = jnp.exp2(s - m_new)
            l = alpha * l + jnp.sum(p, axis=0, keepdims=True)
            acc_scr[...] = alpha * acc_scr[...] + _dot(vt_ref[c, 0:d, :], p.astype(BF16))
            m = m_new
        _finish_attention(o_ref, qb, acc_scr[...] / l)


def _gqa_attention(qg, k, vt, bound, bsz, seq_len):
    d, hk, grp, tq = HEAD_DIM, ATTN_KV_HEADS, ATTN_GROUP, ATTN_TQ
    nqb = ATTN_Q_BLOCKS_PER_STEP
    nq, n_pages = seq_len // (tq * nqb), seq_len // ATTN_KV_PAGE

    def call(body, scratch):
        return pl.pallas_call(
            functools.partial(body, n_pages=n_pages),
            grid=(bsz, hk, nq),
            in_specs=[pl.BlockSpec((None, None, nqb, ATTN_EXT_DIM, grp * tq), lambda b, j, i: (b, j, i, 0, 0)),
                      pl.BlockSpec((None, seq_len, ATTN_EXT_DIM), lambda b, j, i: (j, b, 0)),
                      pl.BlockSpec((None, n_pages, ATTN_V_ROWS, ATTN_KV_PAGE), lambda b, j, i: (b, 0, j, 0))],
            out_specs=pl.BlockSpec((tq * nqb, grp * d), lambda b, j, i: (b * nq + i, j)),
            out_shape=jax.ShapeDtypeStruct((bsz * seq_len, ATTN_Q_HEADS * d), BF16),
            scratch_shapes=scratch,
            compiler_params=_params("parallel", "parallel", "parallel"),
            name=body.__name__.strip("_").replace("_kernel", ""),
        )(qg, k, vt)

    return lax.cond(bound <= ATTN_BOUND_LIMIT,
                    lambda: call(_flash_bounded_kernel, []),
                    lambda: call(_flash_online_kernel, [pltpu.VMEM((d, grp * tq), F32)]))


NAT_ROWS_PER_STEP = 8
NAT_HEAD_UNROLL = 8
NAT_STEP_TOKENS = NAT_ROWS_PER_STEP * GRID_W
NAT_WIN_TOKENS = NAT_WIN_ROWS * GRID_W


def _nat_kernel(q_ref, ktp_ref, ktc_ref, ktn_ref, vp_ref, vc_ref, vn_ref, bias_ref, o_ref, kt_scr, v_scr, oh_scr):
    g = pl.program_id(1)
    ng = pl.num_programs(1)
    st = NAT_STEP_TOKENS
    d = HEAD_DIM
    kt_scr[:, 0:st] = ktp_ref[...]
    kt_scr[:, st:2 * st] = ktc_ref[...]
    kt_scr[:, 2 * st:3 * st] = ktn_ref[...]
    v_scr[:, 0:st, :] = vp_ref[...]
    v_scr[:, st:2 * st, :] = vc_ref[...]
    v_scr[:, 2 * st:3 * st, :] = vn_ref[...]

    def all_heads(frame_rows, bias_index):
        starts = [fr * GRID_W for fr in frame_rows]

        def head(h, carry):
            q = q_ref[h]
            r0 = pl.multiple_of(h * d, d)
            s = jnp.concatenate(
                [_dot(q[GRID_W * r:GRID_W * (r + 1), :],
                      kt_scr[pl.ds(r0, d), starts[r]:starts[r] + NAT_WIN_TOKENS]) + bias_ref[bias_index[r], h]
                 for r in range(NAT_ROWS_PER_STEP)], axis=0)
            m = jnp.max(s, axis=1, keepdims=True)
            p = jnp.exp(s - m)
            l = jnp.sum(p, axis=1, keepdims=True)
            pb = p.astype(BF16)
            o = jnp.concatenate(
                [_dot(pb[GRID_W * r:GRID_W * (r + 1), :], v_scr[h, starts[r]:starts[r] + NAT_WIN_TOKENS, :])
                 for r in range(NAT_ROWS_PER_STEP)], axis=0)
            oh_scr[h] = o / l
            return carry

        lax.fori_loop(0, NAT_HEADS, head, 0, unroll=NAT_HEAD_UNROLL)

    half = NAT_WIN_ROWS // 2
    n = NAT_ROWS_PER_STEP

    @pl.when(g == 0)
    def _():
        all_heads([n + max(r - half, 0) for r in range(n)], [min(r, half) for r in range(n)])

    @pl.when(jnp.logical_and(g > 0, g < ng - 1))
    def _():
        all_heads([r + half for r in range(n)], [half] * n)

    @pl.when(jnp.logical_and(g == ng - 1, g > 0))
    def _():
        all_heads([min(r + half, n) for r in range(n)], [max(r, half) for r in range(n)])

    o_ref[...] = jnp.concatenate([oh_scr[h] for h in range(NAT_HEADS)], axis=1).astype(o_ref.dtype)


def _nat_bias_table(rel_bias):
    w, nb = GRID_W, 2 * NAT_WIN_COLS - 1
    cols = jnp.arange(w)
    col_start = jnp.clip(cols - NAT_WIN_COLS // 2, 0, w - NAT_WIN_COLS)
    in_win = (cols[None, :] >= col_start[:, None]) & (cols[None, :] < col_start[:, None] + NAT_WIN_COLS)
    v = jnp.arange(NAT_WIN_ROWS)
    row_off = v[None, :] - v[:, None] + (NAT_WIN_ROWS - 1)
    row_hot = (row_off[:, :, None] == jnp.arange(2 * NAT_WIN_ROWS - 1)[None, None, :]).astype(F32)
    by_row = jnp.sum(rel_bias.astype(F32)[:, None, None] * row_hot[None, :, :, :, None], axis=3)
    period = nb + w + 1
    padded = jnp.pad(by_row, ((0, 0), (0, 0), (0, 0), (0, period - nb)))
    skewed = jnp.tile(padded, (1, 1, 1, w))[..., :w * (period - 1)].reshape(by_row.shape[:3] + (w, period - 1))
    b = skewed[..., NAT_WIN_COLS - 1:NAT_WIN_COLS - 1 + w]
    b = jnp.where(in_win[None, None, None], b, MASK_VALUE).transpose(1, 0, 3, 2, 4)
    return b.reshape(NAT_WIN_ROWS, NAT_HEADS, w, NAT_WIN_TOKENS)


def _neighbourhood_attention(nq, nkt, nv, bias, layer, bsz, seq_len):
    h, d, st = NAT_HEADS, HEAD_DIM, NAT_STEP_TOKENS
    ng = seq_len // st
    assert ng >= 2 and seq_len // GRID_W >= NAT_WIN_ROWS
    prev = lambda g: jnp.maximum(g - 1, 0)
    nxt = lambda g: jnp.minimum(g + 1, ng - 1)
    same = lambda g: g
    kt_spec = lambda f: pl.BlockSpec((None, h * d, st), lambda b, g: (b, 0, f(g)))
    v_spec = lambda f: pl.BlockSpec((h, st, d), lambda b, g: (0, b * ng + f(g), 0))
    return pl.pallas_call(
        _nat_kernel,
        grid=(bsz, ng),
        in_specs=[v_spec(same), kt_spec(prev), kt_spec(same), kt_spec(nxt),
                  v_spec(prev), v_spec(same), v_spec(nxt), _resident_layer(bias, layer)],
        out_specs=pl.BlockSpec((st, h * d), lambda b, g: (b * ng + g, 0)),
        out_shape=jax.ShapeDtypeStruct((bsz * seq_len, h * d), BF16),
        scratch_shapes=[pltpu.VMEM((h * d, 3 * st), BF16), pltpu.VMEM((h, 3 * st, d), BF16),
                        pltpu.VMEM((h, st, d), F32)],
        compiler_params=_params("parallel", "parallel"),
        name="nat",
    )(nq, nkt, nkt, nkt, nv, nv, nv, bias)


def _s5_matrices(a_re, a_im, log_dt, b_re, b_im, c_re, c_im, d_skip):
    t_len, hs = SSM_CHUNK, SSM_GROUP
    a_re = a_re.astype(F32)
    a_im = a_im.astype(F32)
    dt = jnp.exp(log_dt.astype(F32))[..., None]
    decay = jnp.exp(a_re * dt)
    phase = a_im * dt
    lam_re = decay * jnp.cos(phase)
    lam_im = decay * jnp.sin(phase)
    den = a_re * a_re + a_im * a_im
    num_re = lam_re - 1.0
    coef_re = (num_re * a_re + lam_im * a_im) / den
    coef_im = (lam_im * a_re - num_re * a_im) / den
    b_re = b_re.astype(F32)[None]
    b_im = b_im.astype(F32)[None]
    bbar_re = coef_re[..., None] * b_re - coef_im[..., None] * b_im
    bbar_im = coef_re[..., None] * b_im + coef_im[..., None] * b_re
    c_re = c_re.astype(F32)
    c_im = c_im.astype(F32)

    def powers(exponents):
        e = jnp.asarray(exponents, F32)[:, None, None, None]
        mag = jnp.exp(e * (a_re * dt)[None])
        return mag * jnp.cos(e * phase[None]), mag * jnp.sin(e * phase[None])

    def c_times(p_re, p_im):
        return (c_re[None] * p_re[:, :, :, None, :] - c_im[None] * p_im[:, :, :, None, :],
                c_re[None] * p_im[:, :, :, None, :] + c_im[None] * p_re[:, :, :, None, :])

    def times_bbar(p_re, p_im):
        return (p_re[..., None] * bbar_re[None] - p_im[..., None] * bbar_im[None],
                p_re[..., None] * bbar_im[None] + p_im[..., None] * bbar_re[None])

    tk = jnp.arange(t_len)
    cl_re, cl_im = c_times(*powers(tk))
    taps = jnp.sum(cl_re[..., None] * bbar_re[None, :, :, None] - cl_im[..., None] * bbar_im[None, :, :, None], axis=4)
    lag = tk[None, :] - tk[:, None]
    hot_f = (lag[:, :, None] == tk[None, None, :]).astype(F32)
    hot_r = (-lag[:, :, None] == tk[None, None, :]).astype(F32)
    eye_t = jnp.eye(t_len, dtype=F32)[:, :, None, None, None]
    skip = eye_t * (jnp.eye(hs, dtype=F32)[None] * d_skip.astype(F32)[:, :, None])[None, None]
    hp = lax.Precision.HIGHEST
    toep = (jnp.einsum('ktx,xgoi->ktgoi', hot_f, taps[:, 0], precision=hp)
            + jnp.einsum('ktx,xgoi->ktgoi', hot_r, taps[:, 1], precision=hp) + skip)
    toep = toep.transpose(2, 0, 4, 1, 3).reshape(SSM_GROUPS, S5_TILE, S5_TILE)

    to_in = lambda m: m.transpose(1, 0, 3, 2).reshape(SSM_GROUPS, S5_TILE, SSM_STATE)
    f_re, f_im = times_bbar(*powers(t_len - 1 - tk))
    r_re, r_im = times_bbar(*powers(tk))
    b_mat = jnp.concatenate([to_in(f_re[:, 0]), to_in(r_re[:, 1]), to_in(f_im[:, 0]), to_in(r_im[:, 1])], axis=2)

    to_out = lambda m: m.transpose(1, 3, 0, 2).reshape(SSM_GROUPS, SSM_STATE, S5_TILE)
    of_re, of_im = c_times(*powers(tk + 1))
    or_re, or_im = c_times(*powers(t_len - tk))
    m_mat = jnp.concatenate([to_out(of_re[:, 0]), to_out(or_re[:, 1]), to_out(-of_im[:, 0]), to_out(-or_im[:, 1])],
                            axis=1)

    def chunk_powers(n_fwd, n_rev):
        (fr, fi), (rr, ri) = powers(n_fwd * t_len), powers(n_rev * t_len)
        return jnp.concatenate([fr[:, 0], rr[:, 1], fi[:, 0], ri[:, 1]], axis=-1)
    doubling = jnp.array([1, 2, 4])
    steps = chunk_powers(doubling, doubling)
    rows = jnp.arange(S5_SCAN_ROWS)
    carry = chunk_powers(rows + 1, S5_SCAN_ROWS - rows)
    return toep.astype(BF16), b_mat.astype(BF16), m_mat.astype(BF16), steps, carry


def _s5_kernel(u_ref, b_ref, t_ref, m_ref, step_ref, carry_ref, y_ref, z_scr, s_scr, *, n_tiles, bsz):
    for b in range(bsz):
        cols = slice(S5_TILE * b, S5_TILE * (b + 1))
        z_scr[:, cols] = _dot(u_ref[:, cols], b_ref[...])
    _s5_chunk_scan(z_scr, step_ref, carry_ref, s_scr, n_tiles=n_tiles, pairs=bsz)
    for b in range(bsz):
        cols = slice(S5_TILE * b, S5_TILE * (b + 1))
        y = _dot(u_ref[:, cols], t_ref[...]) + _dot(s_scr[:, cols].astype(BF16), m_ref[...])
        y_ref[:, cols] = y.astype(y_ref.dtype)


def _s5_chunk_scan(z_ref, step_ref, carry_ref, s_ref, *, n_tiles, pairs):
    rows = S5_SCAN_ROWS
    lane = lax.broadcasted_iota(jnp.int32, (rows, LANES), 1)
    sub = lax.broadcasted_iota(jnp.int32, (rows, LANES), 0)
    is_fwd = lane < SSM_STATE
    is_rev = jnp.logical_not(is_fwd)
    both = lambda fwd_rows, rev_rows: jnp.logical_or(jnp.logical_and(is_fwd, fwd_rows),
                                                     jnp.logical_and(is_rev, rev_rows))
    edge = both(sub == 0, sub == rows - 1)

    def upstream(x, dist):
        valid = both(sub >= dist, sub < rows - dist)
        return jnp.where(valid, jnp.where(is_fwd, pltpu.roll(x, dist, 0), pltpu.roll(x, rows - dist, 0)), 0.0)

    def step(k, carry):
        rf = pl.multiple_of(k * rows, rows)
        rr = pl.multiple_of((n_tiles - 1 - k) * rows, rows)
        new = []
        for j in range(pairs):
            cre = slice(S5_STATE_COLS * j, S5_STATE_COLS * j + LANES)
            cim = slice(S5_STATE_COLS * j + LANES, S5_STATE_COLS * (j + 1))
            c_re, c_im = carry[j]
            x_re = jnp.where(is_fwd, z_ref[pl.ds(rf, rows), cre], z_ref[pl.ds(rr, rows), cre])
            x_im = jnp.where(is_fwd, z_ref[pl.ds(rf, rows), cim], z_ref[pl.ds(rr, rows), cim])
            for i, dist in enumerate((1, 2, 4)):
                lr = step_ref[i:i + 1, cre]
                li = step_ref[i:i + 1, cim]
                u_re, u_im = upstream(x_re, dist), upstream(x_im, dist)
                x_re, x_im = x_re + lr * u_re - li * u_im, x_im + lr * u_im + li * u_re
            pr, pi = carry_ref[:, cre], carry_ref[:, cim]
            a_re = x_re + pr * c_re - pi * c_im
            a_im = x_im + pr * c_im + pi * c_re
            e_re = jnp.where(edge, c_re, upstream(a_re, 1))
            e_im = jnp.where(edge, c_im, upstream(a_im, 1))
            pltpu.store(s_ref.at[pl.ds(rf, rows), cre], e_re, mask=is_fwd)
            pltpu.store(s_ref.at[pl.ds(rf, rows), cim], e_im, mask=is_fwd)
            pltpu.store(s_ref.at[pl.ds(rr, rows), cre], e_re, mask=is_rev)
            pltpu.store(s_ref.at[pl.ds(rr, rows), cim], e_im, mask=is_rev)
            last = lambda a: jnp.where(is_fwd, jnp.broadcast_to(a[rows - 1:rows], a.shape),
                                       jnp.broadcast_to(a[0:1], a.shape))
            new.append((last(a_re), last(a_im)))
        return tuple(new)

    zero = jnp.zeros((rows, LANES), F32)
    lax.fori_loop(0, n_tiles, step, tuple((zero, zero) for _ in range(pairs)))


def _s5_bidirectional(u2, mats, layer, bsz, seq_len):
    toep, b_mat, m_mat, steps, carry = mats
    g = SSM_GROUPS
    n_chunks = seq_len // SSM_CHUNK
    assert n_chunks % S5_SCAN_ROWS == 0
    width = bsz * S5_TILE
    u_spec = pl.BlockSpec((None, n_chunks, width), lambda j: (j, 0, 0))
    w_spec = pl.BlockSpec((None, None, S5_TILE, S5_TILE), lambda j: (layer, j, 0, 0))
    return pl.pallas_call(
        functools.partial(_s5_kernel, n_tiles=n_chunks // S5_SCAN_ROWS, bsz=bsz),
        grid=(g,),
        in_specs=[u_spec, w_spec, w_spec, w_spec,
                  pl.BlockSpec((None, steps.shape[1], width), lambda j: (layer, 0, j)),
                  pl.BlockSpec((None, S5_SCAN_ROWS, width), lambda j: (layer, 0, j))],
        out_specs=u_spec,
        out_shape=jax.ShapeDtypeStruct((g, n_chunks, width), BF16),
        scratch_shapes=[pltpu.VMEM((n_chunks, width), F32), pltpu.VMEM((n_chunks, width), F32)],
        compiler_params=_params("parallel"),
        name="s5",
    )(u2, b_mat, toep, m_mat, steps, carry)


MERGE_ROW_PARTS = 2


def _merge_kernel(x_ref, attn_ref, nat_ref, y_ref, gate_ref, wglu_ref, wb_ref, wout_ref, gain_ref, bias_ref, o_ref,
                  y_scr):
    tm = x_ref.shape[0]
    _from_chunk_layout(y_ref, y_scr, tm // SSM_CHUNK)
    d = D_MODEL
    part = tm // MERGE_ROW_PARTS
    for r in range(MERGE_ROW_PARTS):
        rows = slice(part * r, part * (r + 1))
        z = jax.nn.gelu(jnp.concatenate([y_scr[q, rows, :] for q in range(y_scr.shape[0])], axis=1))
        ssm = z * jax.nn.sigmoid(_dot(z.astype(BF16), wglu_ref[...]))
        merged = gate_ref[rows, 0:d].astype(F32) * _dot(attn_ref[rows, :], wb_ref[0])
        merged += gate_ref[rows, d:2 * d].astype(F32) * _dot(nat_ref[rows, :], wb_ref[1])
        merged += gate_ref[rows, 2 * d:3 * d].astype(F32) * _dot(ssm.astype(BF16), wb_ref[2])
        mix = _dot(merged.astype(BF16), wout_ref[...])
        o_ref[rows, :] = _layer_norm(DEEPNORM_ALPHA * x_ref[rows, :] + mix, gain_ref[...], bias_ref[...])


def _merge(x2d, attn_o, nat_o, y_ssm, gates, w_glu, w_branch, w_out, layer, gain, bias, seq_len, tm):
    m = x2d.shape[0]
    nl = seq_len // tm
    row = lambda width: pl.BlockSpec((tm, width), lambda i: (i, 0))
    gain, bias = gain.astype(F32)[None, :], bias.astype(F32)[None, :]
    return pl.pallas_call(
        _merge_kernel,
        grid=(m // tm,),
        in_specs=[row(D_MODEL), row(MIX_WIDTH), row(MIX_WIDTH),
                  pl.BlockSpec((SSM_GROUPS, tm // SSM_CHUNK, S5_TILE), lambda i: (0, i % nl, i // nl)),
                  row(GATE_WIDTH), _resident_layer(w_glu, layer), _resident_layer(w_branch, layer),
                  _resident_layer(w_out, layer), _resident(gain), _resident(bias)],
        out_specs=row(D_MODEL),
        out_shape=jax.ShapeDtypeStruct((m, D_MODEL), F32),
        scratch_shapes=[pltpu.VMEM((MIX_WIDTH // LANES, tm, LANES), F32)],
        compiler_params=_params("parallel"),
        name="merge",
    )(x2d, attn_o, nat_o, y_ssm, gates, w_glu, w_branch, w_out, gain, bias)


def _ffn_kernel(x_ref, wup_ref, wdown_ref, gain_ref, bias_ref, o_ref, xb_scr, acc_scr):
    f = pl.program_id(1)

    @pl.when(f == 0)
    def _():
        xb_scr[...] = x_ref[...].astype(BF16)
        acc_scr[...] = jnp.zeros_like(acc_scr)

    h = jnp.maximum(_dot(xb_scr[...], wup_ref[...]), 0.0)
    acc_scr[...] += _dot((h * h).astype(BF16), wdown_ref[...])

    @pl.when(f == pl.num_programs(1) - 1)
    def _():
        o_ref[...] = _layer_norm(DEEPNORM_ALPHA * x_ref[...] + acc_scr[...], gain_ref[...], bias_ref[...])


def _ffn(x2d, w_up, w_down, layer, gain, bias, tm, tf):
    m = x2d.shape[0]
    gain, bias = gain.astype(F32)[None, :], bias.astype(F32)[None, :]
    vec = pl.BlockSpec((1, D_MODEL), lambda i, f: (0, 0))
    return pl.pallas_call(
        _ffn_kernel,
        grid=(m // tm, FFN_DIM // tf),
        in_specs=[pl.BlockSpec((tm, D_MODEL), lambda i, f: (i, 0)),
                  pl.BlockSpec((None, D_MODEL, tf), lambda i, f: (layer, 0, f)),
                  pl.BlockSpec((None, tf, D_MODEL), lambda i, f: (layer, f, 0)),
                  vec, vec],
        out_specs=pl.BlockSpec((tm, D_MODEL), lambda i, f: (i, 0)),
        out_shape=jax.ShapeDtypeStruct((m, D_MODEL), F32),
        scratch_shapes=[pltpu.VMEM((tm, D_MODEL), BF16), pltpu.VMEM((tm, D_MODEL), F32)],
        compiler_params=_params("parallel", "arbitrary"),
        name="ffn",
    )(x2d, w_up, w_down, gain, bias)


def _tile_sizes(seq_len):
    proj_tm = min(512, seq_len)
    ffn_tm = min(1024, seq_len)
    ffn_tf = 1024
    return proj_tm, ffn_tm, ffn_tf


def kernel(x, w_in, q_norm_gain, k_norm_gain, nat_rel_bias, ssm_a_re, ssm_a_im, ssm_log_dt, ssm_b_re, ssm_b_im, ssm_c_re, ssm_c_im, ssm_d, ssm_w_glu, w_branch, w_out, ln1_gain, ln1_bias, w_ffn_up, w_ffn_down, ln2_gain, ln2_bias):
    bsz, seq_len, _ = x.shape
    proj_tm, ffn_tm, ffn_tf = _tile_sizes(seq_len)
    rope = _rope_tables(seq_len)
    w_bf = w_in.astype(BF16)
    wt = _transposed_proj_weights(w_bf)
    toep, b_mat, m_mat, steps, carry = jax.vmap(_s5_matrices)(
        ssm_a_re, ssm_a_im, ssm_log_dt, ssm_b_re, ssm_b_im, ssm_c_re, ssm_c_im, ssm_d)
    per_batch = lambda t: jnp.tile(t[:, :, :, None, :], (1, 1, 1, bsz, 1)).reshape(t.shape[0], t.shape[1], -1)
    mats = (toep, b_mat, m_mat, per_batch(steps), per_batch(carry))
    nat_bias = jax.vmap(_nat_bias_table)(nat_rel_bias)
    w_glu, w_br, w_o = ssm_w_glu.astype(BF16), w_branch.astype(BF16), w_out.astype(BF16)
    w_up, w_down = w_ffn_up.astype(BF16), w_ffn_down.astype(BF16)
    h = x.reshape(bsz * seq_len, D_MODEL)
    for layer in range(w_in.shape[0]):
        (qt, vt, nkt, k, nq, nv, u2, gates), bound = _input_projections(
            h, w_bf, wt, layer, q_norm_gain[layer], k_norm_gain[layer], rope, bsz, seq_len, proj_tm)
        attn_o = _gqa_attention(qt, k, vt, bound, bsz, seq_len)
        nat_o = _neighbourhood_attention(nq, nkt, nv, nat_bias, layer, bsz, seq_len)
        y2 = _s5_bidirectional(u2, mats, layer, bsz, seq_len)
        h = _merge(h, attn_o, nat_o, y2, gates, w_glu, w_br, w_o, layer, ln1_gain[layer], ln1_bias[layer],
                   seq_len, proj_tm)
        h = _ffn(h, w_up, w_down, layer, ln2_gain[layer], ln2_bias[layer], ffn_tm, ffn_tf)
    return h.reshape(bsz, seq_len, D_MODEL)
```

```python
import functools

import jax
import jax.numpy as jnp
import numpy as np
from jax import lax
from jax.experimental import pallas as pl
from jax.experimental.pallas import tpu as pltpu

D_MODEL = 1024
DEPTH = 2
GRID_W = 64
HEAD_DIM = 64
MIX_WIDTH = 512
ATTN_Q_HEADS = 8
ATTN_KV_HEADS = 2
ATTN_GROUP = ATTN_Q_HEADS // ATTN_KV_HEADS
NAT_HEADS = 8
NAT_WIN_ROWS = 8
NAT_WIN_COLS = 16
SSM_GROUP = 16
SSM_GROUPS = 32
SSM_STATE = 64
SSM_CHUNK = 16
N_BRANCHES = 3
FFN_DIM = 4 * D_MODEL
ROPE_THETA = 10000.0
LN_EPS = 1e-5
RMS_EPS = 1e-6
DEEPNORM_ALPHA = (2 * DEPTH) ** 0.25
ATTN_SCALE = HEAD_DIM ** -0.5
LOG2_E = 1.4426950408889634
MASK_VALUE = -1e30

Q_WIDTH = ATTN_Q_HEADS * HEAD_DIM
KV_WIDTH = ATTN_KV_HEADS * HEAD_DIM
GATE_WIDTH = N_BRANCHES * D_MODEL

LANES = 128
SUBLANES = 8
MXU_WIDTH = 256
V7X_VMEM_BYTES = 64 * 1024 * 1024
VMEM_LIMIT = V7X_VMEM_BYTES - 8 * 1024 * 1024

F32 = jnp.float32
BF16 = jnp.bfloat16
NT_DIMS = (((1,), (1,)), ((), ()))


def _params(*semantics):
    return pltpu.CompilerParams(dimension_semantics=semantics, vmem_limit_bytes=VMEM_LIMIT)


def _dot(a, b):
    return jnp.dot(a, b, preferred_element_type=F32)


def _layer_norm(x, gain, bias):
    mu = jnp.mean(x, axis=-1, keepdims=True)
    xc = x - mu
    var = jnp.mean(xc * xc, axis=-1, keepdims=True)
    return xc * lax.rsqrt(var + LN_EPS) * gain + bias


def _resident(a):
    return pl.BlockSpec(a.shape, lambda *_: (0,) * a.ndim, pipeline_mode=pl.Buffered(1))


def _resident_layer(a, layer):
    return pl.BlockSpec((None,) + a.shape[1:], lambda *_: (layer,) + (0,) * (a.ndim - 1),
                        pipeline_mode=pl.Buffered(1))


W_IN_OFFSETS = (0, Q_WIDTH, Q_WIDTH + KV_WIDTH, Q_WIDTH + 2 * KV_WIDTH, Q_WIDTH + 2 * KV_WIDTH + MIX_WIDTH,
                Q_WIDTH + 2 * KV_WIDTH + 2 * MIX_WIDTH, Q_WIDTH + 2 * KV_WIDTH + 3 * MIX_WIDTH,
                Q_WIDTH + 2 * KV_WIDTH + 4 * MIX_WIDTH)


ATTN_KV_PAGE = 512
ATTN_TQ = MXU_WIDTH
ATTN_Q_BLOCKS_PER_STEP = 2
ATTN_EXT_DIM = LANES
ATTN_V_ROWS = HEAD_DIM + 16
ATTN_BOUND_LIMIT = 60.0
PROJ_T_ROWS = Q_WIDTH + KV_WIDTH + MIX_WIDTH
S5_TILE = SSM_CHUNK * SSM_GROUP
S5_STATE_COLS = 4 * SSM_STATE
S5_SCAN_ROWS = SUBLANES


GROUPS_PER_TILE = LANES // SSM_GROUP
CHUNKS_PER_TILE = LANES // SSM_GROUP


def _to_chunk_layout(x_scr, u_ref, n_chunks):
    lane_grp = lax.broadcasted_iota(jnp.int32, (n_chunks, LANES), 1) // SSM_GROUP
    for half in range(SSM_CHUNK // CHUNKS_PER_TILE):
        for q in range(x_scr.shape[0]):
            steps = [x_scr[q, pl.ds(CHUNKS_PER_TILE * half + tp, n_chunks, stride=SSM_CHUNK), :]
                     for tp in range(CHUNKS_PER_TILE)]
            for gm in range(GROUPS_PER_TILE):
                tile = None
                for tp in range(CHUNKS_PER_TILE):
                    shift = (SSM_GROUP * (tp - gm)) % LANES
                    moved = pltpu.roll(steps[tp], shift, 1) if shift else steps[tp]
                    tile = moved if tile is None else jnp.where(lane_grp == tp, moved, tile)
                u_ref[GROUPS_PER_TILE * q + gm, :, LANES * half:LANES * (half + 1)] = tile.astype(u_ref.dtype)


def _from_chunk_layout(y_ref, y_scr, n_chunks):
    lane_grp = lax.broadcasted_iota(jnp.int32, (n_chunks, LANES), 1) // SSM_GROUP
    for half in range(SSM_CHUNK // CHUNKS_PER_TILE):
        for q in range(y_scr.shape[0]):
            groups = [y_ref[GROUPS_PER_TILE * q + gm, :, LANES * half:LANES * (half + 1)].astype(F32)
                      for gm in range(GROUPS_PER_TILE)]
            for tp in range(CHUNKS_PER_TILE):
                tile = None
                for gm in range(GROUPS_PER_TILE):
                    shift = (SSM_GROUP * (gm - tp)) % LANES
                    moved = pltpu.roll(groups[gm], shift, 1) if shift else groups[gm]
                    tile = moved if tile is None else jnp.where(lane_grp == gm, moved, tile)
                y_scr[q, pl.ds(CHUNKS_PER_TILE * half + tp, n_chunks, stride=SSM_CHUNK), :] = tile


def _proj_kernel(x_ref, wt_ref, w_ref, qgain_ref, kgain_ref, cost_ref, sint_ref, cos_ref, sin_ref, seg_ref,
                 qext_ref, vext_ref, qt_ref, vt_ref, nkt_ref, k_ref, nq_ref, nv_ref, u_ref, g_ref, su_scr):
    tm = x_ref.shape[0]
    d = HEAD_DIM
    xb = x_ref[...].astype(BF16)

    yt = lax.dot_general(wt_ref[...], xb, NT_DIMS, preferred_element_type=F32)
    cost = cost_ref[...]
    sint = sint_ref[...]
    qgain = qgain_ref[...]
    for h in range(ATTN_Q_HEADS):
        blk = yt[d * h:d * (h + 1), :]
        ms = jnp.mean(blk * blk, axis=0, keepdims=True)
        yn = blk * lax.rsqrt(ms + RMS_EPS) * qgain
        partner = jnp.concatenate([yn[16:32], yn[0:16], yn[48:64], yn[32:48]], axis=0)
        qh = ((yn * cost + partner * sint) * (ATTN_SCALE * LOG2_E)).astype(BF16)
        j, g = divmod(h, ATTN_GROUP)
        for qb in range(tm // ATTN_TQ):
            qt_ref[j, qb, 0:d, ATTN_TQ * g:ATTN_TQ * (g + 1)] = qh[:, ATTN_TQ * qb:ATTN_TQ * (qb + 1)]
    for j in range(ATTN_KV_HEADS):
        for qb in range(tm // ATTN_TQ):
            qt_ref[j, qb, d:ATTN_EXT_DIM, :] = qext_ref[...]
    vt = yt[Q_WIDTH:Q_WIDTH + KV_WIDTH, :].astype(BF16)
    for pg in range(tm // ATTN_KV_PAGE):
        cols = slice(ATTN_KV_PAGE * pg, ATTN_KV_PAGE * (pg + 1))
        for j in range(ATTN_KV_HEADS):
            vt_ref[pg, ATTN_V_ROWS * j:ATTN_V_ROWS * j + d, :] = vt[d * j:d * (j + 1), cols]
            vt_ref[pg, ATTN_V_ROWS * j + d:ATTN_V_ROWS * (j + 1), :] = vext_ref[...]
    nkt_ref[...] = yt[Q_WIDTH + KV_WIDTH:, :].astype(BF16)

    o_ak, o_nq, o_nv, o_su, o_gate = (W_IN_OFFSETS[i] for i in (1, 3, 5, 6, 7))
    yk = _dot(xb, w_ref[:, o_ak:o_ak + KV_WIDTH])
    y2 = yk * yk
    hi = y2.astype(BF16)
    lo = (y2 - hi.astype(F32)).astype(BF16)
    ms = (_dot(hi, seg_ref[...]) + _dot(lo, seg_ref[...])) * (1.0 / d)
    kn = yk * lax.rsqrt(ms + RMS_EPS) * kgain_ref[...]
    lane = lax.broadcasted_iota(jnp.int32, (tm, LANES), 1)
    partner = jnp.where((lane % 32) < 16, pltpu.roll(kn, LANES - 16, 1), pltpu.roll(kn, 16, 1))
    kk = kn * cos_ref[...] + partner * sin_ref[...]
    one_hot = (lane == d).astype(F32)
    for j in range(ATTN_KV_HEADS):
        kj = kk if j == 0 else pltpu.roll(kk, LANES - d * j, 1)
        k_ref[j] = jnp.where(lane < d, kj, one_hot).astype(BF16)

    ynq = _dot(xb, w_ref[:, o_nq:o_nq + MIX_WIDTH]) * ATTN_SCALE
    ynv = _dot(xb, w_ref[:, o_nv:o_nv + MIX_WIDTH])
    for h in range(NAT_HEADS):
        nq_ref[h] = ynq[:, d * h:d * (h + 1)].astype(BF16)
        nv_ref[h] = ynv[:, d * h:d * (h + 1)].astype(BF16)
    su = _dot(xb, w_ref[:, o_su:o_su + MIX_WIDTH])
    for q in range(MIX_WIDTH // LANES):
        su_scr[q] = su[:, LANES * q:LANES * (q + 1)]
    _to_chunk_layout(su_scr, u_ref, tm // SSM_CHUNK)
    for n in range(N_BRANCHES):
        y = _dot(xb, w_ref[:, o_gate + D_MODEL * n:o_gate + D_MODEL * (n + 1)])
        g_ref[:, D_MODEL * n:D_MODEL * (n + 1)] = jax.nn.sigmoid(y).astype(BF16)


def _rope_tables(seq_len):
    f32 = np.float32
    t = np.arange(seq_len)
    row = (t // GRID_W).astype(f32)
    col = (t % GRID_W).astype(f32)
    axis_dim = HEAD_DIM // 2
    inv_freq = (f32(1.0) / (f32(ROPE_THETA) ** (np.arange(0, axis_dim, 2, dtype=f32) / f32(axis_dim)))).astype(f32)
    ang_r = row[:, None] * inv_freq[None, :]
    ang_c = col[:, None] * inv_freq[None, :]
    cos_head = np.concatenate([np.cos(ang_r), np.cos(ang_r), np.cos(ang_c), np.cos(ang_c)], axis=1)
    sin_head = np.concatenate([-np.sin(ang_r), np.sin(ang_r), -np.sin(ang_c), np.sin(ang_c)], axis=1)
    reps = LANES // HEAD_DIM
    return tuple(jnp.asarray(a, F32) for a in (np.tile(cos_head, (1, reps)), np.tile(sin_head, (1, reps)),
                                               np.ascontiguousarray(cos_head.T), np.ascontiguousarray(sin_head.T)))


def _transposed_proj_weights(w_bf):
    o = W_IN_OFFSETS
    parts = [w_bf[:, :, o[0]:o[1]], w_bf[:, :, o[2]:o[3]], w_bf[:, :, o[4]:o[5]]]
    return jnp.concatenate(parts, axis=2).transpose(0, 2, 1)


def _input_projections(x2d, w_bf, wt, layer, q_gain, k_gain, rope, bsz, seq_len, tm):
    m = x2d.shape[0]
    nl = seq_len // tm
    pages = tm // ATTN_KV_PAGE
    qgain = jnp.broadcast_to(q_gain.astype(F32)[:, None], (HEAD_DIM, tm))
    kgain = jnp.tile(k_gain.astype(F32), ATTN_KV_HEADS)[None, :]
    seg = jnp.asarray(np.arange(KV_WIDTH)[:, None] // HEAD_DIM == np.arange(KV_WIDTH)[None, :] // HEAD_DIM, BF16)
    bound = (HEAD_DIM * ATTN_SCALE * LOG2_E * 1.02) * jnp.max(jnp.abs(q_gain.astype(F32))) * jnp.max(jnp.abs(k_gain.astype(F32)))
    first_row = lambda rows, width: np.broadcast_to(np.arange(rows)[:, None] == 0, (rows, width))
    qext = jnp.where(first_row(ATTN_EXT_DIM - HEAD_DIM, ATTN_GROUP * ATTN_TQ), -bound, 0.0).astype(BF16)
    vext = jnp.asarray(first_row(ATTN_V_ROWS - HEAD_DIM, ATTN_KV_PAGE), BF16)
    cos, sin, cos_t, sin_t = rope
    tok = lambda width: pl.BlockSpec((tm, width), lambda i: (i, 0))
    heads = lambda n, width=HEAD_DIM: pl.BlockSpec((n, tm, width), lambda i: (0, i, 0))
    feat_t = lambda rows: pl.BlockSpec((None, rows, tm), lambda i: (i // nl, 0, i % nl))
    outs = pl.pallas_call(
        _proj_kernel,
        grid=(m // tm,),
        in_specs=[tok(D_MODEL), _resident_layer(wt, layer), _resident_layer(w_bf, layer), _resident(qgain), _resident(kgain),
                  pl.BlockSpec((HEAD_DIM, tm), lambda i: (0, i % nl)),
                  pl.BlockSpec((HEAD_DIM, tm), lambda i: (0, i % nl)),
                  pl.BlockSpec((tm, LANES), lambda i: (i % nl, 0)),
                  pl.BlockSpec((tm, LANES), lambda i: (i % nl, 0)),
                  _resident(seg), _resident(qext), _resident(vext)],
        out_specs=[pl.BlockSpec((None, ATTN_KV_HEADS, tm // ATTN_TQ, ATTN_EXT_DIM, ATTN_GROUP * ATTN_TQ),
                                lambda i: (i // nl, 0, i % nl, 0, 0)),
                   pl.BlockSpec((None, pages, ATTN_KV_HEADS * ATTN_V_ROWS, ATTN_KV_PAGE),
                                lambda i: (i // nl, i % nl, 0, 0)),
                   feat_t(MIX_WIDTH),
                   heads(ATTN_KV_HEADS, ATTN_EXT_DIM), heads(NAT_HEADS), heads(NAT_HEADS),
                   pl.BlockSpec((SSM_GROUPS, tm // SSM_CHUNK, S5_TILE), lambda i: (0, i % nl, i // nl)),
                   tok(GATE_WIDTH)],
        out_shape=[jax.ShapeDtypeStruct((bsz, ATTN_KV_HEADS, seq_len // ATTN_TQ, ATTN_EXT_DIM, ATTN_GROUP * ATTN_TQ), BF16),
                   jax.ShapeDtypeStruct((bsz, seq_len // ATTN_KV_PAGE, ATTN_KV_HEADS * ATTN_V_ROWS, ATTN_KV_PAGE), BF16),
                   jax.ShapeDtypeStruct((bsz, MIX_WIDTH, seq_len), BF16),
                   jax.ShapeDtypeStruct((ATTN_KV_HEADS, m, ATTN_EXT_DIM), BF16),
                   jax.ShapeDtypeStruct((NAT_HEADS, m, HEAD_DIM), BF16),
                   jax.ShapeDtypeStruct((NAT_HEADS, m, HEAD_DIM), BF16),
                   jax.ShapeDtypeStruct((SSM_GROUPS, seq_len // SSM_CHUNK, bsz * S5_TILE), BF16),
                   jax.ShapeDtypeStruct((m, GATE_WIDTH), BF16)],
        scratch_shapes=[pltpu.VMEM((MIX_WIDTH // LANES, tm, LANES), F32)],
        compiler_params=_params("parallel"),
        name="in_proj",
    )(x2d, wt, w_bf, qgain, kgain, cos_t, sin_t, cos, sin, seg, qext, vext)
    return outs, bound


def _finish_attention(o_ref, qb, o):
    o_ref[ATTN_TQ * qb:ATTN_TQ * (qb + 1), :] = jnp.concatenate(
        [o[:, ATTN_TQ * g:ATTN_TQ * (g + 1)].T for g in range(ATTN_GROUP)], axis=1).astype(o_ref.dtype)


def _flash_bounded_kernel(qg_ref, k_ref, vt_ref, o_ref, *, n_pages):
    d = HEAD_DIM
    for qb in range(qg_ref.shape[0]):
        qg = qg_ref[qb]
        acc = jnp.zeros((ATTN_V_ROWS, qg.shape[1]), F32)
        for c in range(n_pages):
            s = _dot(k_ref[ATTN_KV_PAGE * c:ATTN_KV_PAGE * (c + 1), :], qg)
            acc = acc + _dot(vt_ref[c], jnp.exp2(s).astype(BF16))
        _finish_attention(o_ref, qb, acc[:d] / acc[d:d + 1])


def _flash_online_kernel(qg_ref, k_ref, vt_ref, o_ref, acc_scr, *, n_pages):
    d = HEAD_DIM
    for qb in range(qg_ref.shape[0]):
        qg = qg_ref[qb]
        nq = qg.shape[1]
        acc_scr[...] = jnp.zeros_like(acc_scr)

        def scores(c):
            return _dot(k_ref[ATTN_KV_PAGE * c:ATTN_KV_PAGE * (c + 1), :], qg)

        m = jnp.full((1, nq), -jnp.inf, F32)
        l = jnp.zeros((1, nq), F32)
        s_next = scores(0)
        for c in range(n_pages):
            s = s_next
            if c + 1 < n_pages:
                s_next = scores(c + 1)
            m_new = jnp.maximum(m, jnp.max(s, axis=0, keepdims=True))
            alpha = jnp.exp2(m - m_new)
            p = jnp.exp2(s - m_new)
            l = alpha * l + jnp.sum(p, axis=0, keepdims=True)
            acc_scr[...] = alpha * acc_scr[...] + _dot(vt_ref[c, 0:d, :], p.astype(BF16))
            m = m_new
        _finish_attention(o_ref, qb, acc_scr[...] / l)


def _gqa_attention(qg, k, vt, bound, bsz, seq_len):
    d, hk, grp, tq = HEAD_DIM, ATTN_KV_HEADS, ATTN_GROUP, ATTN_TQ
    nqb = ATTN_Q_BLOCKS_PER_STEP
    nq, n_pages = seq_len // (tq * nqb), seq_len // ATTN_KV_PAGE

    def call(body, scratch):
        return pl.pallas_call(
            functools.partial(body, n_pages=n_pages),
            grid=(bsz, hk, nq),
            in_specs=[pl.BlockSpec((None, None, nqb, ATTN_EXT_DIM, grp * tq), lambda b, j, i: (b, j, i, 0, 0)),
                      pl.BlockSpec((None, seq_len, ATTN_EXT_DIM), lambda b, j, i: (j, b, 0)),
                      pl.BlockSpec((None, n_pages, ATTN_V_ROWS, ATTN_KV_PAGE), lambda b, j, i: (b, 0, j, 0))],
            out_specs=pl.BlockSpec((tq * nqb, grp * d), lambda b, j, i: (b * nq + i, j)),
            out_shape=jax.ShapeDtypeStruct((bsz * seq_len, ATTN_Q_HEADS * d), BF16),
            scratch_shapes=scratch,
            compiler_params=_params("parallel", "parallel", "parallel"),
            name=body.__name__.strip("_").replace("_kernel", ""),
        )(qg, k, vt)

    return lax.cond(bound <= ATTN_BOUND_LIMIT,
                    lambda: call(_flash_bounded_kernel, []),
                    lambda: call(_flash_online_kernel, [pltpu.VMEM((d, grp * tq), F32)]))


NAT_ROWS_PER_STEP = 8
NAT_HEAD_UNROLL = 8
NAT_STEP_TOKENS = NAT_ROWS_PER_STEP * GRID_W
NAT_WIN_TOKENS = NAT_WIN_ROWS * GRID_W


def _nat_kernel(q_ref, ktp_ref, ktc_ref, ktn_ref, vp_ref, vc_ref, vn_ref, bias_ref, o_ref, kt_scr, v_scr, oh_scr):
    g = pl.program_id(1)
    ng = pl.num_programs(1)
    st = NAT_STEP_TOKENS
    d = HEAD_DIM
    kt_scr[:, 0:st] = ktp_ref[...]
    kt_scr[:, st:2 * st] = ktc_ref[...]
    kt_scr[:, 2 * st:3 * st] = ktn_ref[...]
    v_scr[:, 0:st, :] = vp_ref[...]
    v_scr[:, st:2 * st, :] = vc_ref[...]
    v_scr[:, 2 * st:3 * st, :] = vn_ref[...]

    def all_heads(frame_rows, bias_index):
        starts = [fr * GRID_W for fr in frame_rows]

        def head(h, carry):
            q = q_ref[h]
            r0 = pl.multiple_of(h * d, d)
            s = jnp.concatenate(
                [_dot(q[GRID_W * r:GRID_W * (r + 1), :],
                      kt_scr[pl.ds(r0, d), starts[r]:starts[r] + NAT_WIN_TOKENS]) + bias_ref[bias_index[r], h]
                 for r in range(NAT_ROWS_PER_STEP)], axis=0)
            m = jnp.max(s, axis=1, keepdims=True)
            p = jnp.exp(s - m)
            l = jnp.sum(p, axis=1, keepdims=True)
            pb = p.astype(BF16)
            o = jnp.concatenate(
                [_dot(pb[GRID_W * r:GRID_W * (r + 1), :], v_scr[h, starts[r]:starts[r] + NAT_WIN_TOKENS, :])
                 for r in range(NAT_ROWS_PER_STEP)], axis=0)
            oh_scr[h] = o / l
            return carry

        lax.fori_loop(0, NAT_HEADS, head, 0, unroll=NAT_HEAD_UNROLL)

    half = NAT_WIN_ROWS // 2
    n = NAT_ROWS_PER_STEP

    @pl.when(g == 0)
    def _():
        all_heads([n + max(r - half, 0) for r in range(n)], [min(r, half) for r in range(n)])

    @pl.when(jnp.logical_and(g > 0, g < ng - 1))
    def _():
        all_heads([r + half for r in range(n)], [half] * n)

    @pl.when(jnp.logical_and(g == ng - 1, g > 0))
    def _():
        all_heads([min(r + half, n) for r in range(n)], [max(r, half) for r in range(n)])

    o_ref[...] = jnp.concatenate([oh_scr[h] for h in range(NAT_HEADS)], axis=1).astype(o_ref.dtype)


def _nat_bias_table(rel_bias):
    w, nb = GRID_W, 2 * NAT_WIN_COLS - 1
    cols = np.arange(w)
    col_start = np.clip(cols - NAT_WIN_COLS // 2, 0, w - NAT_WIN_COLS)
    in_win = (cols[None, :] >= col_start[:, None]) & (cols[None, :] < col_start[:, None] + NAT_WIN_COLS)
    v = np.arange(NAT_WIN_ROWS)
    row_off = v[None, :] - v[:, None] + (NAT_WIN_ROWS - 1)
    row_hot = (row_off[:, :, None] == np.arange(2 * NAT_WIN_ROWS - 1)[None, None, :]).astype(np.float32)
    by_row = jnp.sum(rel_bias.astype(F32)[:, None, None] * row_hot[None, :, :, :, None], axis=3)
    period = nb + w + 1
    padded = jnp.pad(by_row, ((0, 0), (0, 0), (0, 0), (0, period - nb)))
    skewed = jnp.tile(padded, (1, 1, 1, w))[..., :w * (period - 1)].reshape(by_row.shape[:3] + (w, period - 1))
    b = skewed[..., NAT_WIN_COLS - 1:NAT_WIN_COLS - 1 + w]
    b = jnp.where(in_win[None, None, None], b, MASK_VALUE).transpose(1, 0, 3, 2, 4)
    return b.reshape(NAT_WIN_ROWS, NAT_HEADS, w, NAT_WIN_TOKENS)


def _neighbourhood_attention(nq, nkt, nv, bias, layer, bsz, seq_len):
    h, d, st = NAT_HEADS, HEAD_DIM, NAT_STEP_TOKENS
    ng = seq_len // st
    assert ng >= 2 and seq_len // GRID_W >= NAT_WIN_ROWS
    prev = lambda g: jnp.maximum(g - 1, 0)
    nxt = lambda g: jnp.minimum(g + 1, ng - 1)
    same = lambda g: g
    kt_spec = lambda f: pl.BlockSpec((None, h * d, st), lambda b, g: (b, 0, f(g)))
    v_spec = lambda f: pl.BlockSpec((h, st, d), lambda b, g: (0, b * ng + f(g), 0))
    return pl.pallas_call(
        _nat_kernel,
        grid=(bsz, ng),
        in_specs=[v_spec(same), kt_spec(prev), kt_spec(same), kt_spec(nxt),
                  v_spec(prev), v_spec(same), v_spec(nxt), _resident_layer(bias, layer)],
        out_specs=pl.BlockSpec((st, h * d), lambda b, g: (b * ng + g, 0)),
        out_shape=jax.ShapeDtypeStruct((bsz * seq_len, h * d), BF16),
        scratch_shapes=[pltpu.VMEM((h * d, 3 * st), BF16), pltpu.VMEM((h, 3 * st, d), BF16),
                        pltpu.VMEM((h, st, d), F32)],
        compiler_params=_params("parallel", "parallel"),
        name="nat",
    )(nq, nkt, nkt, nkt, nv, nv, nv, bias)


def _s5_matrices(a_re, a_im, log_dt, b_re, b_im, c_re, c_im, d_skip):
    t_len, hs = SSM_CHUNK, SSM_GROUP
    a_re = a_re.astype(F32)
    a_im = a_im.astype(F32)
    dt = jnp.exp(log_dt.astype(F32))[..., None]
    decay = jnp.exp(a_re * dt)
    phase = a_im * dt
    lam_re = decay * jnp.cos(phase)
    lam_im = decay * jnp.sin(phase)
    den = a_re * a_re + a_im * a_im
    num_re = lam_re - 1.0
    coef_re = (num_re * a_re + lam_im * a_im) / den
    coef_im = (lam_im * a_re - num_re * a_im) / den
    b_re = b_re.astype(F32)[None]
    b_im = b_im.astype(F32)[None]
    bbar_re = coef_re[..., None] * b_re - coef_im[..., None] * b_im
    bbar_im = coef_re[..., None] * b_im + coef_im[..., None] * b_re
    c_re = c_re.astype(F32)
    c_im = c_im.astype(F32)

    def powers(exponents):
        e = jnp.asarray(exponents, F32)[:, None, None, None]
        mag = jnp.exp(e * (a_re * dt)[None])
        return mag * jnp.cos(e * phase[None]), mag * jnp.sin(e * phase[None])

    def c_times(p_re, p_im):
        return (c_re[None] * p_re[:, :, :, None, :] - c_im[None] * p_im[:, :, :, None, :],
                c_re[None] * p_im[:, :, :, None, :] + c_im[None] * p_re[:, :, :, None, :])

    def times_bbar(p_re, p_im):
        return (p_re[..., None] * bbar_re[None] - p_im[..., None] * bbar_im[None],
                p_re[..., None] * bbar_im[None] + p_im[..., None] * bbar_re[None])

    tk = np.arange(t_len)
    cl_re, cl_im = c_times(*powers(tk))
    taps = jnp.sum(cl_re[..., None] * bbar_re[None, :, :, None] - cl_im[..., None] * bbar_im[None, :, :, None], axis=4)
    lag = tk[None, :] - tk[:, None]
    hot_f = (lag[:, :, None] == tk[None, None, :]).astype(np.float32)
    hot_r = (-lag[:, :, None] == tk[None, None, :]).astype(np.float32)
    eye_t = np.eye(t_len, dtype=np.float32)[:, :, None, None, None]
    skip = eye_t * (np.eye(hs, dtype=np.float32)[None] * d_skip.astype(F32)[:, :, None])[None, None]
    hp = lax.Precision.HIGHEST
    toep = (jnp.einsum('ktx,xgoi->ktgoi', hot_f, taps[:, 0], precision=hp)
            + jnp.einsum('ktx,xgoi->ktgoi', hot_r, taps[:, 1], precision=hp) + skip)
    toep = toep.transpose(2, 0, 4, 1, 3).reshape(SSM_GROUPS, S5_TILE, S5_TILE)

    to_in = lambda m: m.transpose(1, 0, 3, 2).reshape(SSM_GROUPS, S5_TILE, SSM_STATE)
    f_re, f_im = times_bbar(*powers(t_len - 1 - tk))
    r_re, r_im = times_bbar(*powers(tk))
    b_mat = jnp.concatenate([to_in(f_re[:, 0]), to_in(r_re[:, 1]), to_in(f_im[:, 0]), to_in(r_im[:, 1])], axis=2)

    to_out = lambda m: m.transpose(1, 3, 0, 2).reshape(SSM_GROUPS, SSM_STATE, S5_TILE)
    of_re, of_im = c_times(*powers(tk + 1))
    or_re, or_im = c_times(*powers(t_len - tk))
    m_mat = jnp.concatenate([to_out(of_re[:, 0]), to_out(or_re[:, 1]), to_out(-of_im[:, 0]), to_out(-or_im[:, 1])],
                            axis=1)

    def chunk_powers(n_fwd, n_rev):
        (fr, fi), (rr, ri) = powers(n_fwd * t_len), powers(n_rev * t_len)
        return jnp.concatenate([fr[:, 0], rr[:, 1], fi[:, 0], ri[:, 1]], axis=-1)
    doubling = np.array([1, 2, 4])
    steps = chunk_powers(doubling, doubling)
    rows = np.arange(S5_SCAN_ROWS)
    carry = chunk_powers(rows + 1, S5_SCAN_ROWS - rows)
    return toep.astype(BF16), b_mat.astype(BF16), m_mat.astype(BF16), steps, carry


def _s5_kernel(u_ref, b_ref, t_ref, m_ref, step_ref, carry_ref, y_ref, z_scr, s_scr, *, n_tiles, bsz):
    for b in range(bsz):
        cols = slice(S5_TILE * b, S5_TILE * (b + 1))
        z_scr[:, cols] = _dot(u_ref[:, cols], b_ref[...])
    _s5_chunk_scan(z_scr, step_ref, carry_ref, s_scr, n_tiles=n_tiles, pairs=bsz)
    for b in range(bsz):
        cols = slice(S5_TILE * b, S5_TILE * (b + 1))
        y = _dot(u_ref[:, cols], t_ref[...]) + _dot(s_scr[:, cols].astype(BF16), m_ref[...])
        y_ref[:, cols] = y.astype(y_ref.dtype)


def _s5_chunk_scan(z_ref, step_ref, carry_ref, s_ref, *, n_tiles, pairs):
    rows = S5_SCAN_ROWS
    lane = lax.broadcasted_iota(jnp.int32, (rows, LANES), 1)
    sub = lax.broadcasted_iota(jnp.int32, (rows, LANES), 0)
    is_fwd = lane < SSM_STATE
    is_rev = jnp.logical_not(is_fwd)
    both = lambda fwd_rows, rev_rows: jnp.logical_or(jnp.logical_and(is_fwd, fwd_rows),
                                                     jnp.logical_and(is_rev, rev_rows))
    edge = both(sub == 0, sub == rows - 1)

    def upstream(x, dist):
        valid = both(sub >= dist, sub < rows - dist)
        return jnp.where(valid, jnp.where(is_fwd, pltpu.roll(x, dist, 0), pltpu.roll(x, rows - dist, 0)), 0.0)

    def step(k, carry):
        rf = pl.multiple_of(k * rows, rows)
        rr = pl.multiple_of((n_tiles - 1 - k) * rows, rows)
        new = []
        for j in range(pairs):
            cre = slice(S5_STATE_COLS * j, S5_STATE_COLS * j + LANES)
            cim = slice(S5_STATE_COLS * j + LANES, S5_STATE_COLS * (j + 1))
            c_re, c_im = carry[j]
            x_re = jnp.where(is_fwd, z_ref[pl.ds(rf, rows), cre], z_ref[pl.ds(rr, rows), cre])
            x_im = jnp.where(is_fwd, z_ref[pl.ds(rf, rows), cim], z_ref[pl.ds(rr, rows), cim])
            for i, dist in enumerate((1, 2, 4)):
                lr = step_ref[i:i + 1, cre]
                li = step_ref[i:i + 1, cim]
                u_re, u_im = upstream(x_re, dist), upstream(x_im, dist)
                x_re, x_im = x_re + lr * u_re - li * u_im, x_im + lr * u_im + li * u_re
            pr, pi = carry_ref[:, cre], carry_ref[:, cim]
            a_re = x_re + pr * c_re - pi * c_im
            a_im = x_im + pr * c_im + pi * c_re
            e_re = jnp.where(edge, c_re, upstream(a_re, 1))
            e_im = jnp.where(edge, c_im, upstream(a_im, 1))
            pltpu.store(s_ref.at[pl.ds(rf, rows), cre], e_re, mask=is_fwd)
            pltpu.store(s_ref.at[pl.ds(rf, rows), cim], e_im, mask=is_fwd)
            pltpu.store(s_ref.at[pl.ds(rr, rows), cre], e_re, mask=is_rev)
            pltpu.store(s_ref.at[pl.ds(rr, rows), cim], e_im, mask=is_rev)
            last = lambda a: jnp.where(is_fwd, jnp.broadcast_to(a[rows - 1:rows], a.shape),
                                       jnp.broadcast_to(a[0:1], a.shape))
            new.append((last(a_re), last(a_im)))
        return tuple(new)

    zero = jnp.zeros((rows, LANES), F32)
    lax.fori_loop(0, n_tiles, step, tuple((zero, zero) for _ in range(pairs)))


def _s5_bidirectional(u2, mats, layer, bsz, seq_len):
    toep, b_mat, m_mat, steps, carry = mats
    g = SSM_GROUPS
    n_chunks = seq_len // SSM_CHUNK
    assert n_chunks % S5_SCAN_ROWS == 0
    width = bsz * S5_TILE
    u_spec = pl.BlockSpec((None, n_chunks, width), lambda j: (j, 0, 0))
    w_spec = pl.BlockSpec((None, None, S5_TILE, S5_TILE), lambda j: (layer, j, 0, 0))
    return pl.pallas_call(
        functools.partial(_s5_kernel, n_tiles=n_chunks // S5_SCAN_ROWS, bsz=bsz),
        grid=(g,),
        in_specs=[u_spec, w_spec, w_spec, w_spec,
                  pl.BlockSpec((None, steps.shape[1], width), lambda j: (layer, 0, j)),
                  pl.BlockSpec((None, S5_SCAN_ROWS, width), lambda j: (layer, 0, j))],
        out_specs=u_spec,
        out_shape=jax.ShapeDtypeStruct((g, n_chunks, width), BF16),
        scratch_shapes=[pltpu.VMEM((n_chunks, width), F32), pltpu.VMEM((n_chunks, width), F32)],
        compiler_params=_params("parallel"),
        name="s5",
    )(u2, b_mat, toep, m_mat, steps, carry)


MERGE_ROW_PARTS = 2


def _merge_kernel(x_ref, attn_ref, nat_ref, y_ref, gate_ref, wglu_ref, wb_ref, wout_ref, gain_ref, bias_ref, o_ref,
                  y_scr):
    tm = x_ref.shape[0]
    _from_chunk_layout(y_ref, y_scr, tm // SSM_CHUNK)
    d = D_MODEL
    part = tm // MERGE_ROW_PARTS
    for r in range(MERGE_ROW_PARTS):
        rows = slice(part * r, part * (r + 1))
        z = jax.nn.gelu(jnp.concatenate([y_scr[q, rows, :] for q in range(y_scr.shape[0])], axis=1))
        ssm = z * jax.nn.sigmoid(_dot(z.astype(BF16), wglu_ref[...]))
        merged = gate_ref[rows, 0:d].astype(F32) * _dot(attn_ref[rows, :], wb_ref[0])
        merged += gate_ref[rows, d:2 * d].astype(F32) * _dot(nat_ref[rows, :], wb_ref[1])
        merged += gate_ref[rows, 2 * d:3 * d].astype(F32) * _dot(ssm.astype(BF16), wb_ref[2])
        mix = _dot(merged.astype(BF16), wout_ref[...])
        o_ref[rows, :] = _layer_norm(DEEPNORM_ALPHA * x_ref[rows, :] + mix, gain_ref[...], bias_ref[...])


def _merge(x2d, attn_o, nat_o, y_ssm, gates, w_glu, w_branch, w_out, layer, gain, bias, seq_len, tm):
    m = x2d.shape[0]
    nl = seq_len // tm
    row = lambda width: pl.BlockSpec((tm, width), lambda i: (i, 0))
    gain, bias = gain.astype(F32)[None, :], bias.astype(F32)[None, :]
    return pl.pallas_call(
        _merge_kernel,
        grid=(m // tm,),
        in_specs=[row(D_MODEL), row(MIX_WIDTH), row(MIX_WIDTH),
                  pl.BlockSpec((SSM_GROUPS, tm // SSM_CHUNK, S5_TILE), lambda i: (0, i % nl, i // nl)),
                  row(GATE_WIDTH), _resident_layer(w_glu, layer), _resident_layer(w_branch, layer),
                  _resident_layer(w_out, layer), _resident(gain), _resident(bias)],
        out_specs=row(D_MODEL),
        out_shape=jax.ShapeDtypeStruct((m, D_MODEL), F32),
        scratch_shapes=[pltpu.VMEM((MIX_WIDTH // LANES, tm, LANES), F32)],
        compiler_params=_params("parallel"),
        name="merge",
    )(x2d, attn_o, nat_o, y_ssm, gates, w_glu, w_branch, w_out, gain, bias)


def _ffn_kernel(x_ref, wup_ref, wdown_ref, gain_ref, bias_ref, o_ref, xb_scr, acc_scr):
    f = pl.program_id(1)

    @pl.when(f == 0)
    def _():
        xb_scr[...] = x_ref[...].astype(BF16)
        acc_scr[...] = jnp.zeros_like(acc_scr)

    h = jnp.maximum(_dot(xb_scr[...], wup_ref[...]), 0.0)
    acc_scr[...] += _dot((h * h).astype(BF16), wdown_ref[...])

    @pl.when(f == pl.num_programs(1) - 1)
    def _():
        o_ref[...] = _layer_norm(DEEPNORM_ALPHA * x_ref[...] + acc_scr[...], gain_ref[...], bias_ref[...])


def _ffn(x2d, w_up, w_down, layer, gain, bias, tm, tf):
    m = x2d.shape[0]
    gain, bias = gain.astype(F32)[None, :], bias.astype(F32)[None, :]
    vec = pl.BlockSpec((1, D_MODEL), lambda i, f: (0, 0))
    return pl.pallas_call(
        _ffn_kernel,
        grid=(m // tm, FFN_DIM // tf),
        in_specs=[pl.BlockSpec((tm, D_MODEL), lambda i, f: (i, 0)),
                  pl.BlockSpec((None, D_MODEL, tf), lambda i, f: (layer, 0, f)),
                  pl.BlockSpec((None, tf, D_MODEL), lambda i, f: (layer, f, 0)),
                  vec, vec],
        out_specs=pl.BlockSpec((tm, D_MODEL), lambda i, f: (i, 0)),
        out_shape=jax.ShapeDtypeStruct((m, D_MODEL), F32),
        scratch_shapes=[pltpu.VMEM((tm, D_MODEL), BF16), pltpu.VMEM((tm, D_MODEL), F32)],
        compiler_params=_params("parallel", "arbitrary"),
        name="ffn",
    )(x2d, w_up, w_down, gain, bias)


def _tile_sizes(seq_len):
    proj_tm = min(512, seq_len)
    ffn_tm = min(1024, seq_len)
    ffn_tf = 1024
    return proj_tm, ffn_tm, ffn_tf


def kernel(x, w_in, q_norm_gain, k_norm_gain, nat_rel_bias, ssm_a_re, ssm_a_im, ssm_log_dt, ssm_b_re, ssm_b_im, ssm_c_re, ssm_c_im, ssm_d, ssm_w_glu, w_branch, w_out, ln1_gain, ln1_bias, w_ffn_up, w_ffn_down, ln2_gain, ln2_bias):
    bsz, seq_len, _ = x.shape
    proj_tm, ffn_tm, ffn_tf = _tile_sizes(seq_len)
    rope = _rope_tables(seq_len)
    w_bf = w_in.astype(BF16)
    wt = _transposed_proj_weights(w_bf)
    toep, b_mat, m_mat, steps, carry = jax.vmap(_s5_matrices)(
        ssm_a_re, ssm_a_im, ssm_log_dt, ssm_b_re, ssm_b_im, ssm_c_re, ssm_c_im, ssm_d)
    per_batch = lambda t: jnp.tile(t[:, :, :, None, :], (1, 1, 1, bsz, 1)).reshape(t.shape[0], t.shape[1], -1)
    mats = (toep, b_mat, m_mat, per_batch(steps), per_batch(carry))
    nat_bias = jax.vmap(_nat_bias_table)(nat_rel_bias)
    w_glu, w_br, w_o = ssm_w_glu.astype(BF16), w_branch.astype(BF16), w_out.astype(BF16)
    w_up, w_down = w_ffn_up.astype(BF16), w_ffn_down.astype(BF16)
    h = x.reshape(bsz * seq_len, D_MODEL)
    for layer in range(w_in.shape[0]):
        (qt, vt, nkt, k, nq, nv, u2, gates), bound = _input_projections(
            h, w_bf, wt, layer, q_norm_gain[layer], k_norm_gain[layer], rope, bsz, seq_len, proj_tm)
        attn_o = _gqa_attention(qt, k, vt, bound, bsz, seq_len)
        nat_o = _neighbourhood_attention(nq, nkt, nv, nat_bias, layer, bsz, seq_len)
        y2 = _s5_bidirectional(u2, mats, layer, bsz, seq_len)
        h = _merge(h, attn_o, nat_o, y2, gates, w_glu, w_br, w_o, layer, ln1_gain[layer], ln1_bias[layer],
                   seq_len, proj_tm)
        h = _ffn(h, w_up, w_down, layer, ln2_gain[layer], ln2_bias[layer], ffn_tm, ffn_tf)
    return h.reshape(bsz, seq_len, D_MODEL)
```

```python
import functools

import jax
import jax.numpy as jnp
import numpy as np
from jax import lax
from jax.experimental import pallas as pl
from jax.experimental.pallas import tpu as pltpu

D_MODEL = 1024
DEPTH = 2
GRID_W = 64
HEAD_DIM = 64
MIX_WIDTH = 512
ATTN_Q_HEADS = 8
ATTN_KV_HEADS = 2
ATTN_GROUP = ATTN_Q_HEADS // ATTN_KV_HEADS
NAT_HEADS = 8
NAT_WIN_ROWS = 8
NAT_WIN_COLS = 16
SSM_GROUP = 16
SSM_GROUPS = 32
SSM_STATE = 64
SSM_CHUNK = 16
N_BRANCHES = 3
FFN_DIM = 4 * D_MODEL
ROPE_THETA = 10000.0
LN_EPS = 1e-5
RMS_EPS = 1e-6
DEEPNORM_ALPHA = (2 * DEPTH) ** 0.25
ATTN_SCALE = HEAD_DIM ** -0.5
LOG2_E = 1.4426950408889634
MASK_VALUE = -1e30

Q_WIDTH = ATTN_Q_HEADS * HEAD_DIM
KV_WIDTH = ATTN_KV_HEADS * HEAD_DIM
GATE_WIDTH = N_BRANCHES * D_MODEL

LANES = 128
SUBLANES = 8
MXU_WIDTH = 256
V7X_VMEM_BYTES = 64 * 1024 * 1024
VMEM_LIMIT = V7X_VMEM_BYTES - 8 * 1024 * 1024

F32 = jnp.float32
BF16 = jnp.bfloat16
NT_DIMS = (((1,), (1,)), ((), ()))


def _params(*semantics):
    return pltpu.CompilerParams(dimension_semantics=semantics, vmem_limit_bytes=VMEM_LIMIT)


def _dot(a, b):
    return jnp.dot(a, b, preferred_element_type=F32)


def _layer_norm(x, gain, bias):
    mu = jnp.mean(x, axis=-1, keepdims=True)
    xc = x - mu
    var = jnp.mean(xc * xc, axis=-1, keepdims=True)
    return xc * lax.rsqrt(var + LN_EPS) * gain + bias


def _resident(a):
    return pl.BlockSpec(a.shape, lambda *_: (0,) * a.ndim, pipeline_mode=pl.Buffered(1))


def _resident_layer(a, layer):
    return pl.BlockSpec((None,) + a.shape[1:], lambda *_: (layer,) + (0,) * (a.ndim - 1),
                        pipeline_mode=pl.Buffered(1))


W_IN_OFFSETS = (0, Q_WIDTH, Q_WIDTH + KV_WIDTH, Q_WIDTH + 2 * KV_WIDTH, Q_WIDTH + 2 * KV_WIDTH + MIX_WIDTH,
                Q_WIDTH + 2 * KV_WIDTH + 2 * MIX_WIDTH, Q_WIDTH + 2 * KV_WIDTH + 3 * MIX_WIDTH,
                Q_WIDTH + 2 * KV_WIDTH + 4 * MIX_WIDTH)


ATTN_KV_PAGE = 512
ATTN_TQ = MXU_WIDTH
ATTN_Q_BLOCKS_PER_STEP = 2
ATTN_EXT_DIM = LANES
ATTN_V_ROWS = HEAD_DIM + 16
ATTN_BOUND_LIMIT = 60.0
PROJ_T_ROWS = Q_WIDTH + KV_WIDTH + MIX_WIDTH
S5_TILE = SSM_CHUNK * SSM_GROUP
S5_STATE_COLS = 4 * SSM_STATE
S5_SCAN_ROWS = SUBLANES


GROUPS_PER_TILE = LANES // SSM_GROUP
CHUNKS_PER_TILE = LANES // SSM_GROUP


def _to_chunk_layout(x_scr, u_ref, n_chunks):
    lane_grp = lax.broadcasted_iota(jnp.int32, (n_chunks, LANES), 1) // SSM_GROUP
    for half in range(SSM_CHUNK // CHUNKS_PER_TILE):
        for q in range(x_scr.shape[0]):
            steps = [x_scr[q, pl.ds(CHUNKS_PER_TILE * half + tp, n_chunks, stride=SSM_CHUNK), :]
                     for tp in range(CHUNKS_PER_TILE)]
            for gm in range(GROUPS_PER_TILE):
                tile = None
                for tp in range(CHUNKS_PER_TILE):
                    shift = (SSM_GROUP * (tp - gm)) % LANES
                    moved = pltpu.roll(steps[tp], shift, 1) if shift else steps[tp]
                    tile = moved if tile is None else jnp.where(lane_grp == tp, moved, tile)
                u_ref[GROUPS_PER_TILE * q + gm, :, LANES * half:LANES * (half + 1)] = tile.astype(u_ref.dtype)


def _from_chunk_layout(y_ref, y_scr, n_chunks):
    lane_grp = lax.broadcasted_iota(jnp.int32, (n_chunks, LANES), 1) // SSM_GROUP
    for half in range(SSM_CHUNK // CHUNKS_PER_TILE):
        for q in range(y_scr.shape[0]):
            groups = [y_ref[GROUPS_PER_TILE * q + gm, :, LANES * half:LANES * (half + 1)].astype(F32)
                      for gm in range(GROUPS_PER_TILE)]
            for tp in range(CHUNKS_PER_TILE):
                tile = None
                for gm in range(GROUPS_PER_TILE):
                    shift = (SSM_GROUP * (gm - tp)) % LANES
                    moved = pltpu.roll(groups[gm], shift, 1) if shift else groups[gm]
                    tile = moved if tile is None else jnp.where(lane_grp == gm, moved, tile)
                y_scr[q, pl.ds(CHUNKS_PER_TILE * half + tp, n_chunks, stride=SSM_CHUNK), :] = tile


def _proj_kernel(x_ref, wt_ref, w_ref, qgain_ref, kgain_ref, cost_ref, sint_ref, cos_ref, sin_ref, seg_ref,
                 qext_ref, vext_ref, qt_ref, vt_ref, nkt_ref, k_ref, nq_ref, nv_ref, u_ref, g_ref, su_scr):
    tm = x_ref.shape[0]
    d = HEAD_DIM
    xb = x_ref[...].astype(BF16)

    yt = lax.dot_general(wt_ref[...], xb, NT_DIMS, preferred_element_type=F32)
    cost = cost_ref[...]
    sint = sint_ref[...]
    qgain = qgain_ref[...]
    for h in range(ATTN_Q_HEADS):
        blk = yt[d * h:d * (h + 1), :]
        ms = jnp.mean(blk * blk, axis=0, keepdims=True)
        yn = blk * lax.rsqrt(ms + RMS_EPS) * qgain
        partner = jnp.concatenate([yn[16:32], yn[0:16], yn[48:64], yn[32:48]], axis=0)
        qh = ((yn * cost + partner * sint) * (ATTN_SCALE * LOG2_E)).astype(BF16)
        j, g = divmod(h, ATTN_GROUP)
        for qb in range(tm // ATTN_TQ):
            qt_ref[j, qb, 0:d, ATTN_TQ * g:ATTN_TQ * (g + 1)] = qh[:, ATTN_TQ * qb:ATTN_TQ * (qb + 1)]
    for j in range(ATTN_KV_HEADS):
        for qb in range(tm // ATTN_TQ):
            qt_ref[j, qb, d:ATTN_EXT_DIM, :] = qext_ref[...]
    vt = yt[Q_WIDTH:Q_WIDTH + KV_WIDTH, :].astype(BF16)
    for pg in range(tm // ATTN_KV_PAGE):
        cols = slice(ATTN_KV_PAGE * pg, ATTN_KV_PAGE * (pg + 1))
        for j in range(ATTN_KV_HEADS):
            vt_ref[pg, ATTN_V_ROWS * j:ATTN_V_ROWS * j + d, :] = vt[d * j:d * (j + 1), cols]
            vt_ref[pg, ATTN_V_ROWS * j + d:ATTN_V_ROWS * (j + 1), :] = vext_ref[...]
    nkt_ref[...] = yt[Q_WIDTH + KV_WIDTH:, :].astype(BF16)

    o_ak, o_nq, o_nv, o_su, o_gate = (W_IN_OFFSETS[i] for i in (1, 3, 5, 6, 7))
    yk = _dot(xb, w_ref[:, o_ak:o_ak + KV_WIDTH])
    y2 = yk * yk
    hi = y2.astype(BF16)
    lo = (y2 - hi.astype(F32)).astype(BF16)
    ms = (_dot(hi, seg_ref[...]) + _dot(lo, seg_ref[...])) * (1.0 / d)
    kn = yk * lax.rsqrt(ms + RMS_EPS) * kgain_ref[...]
    lane = lax.broadcasted_iota(jnp.int32, (tm, LANES), 1)
    partner = jnp.where((lane % 32) < 16, pltpu.roll(kn, LANES - 16, 1), pltpu.roll(kn, 16, 1))
    kk = kn * cos_ref[...] + partner * sin_ref[...]
    one_hot = (lane == d).astype(F32)
    for j in range(ATTN_KV_HEADS):
        kj = kk if j == 0 else pltpu.roll(kk, LANES - d * j, 1)
        k_ref[j] = jnp.where(lane < d, kj, one_hot).astype(BF16)

    ynq = _dot(xb, w_ref[:, o_nq:o_nq + MIX_WIDTH]) * ATTN_SCALE
    ynv = _dot(xb, w_ref[:, o_nv:o_nv + MIX_WIDTH])
    for h in range(NAT_HEADS):
        nq_ref[h] = ynq[:, d * h:d * (h + 1)].astype(BF16)
        nv_ref[h] = ynv[:, d * h:d * (h + 1)].astype(BF16)
    su = _dot(xb, w_ref[:, o_su:o_su + MIX_WIDTH])
    for q in range(MIX_WIDTH // LANES):
        su_scr[q] = su[:, LANES * q:LANES * (q + 1)]
    _to_chunk_layout(su_scr, u_ref, tm // SSM_CHUNK)
    for n in range(N_BRANCHES):
        y = _dot(xb, w_ref[:, o_gate + D_MODEL * n:o_gate + D_MODEL * (n + 1)])
        g_ref[:, D_MODEL * n:D_MODEL * (n + 1)] = jax.nn.sigmoid(y).astype(BF16)


def _rope_tables(seq_len):
    t = jnp.arange(seq_len)
    row = (t // GRID_W).astype(F32)
    col = (t % GRID_W).astype(F32)
    axis_dim = HEAD_DIM // 2
    inv_freq = 1.0 / (ROPE_THETA ** (jnp.arange(0, axis_dim, 2, dtype=F32) / axis_dim))
    ang_r = row[:, None] * inv_freq[None, :]
    ang_c = col[:, None] * inv_freq[None, :]
    cos_head = jnp.concatenate([jnp.cos(ang_r), jnp.cos(ang_r), jnp.cos(ang_c), jnp.cos(ang_c)], axis=1)
    sin_head = jnp.concatenate([-jnp.sin(ang_r), jnp.sin(ang_r), -jnp.sin(ang_c), jnp.sin(ang_c)], axis=1)
    reps = LANES // HEAD_DIM
    return (jnp.tile(cos_head, (1, reps)), jnp.tile(sin_head, (1, reps)), cos_head.T, sin_head.T)


def _transposed_proj_weights(w_bf):
    o = W_IN_OFFSETS
    parts = [w_bf[:, :, o[0]:o[1]], w_bf[:, :, o[2]:o[3]], w_bf[:, :, o[4]:o[5]]]
    return jnp.concatenate(parts, axis=2).transpose(0, 2, 1)


def _input_projections(x2d, w_bf, wt, layer, q_gain, k_gain, rope, bsz, seq_len, tm):
    m = x2d.shape[0]
    nl = seq_len // tm
    pages = tm // ATTN_KV_PAGE
    qgain = jnp.broadcast_to(q_gain.astype(F32)[:, None], (HEAD_DIM, tm))
    kgain = jnp.tile(k_gain.astype(F32), ATTN_KV_HEADS)[None, :]
    seg = jnp.asarray(np.arange(KV_WIDTH)[:, None] // HEAD_DIM == np.arange(KV_WIDTH)[None, :] // HEAD_DIM, BF16)
    bound = (HEAD_DIM * ATTN_SCALE * LOG2_E * 1.02) * jnp.max(jnp.abs(q_gain.astype(F32))) * jnp.max(jnp.abs(k_gain.astype(F32)))
    first_row = lambda rows, width: np.broadcast_to(np.arange(rows)[:, None] == 0, (rows, width))
    qext = jnp.where(first_row(ATTN_EXT_DIM - HEAD_DIM, ATTN_GROUP * ATTN_TQ), -bound, 0.0).astype(BF16)
    vext = jnp.asarray(first_row(ATTN_V_ROWS - HEAD_DIM, ATTN_KV_PAGE), BF16)
    cos, sin, cos_t, sin_t = rope
    tok = lambda width: pl.BlockSpec((tm, width), lambda i: (i, 0))
    heads = lambda n, width=HEAD_DIM: pl.BlockSpec((n, tm, width), lambda i: (0, i, 0))
    feat_t = lambda rows: pl.BlockSpec((None, rows, tm), lambda i: (i // nl, 0, i % nl))
    outs = pl.pallas_call(
        _proj_kernel,
        grid=(m // tm,),
        in_specs=[tok(D_MODEL), _resident_layer(wt, layer), _resident_layer(w_bf, layer), _resident(qgain), _resident(kgain),
                  pl.BlockSpec((HEAD_DIM, tm), lambda i: (0, i % nl)),
                  pl.BlockSpec((HEAD_DIM, tm), lambda i: (0, i % nl)),
                  pl.BlockSpec((tm, LANES), lambda i: (i % nl, 0)),
                  pl.BlockSpec((tm, LANES), lambda i: (i % nl, 0)),
                  _resident(seg), _resident(qext), _resident(vext)],
        out_specs=[pl.BlockSpec((None, ATTN_KV_HEADS, tm // ATTN_TQ, ATTN_EXT_DIM, ATTN_GROUP * ATTN_TQ),
                                lambda i: (i // nl, 0, i % nl, 0, 0)),
                   pl.BlockSpec((None, pages, ATTN_KV_HEADS * ATTN_V_ROWS, ATTN_KV_PAGE),
                                lambda i: (i // nl, i % nl, 0, 0)),
                   feat_t(MIX_WIDTH),
                   heads(ATTN_KV_HEADS, ATTN_EXT_DIM), heads(NAT_HEADS), heads(NAT_HEADS),
                   pl.BlockSpec((SSM_GROUPS, tm // SSM_CHUNK, S5_TILE), lambda i: (0, i % nl, i // nl)),
                   tok(GATE_WIDTH)],
        out_shape=[jax.ShapeDtypeStruct((bsz, ATTN_KV_HEADS, seq_len // ATTN_TQ, ATTN_EXT_DIM, ATTN_GROUP * ATTN_TQ), BF16),
                   jax.ShapeDtypeStruct((bsz, seq_len // ATTN_KV_PAGE, ATTN_KV_HEADS * ATTN_V_ROWS, ATTN_KV_PAGE), BF16),
                   jax.ShapeDtypeStruct((bsz, MIX_WIDTH, seq_len), BF16),
                   jax.ShapeDtypeStruct((ATTN_KV_HEADS, m, ATTN_EXT_DIM), BF16),
                   jax.ShapeDtypeStruct((NAT_HEADS, m, HEAD_DIM), BF16),
                   jax.ShapeDtypeStruct((NAT_HEADS, m, HEAD_DIM), BF16),
                   jax.ShapeDtypeStruct((SSM_GROUPS, seq_len // SSM_CHUNK, bsz * S5_TILE), BF16),
                   jax.ShapeDtypeStruct((m, GATE_WIDTH), BF16)],
        scratch_shapes=[pltpu.VMEM((MIX_WIDTH // LANES, tm, LANES), F32)],
        compiler_params=_params("parallel"),
        name="in_proj",
    )(x2d, wt, w_bf, qgain, kgain, cos_t, sin_t, cos, sin, seg, qext, vext)
    return outs, bound


def _finish_attention(o_ref, qb, o):
    o_ref[ATTN_TQ * qb:ATTN_TQ * (qb + 1), :] = jnp.concatenate(
        [o[:, ATTN_TQ * g:ATTN_TQ * (g + 1)].T for g in range(ATTN_GROUP)], axis=1).astype(o_ref.dtype)


def _flash_bounded_kernel(qg_ref, k_ref, vt_ref, o_ref, *, n_pages):
    d = HEAD_DIM
    for qb in range(qg_ref.shape[0]):
        qg = qg_ref[qb]
        acc = jnp.zeros((ATTN_V_ROWS, qg.shape[1]), F32)
        for c in range(n_pages):
            s = _dot(k_ref[ATTN_KV_PAGE * c:ATTN_KV_PAGE * (c + 1), :], qg)
            acc = acc + _dot(vt_ref[c], jnp.exp2(s).astype(BF16))
        _finish_attention(o_ref, qb, acc[:d] / acc[d:d + 1])


def _flash_online_kernel(qg_ref, k_ref, vt_ref, o_ref, acc_scr, *, n_pages):
    d = HEAD_DIM
    for qb in range(qg_ref.shape[0]):
        qg = qg_ref[qb]
        nq = qg.shape[1]
        acc_scr[...] = jnp.zeros_like(acc_scr)

        def scores(c):
            return _dot(k_ref[ATTN_KV_PAGE * c:ATTN_KV_PAGE * (c + 1), :], qg)

        m = jnp.full((1, nq), -jnp.inf, F32)
        l = jnp.zeros((1, nq), F32)
        s_next = scores(0)
        for c in range(n_pages):
            s = s_next
            if c + 1 < n_pages:
                s_next = scores(c + 1)
            m_new = jnp.maximum(m, jnp.max(s, axis=0, keepdims=True))
            alpha = jnp.exp2(m - m_new)
            p = jnp.exp2(s - m_new)
            l = alpha * l + jnp.sum(p, axis=0, keepdims=True)
            acc_scr[...] = alpha * acc_scr[...] + _dot(vt_ref[c, 0:d, :], p.astype(BF16))
            m = m_new
        _finish_attention(o_ref, qb, acc_scr[...] / l)


def _gqa_attention(qg, k, vt, bound, bsz, seq_len):
    d, hk, grp, tq = HEAD_DIM, ATTN_KV_HEADS, ATTN_GROUP, ATTN_TQ
    nqb = ATTN_Q_BLOCKS_PER_STEP
    nq, n_pages = seq_len // (tq * nqb), seq_len // ATTN_KV_PAGE

    def call(body, scratch):
        return pl.pallas_call(
            functools.partial(body, n_pages=n_pages),
            grid=(bsz, hk, nq),
            in_specs=[pl.BlockSpec((None, None, nqb, ATTN_EXT_DIM, grp * tq), lambda b, j, i: (b, j, i, 0, 0)),
                      pl.BlockSpec((None, seq_len, ATTN_EXT_DIM), lambda b, j, i: (j, b, 0)),
                      pl.BlockSpec((None, n_pages, ATTN_V_ROWS, ATTN_KV_PAGE), lambda b, j, i: (b, 0, j, 0))],
            out_specs=pl.BlockSpec((tq * nqb, grp * d), lambda b, j, i: (b * nq + i, j)),
            out_shape=jax.ShapeDtypeStruct((bsz * seq_len, ATTN_Q_HEADS * d), BF16),
            scratch_shapes=scratch,
            compiler_params=_params("parallel", "parallel", "parallel"),
            name=body.__name__.strip("_").replace("_kernel", ""),
        )(qg, k, vt)

    return lax.cond(bound <= ATTN_BOUND_LIMIT,
                    lambda: call(_flash_bounded_kernel, []),
                    lambda: call(_flash_online_kernel, [pltpu.VMEM((d, grp * tq), F32)]))


NAT_ROWS_PER_STEP = 8
NAT_HEAD_UNROLL = 8
NAT_STEP_TOKENS = NAT_ROWS_PER_STEP * GRID_W
NAT_WIN_TOKENS = NAT_WIN_ROWS * GRID_W


def _nat_kernel(q_ref, ktp_ref, ktc_ref, ktn_ref, vp_ref, vc_ref, vn_ref, bias_ref, o_ref, kt_scr, v_scr, oh_scr):
    g = pl.program_id(1)
    ng = pl.num_programs(1)
    st = NAT_STEP_TOKENS
    d = HEAD_DIM
    kt_scr[:, 0:st] = ktp_ref[...]
    kt_scr[:, st:2 * st] = ktc_ref[...]
    kt_scr[:, 2 * st:3 * st] = ktn_ref[...]
    v_scr[:, 0:st, :] = vp_ref[...]
    v_scr[:, st:2 * st, :] = vc_ref[...]
    v_scr[:, 2 * st:3 * st, :] = vn_ref[...]

    def all_heads(frame_rows, bias_index):
        starts = [fr * GRID_W for fr in frame_rows]

        def head(h, carry):
            q = q_ref[h]
            r0 = pl.multiple_of(h * d, d)
            s = jnp.concatenate(
                [_dot(q[GRID_W * r:GRID_W * (r + 1), :],
                      kt_scr[pl.ds(r0, d), starts[r]:starts[r] + NAT_WIN_TOKENS]) + bias_ref[bias_index[r], h]
                 for r in range(NAT_ROWS_PER_STEP)], axis=0)
            m = jnp.max(s, axis=1, keepdims=True)
            p = jnp.exp(s - m)
            l = jnp.sum(p, axis=1, keepdims=True)
            pb = p.astype(BF16)
            o = jnp.concatenate(
                [_dot(pb[GRID_W * r:GRID_W * (r + 1), :], v_scr[h, starts[r]:starts[r] + NAT_WIN_TOKENS, :])
                 for r in range(NAT_ROWS_PER_STEP)], axis=0)
            oh_scr[h] = o / l
            return carry

        lax.fori_loop(0, NAT_HEADS, head, 0, unroll=NAT_HEAD_UNROLL)

    half = NAT_WIN_ROWS // 2
    n = NAT_ROWS_PER_STEP

    @pl.when(g == 0)
    def _():
        all_heads([n + max(r - half, 0) for r in range(n)], [min(r, half) for r in range(n)])

    @pl.when(jnp.logical_and(g > 0, g < ng - 1))
    def _():
        all_heads([r + half for r in range(n)], [half] * n)

    @pl.when(jnp.logical_and(g == ng - 1, g > 0))
    def _():
        all_heads([min(r + half, n) for r in range(n)], [max(r, half) for r in range(n)])

    o_ref[...] = jnp.concatenate([oh_scr[h] for h in range(NAT_HEADS)], axis=1).astype(o_ref.dtype)


def _nat_bias_table(rel_bias):
    w, nb = GRID_W, 2 * NAT_WIN_COLS - 1
    cols = np.arange(w)
    col_start = np.clip(cols - NAT_WIN_COLS // 2, 0, w - NAT_WIN_COLS)
    in_win = (cols[None, :] >= col_start[:, None]) & (cols[None, :] < col_start[:, None] + NAT_WIN_COLS)
    v = np.arange(NAT_WIN_ROWS)
    row_off = v[None, :] - v[:, None] + (NAT_WIN_ROWS - 1)
    row_hot = (row_off[:, :, None] == np.arange(2 * NAT_WIN_ROWS - 1)[None, None, :]).astype(np.float32)
    by_row = jnp.sum(rel_bias.astype(F32)[:, None, None] * row_hot[None, :, :, :, None], axis=3)
    period = nb + w + 1
    padded = jnp.pad(by_row, ((0, 0), (0, 0), (0, 0), (0, period - nb)))
    skewed = jnp.tile(padded, (1, 1, 1, w))[..., :w * (period - 1)].reshape(by_row.shape[:3] + (w, period - 1))
    b = skewed[..., NAT_WIN_COLS - 1:NAT_WIN_COLS - 1 + w]
    b = jnp.where(in_win[None, None, None], b, MASK_VALUE).transpose(1, 0, 3, 2, 4)
    return b.reshape(NAT_WIN_ROWS, NAT_HEADS, w, NAT_WIN_TOKENS)


def _neighbourhood_attention(nq, nkt, nv, bias, layer, bsz, seq_len):
    h, d, st = NAT_HEADS, HEAD_DIM, NAT_STEP_TOKENS
    ng = seq_len // st
    assert ng >= 2 and seq_len // GRID_W >= NAT_WIN_ROWS
    prev = lambda g: jnp.maximum(g - 1, 0)
    nxt = lambda g: jnp.minimum(g + 1, ng - 1)
    same = lambda g: g
    kt_spec = lambda f: pl.BlockSpec((None, h * d, st), lambda b, g: (b, 0, f(g)))
    v_spec = lambda f: pl.BlockSpec((h, st, d), lambda b, g: (0, b * ng + f(g), 0))
    return pl.pallas_call(
        _nat_kernel,
        grid=(bsz, ng),
        in_specs=[v_spec(same), kt_spec(prev), kt_spec(same), kt_spec(nxt),
                  v_spec(prev), v_spec(same), v_spec(nxt), _resident_layer(bias, layer)],
        out_specs=pl.BlockSpec((st, h * d), lambda b, g: (b * ng + g, 0)),
        out_shape=jax.ShapeDtypeStruct((bsz * seq_len, h * d), BF16),
        scratch_shapes=[pltpu.VMEM((h * d, 3 * st), BF16), pltpu.VMEM((h, 3 * st, d), BF16),
                        pltpu.VMEM((h, st, d), F32)],
        compiler_params=_params("parallel", "parallel"),
        name="nat",
    )(nq, nkt, nkt, nkt, nv, nv, nv, bias)


def _s5_matrices(a_re, a_im, log_dt, b_re, b_im, c_re, c_im, d_skip):
    t_len, hs = SSM_CHUNK, SSM_GROUP
    a_re = a_re.astype(F32)
    a_im = a_im.astype(F32)
    dt = jnp.exp(log_dt.astype(F32))[..., None]
    decay = jnp.exp(a_re * dt)
    phase = a_im * dt
    lam_re = decay * jnp.cos(phase)
    lam_im = decay * jnp.sin(phase)
    den = a_re * a_re + a_im * a_im
    num_re = lam_re - 1.0
    coef_re = (num_re * a_re + lam_im * a_im) / den
    coef_im = (lam_im * a_re - num_re * a_im) / den
    b_re = b_re.astype(F32)[None]
    b_im = b_im.astype(F32)[None]
    bbar_re = coef_re[..., None] * b_re - coef_im[..., None] * b_im
    bbar_im = coef_re[..., None] * b_im + coef_im[..., None] * b_re
    c_re = c_re.astype(F32)
    c_im = c_im.astype(F32)

    def powers(exponents):
        e = jnp.asarray(exponents, F32)[:, None, None, None]
        mag = jnp.exp(e * (a_re * dt)[None])
        return mag * jnp.cos(e * phase[None]), mag * jnp.sin(e * phase[None])

    def c_times(p_re, p_im):
        return (c_re[None] * p_re[:, :, :, None, :] - c_im[None] * p_im[:, :, :, None, :],
                c_re[None] * p_im[:, :, :, None, :] + c_im[None] * p_re[:, :, :, None, :])

    def times_bbar(p_re, p_im):
        return (p_re[..., None] * bbar_re[None] - p_im[..., None] * bbar_im[None],
                p_re[..., None] * bbar_im[None] + p_im[..., None] * bbar_re[None])

    tk = np.arange(t_len)
    cl_re, cl_im = c_times(*powers(tk))
    taps = jnp.sum(cl_re[..., None] * bbar_re[None, :, :, None] - cl_im[..., None] * bbar_im[None, :, :, None], axis=4)
    lag = tk[None, :] - tk[:, None]
    hot_f = (lag[:, :, None] == tk[None, None, :]).astype(np.float32)
    hot_r = (-lag[:, :, None] == tk[None, None, :]).astype(np.float32)
    eye_t = np.eye(t_len, dtype=np.float32)[:, :, None, None, None]
    skip = eye_t * (np.eye(hs, dtype=np.float32)[None] * d_skip.astype(F32)[:, :, None])[None, None]
    hp = lax.Precision.HIGHEST
    toep = (jnp.einsum('ktx,xgoi->ktgoi', hot_f, taps[:, 0], precision=hp)
            + jnp.einsum('ktx,xgoi->ktgoi', hot_r, taps[:, 1], precision=hp) + skip)
    toep = toep.transpose(2, 0, 4, 1, 3).reshape(SSM_GROUPS, S5_TILE, S5_TILE)

    to_in = lambda m: m.transpose(1, 0, 3, 2).reshape(SSM_GROUPS, S5_TILE, SSM_STATE)
    f_re, f_im = times_bbar(*powers(t_len - 1 - tk))
    r_re, r_im = times_bbar(*powers(tk))
    b_mat = jnp.concatenate([to_in(f_re[:, 0]), to_in(r_re[:, 1]), to_in(f_im[:, 0]), to_in(r_im[:, 1])], axis=2)

    to_out = lambda m: m.transpose(1, 3, 0, 2).reshape(SSM_GROUPS, SSM_STATE, S5_TILE)
    of_re, of_im = c_times(*powers(tk + 1))
    or_re, or_im = c_times(*powers(t_len - tk))
    m_mat = jnp.concatenate([to_out(of_re[:, 0]), to_out(or_re[:, 1]), to_out(-of_im[:, 0]), to_out(-or_im[:, 1])],
                            axis=1)

    def chunk_powers(n_fwd, n_rev):
        (fr, fi), (rr, ri) = powers(n_fwd * t_len), powers(n_rev * t_len)
        return jnp.concatenate([fr[:, 0], rr[:, 1], fi[:, 0], ri[:, 1]], axis=-1)
    doubling = np.array([1, 2, 4])
    steps = chunk_powers(doubling, doubling)
    rows = np.arange(S5_SCAN_ROWS)
    carry = chunk_powers(rows + 1, S5_SCAN_ROWS - rows)
    return toep.astype(BF16), b_mat.astype(BF16), m_mat.astype(BF16), steps, carry


def _s5_kernel(u_ref, b_ref, t_ref, m_ref, step_ref, carry_ref, y_ref, z_scr, s_scr, *, n_tiles, bsz):
    for b in range(bsz):
        cols = slice(S5_TILE * b, S5_TILE * (b + 1))
        z_scr[:, cols] = _dot(u_ref[:, cols], b_ref[...])
    _s5_chunk_scan(z_scr, step_ref, carry_ref, s_scr, n_tiles=n_tiles, pairs=bsz)
    for b in range(bsz):
        cols = slice(S5_TILE * b, S5_TILE * (b + 1))
        y = _dot(u_ref[:, cols], t_ref[...]) + _dot(s_scr[:, cols].astype(BF16), m_ref[...])
        y_ref[:, cols] = y.astype(y_ref.dtype)


def _s5_chunk_scan(z_ref, step_ref, carry_ref, s_ref, *, n_tiles, pairs):
    rows = S5_SCAN_ROWS
    lane = lax.broadcasted_iota(jnp.int32, (rows, LANES), 1)
    sub = lax.broadcasted_iota(jnp.int32, (rows, LANES), 0)
    is_fwd = lane < SSM_STATE
    is_rev = jnp.logical_not(is_fwd)
    both = lambda fwd_rows, rev_rows: jnp.logical_or(jnp.logical_and(is_fwd, fwd_rows),
                                                     jnp.logical_and(is_rev, rev_rows))
    edge = both(sub == 0, sub == rows - 1)

    def upstream(x, dist):
        valid = both(sub >= dist, sub < rows - dist)
        return jnp.where(valid, jnp.where(is_fwd, pltpu.roll(x, dist, 0), pltpu.roll(x, rows - dist, 0)), 0.0)

    def step(k, carry):
        rf = pl.multiple_of(k * rows, rows)
        rr = pl.multiple_of((n_tiles - 1 - k) * rows, rows)
        new = []
        for j in range(pairs):
            cre = slice(S5_STATE_COLS * j, S5_STATE_COLS * j + LANES)
            cim = slice(S5_STATE_COLS * j + LANES, S5_STATE_COLS * (j + 1))
            c_re, c_im = carry[j]
            x_re = jnp.where(is_fwd, z_ref[pl.ds(rf, rows), cre], z_ref[pl.ds(rr, rows), cre])
            x_im = jnp.where(is_fwd, z_ref[pl.ds(rf, rows), cim], z_ref[pl.ds(rr, rows), cim])
            for i, dist in enumerate((1, 2, 4)):
                lr = step_ref[i:i + 1, cre]
                li = step_ref[i:i + 1, cim]
                u_re, u_im = upstream(x_re, dist), upstream(x_im, dist)
                x_re, x_im = x_re + lr * u_re - li * u_im, x_im + lr * u_im + li * u_re
            pr, pi = carry_ref[:, cre], carry_ref[:, cim]
            a_re = x_re + pr * c_re - pi * c_im
            a_im = x_im + pr * c_im + pi * c_re
            e_re = jnp.where(edge, c_re, upstream(a_re, 1))
            e_im = jnp.where(edge, c_im, upstream(a_im, 1))
            pltpu.store(s_ref.at[pl.ds(rf, rows), cre], e_re, mask=is_fwd)
            pltpu.store(s_ref.at[pl.ds(rf, rows), cim], e_im, mask=is_fwd)
            pltpu.store(s_ref.at[pl.ds(rr, rows), cre], e_re, mask=is_rev)
            pltpu.store(s_ref.at[pl.ds(rr, rows), cim], e_im, mask=is_rev)
            last = lambda a: jnp.where(is_fwd, jnp.broadcast_to(a[rows - 1:rows], a.shape),
                                       jnp.broadcast_to(a[0:1], a.shape))
            new.append((last(a_re), last(a_im)))
        return tuple(new)

    zero = jnp.zeros((rows, LANES), F32)
    lax.fori_loop(0, n_tiles, step, tuple((zero, zero) for _ in range(pairs)))


def _s5_bidirectional(u2, mats, layer, bsz, seq_len):
    toep, b_mat, m_mat, steps, carry = mats
    g = SSM_GROUPS
    n_chunks = seq_len // SSM_CHUNK
    assert n_chunks % S5_SCAN_ROWS == 0
    width = bsz * S5_TILE
    u_spec = pl.BlockSpec((None, n_chunks, width), lambda j: (j, 0, 0))
    w_spec = pl.BlockSpec((None, None, S5_TILE, S5_TILE), lambda j: (layer, j, 0, 0))
    return pl.pallas_call(
        functools.partial(_s5_kernel, n_tiles=n_chunks // S5_SCAN_ROWS, bsz=bsz),
        grid=(g,),
        in_specs=[u_spec, w_spec, w_spec, w_spec,
                  pl.BlockSpec((None, steps.shape[1], width), lambda j: (layer, 0, j)),
                  pl.BlockSpec((None, S5_SCAN_ROWS, width), lambda j: (layer, 0, j))],
        out_specs=u_spec,
        out_shape=jax.ShapeDtypeStruct((g, n_chunks, width), BF16),
        scratch_shapes=[pltpu.VMEM((n_chunks, width), F32), pltpu.VMEM((n_chunks, width), F32)],
        compiler_params=_params("parallel"),
        name="s5",
    )(u2, b_mat, toep, m_mat, steps, carry)


MERGE_ROW_PARTS = 2


def _merge_kernel(x_ref, attn_ref, nat_ref, y_ref, gate_ref, wglu_ref, wb_ref, wout_ref, gain_ref, bias_ref, o_ref,
                  y_scr):
    tm = x_ref.shape[0]
    _from_chunk_layout(y_ref, y_scr, tm // SSM_CHUNK)
    d = D_MODEL
    part = tm // MERGE_ROW_PARTS
    for r in range(MERGE_ROW_PARTS):
        rows = slice(part * r, part * (r + 1))
        z = jax.nn.gelu(jnp.concatenate([y_scr[q, rows, :] for q in range(y_scr.shape[0])], axis=1))
        ssm = z * jax.nn.sigmoid(_dot(z.astype(BF16), wglu_ref[...]))
        merged = gate_ref[rows, 0:d].astype(F32) * _dot(attn_ref[rows, :], wb_ref[0])
        merged += gate_ref[rows, d:2 * d].astype(F32) * _dot(nat_ref[rows, :], wb_ref[1])
        merged += gate_ref[rows, 2 * d:3 * d].astype(F32) * _dot(ssm.astype(BF16), wb_ref[2])
        mix = _dot(merged.astype(BF16), wout_ref[...])
        o_ref[rows, :] = _layer_norm(DEEPNORM_ALPHA * x_ref[rows, :] + mix, gain_ref[...], bias_ref[...])


def _merge(x2d, attn_o, nat_o, y_ssm, gates, w_glu, w_branch, w_out, layer, gain, bias, seq_len, tm):
    m = x2d.shape[0]
    nl = seq_len // tm
    row = lambda width: pl.BlockSpec((tm, width), lambda i: (i, 0))
    gain, bias = gain.astype(F32)[None, :], bias.astype(F32)[None, :]
    return pl.pallas_call(
        _merge_kernel,
        grid=(m // tm,),
        in_specs=[row(D_MODEL), row(MIX_WIDTH), row(MIX_WIDTH),
                  pl.BlockSpec((SSM_GROUPS, tm // SSM_CHUNK, S5_TILE), lambda i: (0, i % nl, i // nl)),
                  row(GATE_WIDTH), _resident_layer(w_glu, layer), _resident_layer(w_branch, layer),
                  _resident_layer(w_out, layer), _resident(gain), _resident(bias)],
        out_specs=row(D_MODEL),
        out_shape=jax.ShapeDtypeStruct((m, D_MODEL), F32),
        scratch_shapes=[pltpu.VMEM((MIX_WIDTH // LANES, tm, LANES), F32)],
        compiler_params=_params("parallel"),
        name="merge",
    )(x2d, attn_o, nat_o, y_ssm, gates, w_glu, w_branch, w_out, gain, bias)


def _ffn_kernel(x_ref, wup_ref, wdown_ref, gain_ref, bias_ref, o_ref, xb_scr, acc_scr):
    f = pl.program_id(1)

    @pl.when(f == 0)
    def _():
        xb_scr[...] = x_ref[...].astype(BF16)
        acc_scr[...] = jnp.zeros_like(acc_scr)

    h = jnp.maximum(_dot(xb_scr[...], wup_ref[...]), 0.0)
    acc_scr[...] += _dot((h * h).astype(BF16), wdown_ref[...])

    @pl.when(f == pl.num_programs(1) - 1)
    def _():
        o_ref[...] = _layer_norm(DEEPNORM_ALPHA * x_ref[...] + acc_scr[...], gain_ref[...], bias_ref[...])


def _ffn(x2d, w_up, w_down, layer, gain, bias, tm, tf):
    m = x2d.shape[0]
    gain, bias = gain.astype(F32)[None, :], bias.astype(F32)[None, :]
    vec = pl.BlockSpec((1, D_MODEL), lambda i, f: (0, 0))
    return pl.pallas_call(
        _ffn_kernel,
        grid=(m // tm, FFN_DIM // tf),
        in_specs=[pl.BlockSpec((tm, D_MODEL), lambda i, f: (i, 0)),
                  pl.BlockSpec((None, D_MODEL, tf), lambda i, f: (layer, 0, f)),
                  pl.BlockSpec((None, tf, D_MODEL), lambda i, f: (layer, f, 0)),
                  vec, vec],
        out_specs=pl.BlockSpec((tm, D_MODEL), lambda i, f: (i, 0)),
        out_shape=jax.ShapeDtypeStruct((m, D_MODEL), F32),
        scratch_shapes=[pltpu.VMEM((tm, D_MODEL), BF16), pltpu.VMEM((tm, D_MODEL), F32)],
        compiler_params=_params("parallel", "arbitrary"),
        name="ffn",
    )(x2d, w_up, w_down, gain, bias)


def _tile_sizes(seq_len):
    proj_tm = min(512, seq_len)
    ffn_tm = min(1024, seq_len)
    ffn_tf = 1024
    return proj_tm, ffn_tm, ffn_tf


def kernel(x, w_in, q_norm_gain, k_norm_gain, nat_rel_bias, ssm_a_re, ssm_a_im, ssm_log_dt, ssm_b_re, ssm_b_im, ssm_c_re, ssm_c_im, ssm_d, ssm_w_glu, w_branch, w_out, ln1_gain, ln1_bias, w_ffn_up, w_ffn_down, ln2_gain, ln2_bias):
    bsz, seq_len, _ = x.shape
    proj_tm, ffn_tm, ffn_tf = _tile_sizes(seq_len)
    rope = _rope_tables(seq_len)
    w_bf = w_in.astype(BF16)
    wt = _transposed_proj_weights(w_bf)
    toep, b_mat, m_mat, steps, carry = jax.vmap(_s5_matrices)(
        ssm_a_re, ssm_a_im, ssm_log_dt, ssm_b_re, ssm_b_im, ssm_c_re, ssm_c_im, ssm_d)
    per_batch = lambda t: jnp.tile(t[:, :, :, None, :], (1, 1, 1, bsz, 1)).reshape(t.shape[0], t.shape[1], -1)
    mats = (toep, b_mat, m_mat, per_batch(steps), per_batch(carry))
    nat_bias = jax.vmap(_nat_bias_table)(nat_rel_bias)
    w_glu, w_br, w_o = ssm_w_glu.astype(BF16), w_branch.astype(BF16), w_out.astype(BF16)
    w_up, w_down = w_ffn_up.astype(BF16), w_ffn_down.astype(BF16)
    h = x.reshape(bsz * seq_len, D_MODEL)
    for layer in range(w_in.shape[0]):
        (qt, vt, nkt, k, nq, nv, u2, gates), bound = _input_projections(
            h, w_bf, wt, layer, q_norm_gain[layer], k_norm_gain[layer], rope, bsz, seq_len, proj_tm)
        attn_o = _gqa_attention(qt, k, vt, bound, bsz, seq_len)
        nat_o = _neighbourhood_attention(nq, nkt, nv, nat_bias, layer, bsz, seq_len)
        y2 = _s5_bidirectional(u2, mats, layer, bsz, seq_len)
        h = _merge(h, attn_o, nat_o, y2, gates, w_glu, w_br, w_o, layer, ln1_gain[layer], ln1_bias[layer],
                   seq_len, proj_tm)
        h = _ffn(h, w_up, w_down, layer, ln2_gain[layer], ln2_bias[layer], ffn_tm, ffn_tf)
    return h.reshape(bsz, seq_len, D_MODEL)
```

```python
import functools

import jax
import jax.numpy as jnp
from jax import lax
from jax.experimental import pallas as pl
from jax.experimental.pallas import tpu as pltpu

D_MODEL = 1024
DEPTH = 2
GRID_W = 64
HEAD_DIM = 64
MIX_WIDTH = 512
ATTN_Q_HEADS = 8
ATTN_KV_HEADS = 2
ATTN_GROUP = ATTN_Q_HEADS // ATTN_KV_HEADS
NAT_HEADS = 8
NAT_WIN_ROWS = 8
NAT_WIN_COLS = 16
SSM_GROUP = 16
SSM_GROUPS = 32
SSM_STATE = 64
SSM_CHUNK = 16
N_BRANCHES = 3
FFN_DIM = 4 * D_MODEL
ROPE_THETA = 10000.0
LN_EPS = 1e-5
RMS_EPS = 1e-6
DEEPNORM_ALPHA = (2 * DEPTH) ** 0.25
ATTN_SCALE = HEAD_DIM ** -0.5
LOG2_E = 1.4426950408889634
MASK_VALUE = -1e30

Q_WIDTH = ATTN_Q_HEADS * HEAD_DIM
KV_WIDTH = ATTN_KV_HEADS * HEAD_DIM
GATE_WIDTH = N_BRANCHES * D_MODEL

LANES = 128
SUBLANES = 8
MXU_WIDTH = 256
V7X_VMEM_BYTES = 64 * 1024 * 1024
VMEM_LIMIT = V7X_VMEM_BYTES - 8 * 1024 * 1024

F32 = jnp.float32
BF16 = jnp.bfloat16
NT_DIMS = (((1,), (1,)), ((), ()))


def _params(*semantics):
    return pltpu.CompilerParams(dimension_semantics=semantics, vmem_limit_bytes=VMEM_LIMIT)


def _dot(a, b):
    return jnp.dot(a, b, preferred_element_type=F32)


def _layer_norm(x, gain, bias):
    mu = jnp.mean(x, axis=-1, keepdims=True)
    xc = x - mu
    var = jnp.mean(xc * xc, axis=-1, keepdims=True)
    return xc * lax.rsqrt(var + LN_EPS) * gain + bias


def _resident(a):
    return pl.BlockSpec(a.shape, lambda *_: (0,) * a.ndim, pipeline_mode=pl.Buffered(1))


def _resident_layer(a, layer):
    return pl.BlockSpec((None,) + a.shape[1:], lambda *_: (layer,) + (0,) * (a.ndim - 1),
                        pipeline_mode=pl.Buffered(1))


W_IN_OFFSETS = (0, Q_WIDTH, Q_WIDTH + KV_WIDTH, Q_WIDTH + 2 * KV_WIDTH, Q_WIDTH + 2 * KV_WIDTH + MIX_WIDTH,
                Q_WIDTH + 2 * KV_WIDTH + 2 * MIX_WIDTH, Q_WIDTH + 2 * KV_WIDTH + 3 * MIX_WIDTH,
                Q_WIDTH + 2 * KV_WIDTH + 4 * MIX_WIDTH)


ATTN_KV_PAGE = 512
ATTN_TQ = MXU_WIDTH
ATTN_Q_BLOCKS_PER_STEP = 2
ATTN_EXT_DIM = LANES
ATTN_V_ROWS = HEAD_DIM + 16
ATTN_BOUND_LIMIT = 60.0
PROJ_T_ROWS = Q_WIDTH + KV_WIDTH + MIX_WIDTH
S5_TILE = SSM_CHUNK * SSM_GROUP
S5_STATE_COLS = 4 * SSM_STATE
S5_SCAN_ROWS = SUBLANES


GROUPS_PER_TILE = LANES // SSM_GROUP
CHUNKS_PER_TILE = LANES // SSM_GROUP


def _to_chunk_layout(x_scr, u_ref, n_chunks):
    lane_grp = lax.broadcasted_iota(jnp.int32, (n_chunks, LANES), 1) // SSM_GROUP
    for half in range(SSM_CHUNK // CHUNKS_PER_TILE):
        for q in range(x_scr.shape[0]):
            steps = [x_scr[q, pl.ds(CHUNKS_PER_TILE * half + tp, n_chunks, stride=SSM_CHUNK), :]
                     for tp in range(CHUNKS_PER_TILE)]
            for gm in range(GROUPS_PER_TILE):
                tile = None
                for tp in range(CHUNKS_PER_TILE):
                    shift = (SSM_GROUP * (tp - gm)) % LANES
                    moved = pltpu.roll(steps[tp], shift, 1) if shift else steps[tp]
                    tile = moved if tile is None else jnp.where(lane_grp == tp, moved, tile)
                u_ref[GROUPS_PER_TILE * q + gm, :, LANES * half:LANES * (half + 1)] = tile.astype(u_ref.dtype)


def _from_chunk_layout(y_ref, y_scr, n_chunks):
    lane_grp = lax.broadcasted_iota(jnp.int32, (n_chunks, LANES), 1) // SSM_GROUP
    for half in range(SSM_CHUNK // CHUNKS_PER_TILE):
        for q in range(y_scr.shape[0]):
            groups = [y_ref[GROUPS_PER_TILE * q + gm, :, LANES * half:LANES * (half + 1)].astype(F32)
                      for gm in range(GROUPS_PER_TILE)]
            for tp in range(CHUNKS_PER_TILE):
                tile = None
                for gm in range(GROUPS_PER_TILE):
                    shift = (SSM_GROUP * (gm - tp)) % LANES
                    moved = pltpu.roll(groups[gm], shift, 1) if shift else groups[gm]
                    tile = moved if tile is None else jnp.where(lane_grp == gm, moved, tile)
                y_scr[q, pl.ds(CHUNKS_PER_TILE * half + tp, n_chunks, stride=SSM_CHUNK), :] = tile


def _proj_kernel(x_ref, wt_ref, w_ref, qgain_ref, kgain_ref, cost_ref, sint_ref, cos_ref, sin_ref, seg_ref,
                 qext_ref, vext_ref, qt_ref, vt_ref, nkt_ref, k_ref, nq_ref, nv_ref, u_ref, g_ref, su_scr):
    tm = x_ref.shape[0]
    d = HEAD_DIM
    xb = x_ref[...].astype(BF16)

    yt = lax.dot_general(wt_ref[...], xb, NT_DIMS, preferred_element_type=F32)
    cost = cost_ref[...]
    sint = sint_ref[...]
    qgain = qgain_ref[...]
    for h in range(ATTN_Q_HEADS):
        blk = yt[d * h:d * (h + 1), :]
        ms = jnp.mean(blk * blk, axis=0, keepdims=True)
        yn = blk * lax.rsqrt(ms + RMS_EPS) * qgain
        partner = jnp.concatenate([yn[16:32], yn[0:16], yn[48:64], yn[32:48]], axis=0)
        qh = ((yn * cost + partner * sint) * (ATTN_SCALE * LOG2_E)).astype(BF16)
        j, g = divmod(h, ATTN_GROUP)
        for qb in range(tm // ATTN_TQ):
            qt_ref[j, qb, 0:d, ATTN_TQ * g:ATTN_TQ * (g + 1)] = qh[:, ATTN_TQ * qb:ATTN_TQ * (qb + 1)]
    for j in range(ATTN_KV_HEADS):
        for qb in range(tm // ATTN_TQ):
            qt_ref[j, qb, d:ATTN_EXT_DIM, :] = qext_ref[...]
    vt = yt[Q_WIDTH:Q_WIDTH + KV_WIDTH, :].astype(BF16)
    for pg in range(tm // ATTN_KV_PAGE):
        cols = slice(ATTN_KV_PAGE * pg, ATTN_KV_PAGE * (pg + 1))
        for j in range(ATTN_KV_HEADS):
            vt_ref[pg, ATTN_V_ROWS * j:ATTN_V_ROWS * j + d, :] = vt[d * j:d * (j + 1), cols]
            vt_ref[pg, ATTN_V_ROWS * j + d:ATTN_V_ROWS * (j + 1), :] = vext_ref[...]
    nkt_ref[...] = yt[Q_WIDTH + KV_WIDTH:, :].astype(BF16)

    o_ak, o_nq, o_nv, o_su, o_gate = (W_IN_OFFSETS[i] for i in (1, 3, 5, 6, 7))
    yk = _dot(xb, w_ref[:, o_ak:o_ak + KV_WIDTH])
    y2 = yk * yk
    hi = y2.astype(BF16)
    lo = (y2 - hi.astype(F32)).astype(BF16)
    ms = (_dot(hi, seg_ref[...]) + _dot(lo, seg_ref[...])) * (1.0 / d)
    kn = yk * lax.rsqrt(ms + RMS_EPS) * kgain_ref[...]
    lane = lax.broadcasted_iota(jnp.int32, (tm, LANES), 1)
    partner = jnp.where((lane % 32) < 16, pltpu.roll(kn, LANES - 16, 1), pltpu.roll(kn, 16, 1))
    kk = kn * cos_ref[...] + partner * sin_ref[...]
    one_hot = (lane == d).astype(F32)
    for j in range(ATTN_KV_HEADS):
        kj = kk if j == 0 else pltpu.roll(kk, LANES - d * j, 1)
        k_ref[j] = jnp.where(lane < d, kj, one_hot).astype(BF16)

    ynq = _dot(xb, w_ref[:, o_nq:o_nq + MIX_WIDTH]) * ATTN_SCALE
    ynv = _dot(xb, w_ref[:, o_nv:o_nv + MIX_WIDTH])
    for h in range(NAT_HEADS):
        nq_ref[h] = ynq[:, d * h:d * (h + 1)].astype(BF16)
        nv_ref[h] = ynv[:, d * h:d * (h + 1)].astype(BF16)
    su = _dot(xb, w_ref[:, o_su:o_su + MIX_WIDTH])
    for q in range(MIX_WIDTH // LANES):
        su_scr[q] = su[:, LANES * q:LANES * (q + 1)]
    _to_chunk_layout(su_scr, u_ref, tm // SSM_CHUNK)
    for n in range(N_BRANCHES):
        y = _dot(xb, w_ref[:, o_gate + D_MODEL * n:o_gate + D_MODEL * (n + 1)])
        g_ref[:, D_MODEL * n:D_MODEL * (n + 1)] = jax.nn.sigmoid(y).astype(BF16)


def _rope_tables(seq_len):
    t = jnp.arange(seq_len)
    row = (t // GRID_W).astype(F32)
    col = (t % GRID_W).astype(F32)
    axis_dim = HEAD_DIM // 2
    inv_freq = 1.0 / (ROPE_THETA ** (jnp.arange(0, axis_dim, 2, dtype=F32) / axis_dim))
    ang_r = row[:, None] * inv_freq[None, :]
    ang_c = col[:, None] * inv_freq[None, :]
    cos_head = jnp.concatenate([jnp.cos(ang_r), jnp.cos(ang_r), jnp.cos(ang_c), jnp.cos(ang_c)], axis=1)
    sin_head = jnp.concatenate([-jnp.sin(ang_r), jnp.sin(ang_r), -jnp.sin(ang_c), jnp.sin(ang_c)], axis=1)
    reps = LANES // HEAD_DIM
    return (jnp.tile(cos_head, (1, reps)), jnp.tile(sin_head, (1, reps)), cos_head.T, sin_head.T)


def _transposed_proj_weights(w_bf):
    o = W_IN_OFFSETS
    parts = [w_bf[:, :, o[0]:o[1]], w_bf[:, :, o[2]:o[3]], w_bf[:, :, o[4]:o[5]]]
    return jnp.concatenate(parts, axis=2).transpose(0, 2, 1)


def _input_projections(x2d, w_bf, wt, layer, q_gain, k_gain, rope, bsz, seq_len, tm):
    m = x2d.shape[0]
    nl = seq_len // tm
    pages = tm // ATTN_KV_PAGE
    qgain = jnp.broadcast_to(q_gain.astype(F32)[:, None], (HEAD_DIM, tm))
    kgain = jnp.tile(k_gain.astype(F32), ATTN_KV_HEADS)[None, :]
    seg = (jnp.arange(KV_WIDTH)[:, None] // HEAD_DIM == jnp.arange(KV_WIDTH)[None, :] // HEAD_DIM).astype(BF16)
    bound = (HEAD_DIM * ATTN_SCALE * LOG2_E * 1.02) * jnp.max(jnp.abs(q_gain.astype(F32))) * jnp.max(jnp.abs(k_gain.astype(F32)))
    first_row = lambda rows, width: (jnp.arange(rows)[:, None] == 0) & (jnp.arange(width)[None, :] >= 0)
    qext = jnp.where(first_row(ATTN_EXT_DIM - HEAD_DIM, ATTN_GROUP * ATTN_TQ), -bound, 0.0).astype(BF16)
    vext = first_row(ATTN_V_ROWS - HEAD_DIM, ATTN_KV_PAGE).astype(BF16)
    cos, sin, cos_t, sin_t = rope
    tok = lambda width: pl.BlockSpec((tm, width), lambda i: (i, 0))
    heads = lambda n, width=HEAD_DIM: pl.BlockSpec((n, tm, width), lambda i: (0, i, 0))
    feat_t = lambda rows: pl.BlockSpec((None, rows, tm), lambda i: (i // nl, 0, i % nl))
    outs = pl.pallas_call(
        _proj_kernel,
        grid=(m // tm,),
        in_specs=[tok(D_MODEL), _resident_layer(wt, layer), _resident_layer(w_bf, layer), _resident(qgain), _resident(kgain),
                  pl.BlockSpec((HEAD_DIM, tm), lambda i: (0, i % nl)),
                  pl.BlockSpec((HEAD_DIM, tm), lambda i: (0, i % nl)),
                  pl.BlockSpec((tm, LANES), lambda i: (i % nl, 0)),
                  pl.BlockSpec((tm, LANES), lambda i: (i % nl, 0)),
                  _resident(seg), _resident(qext), _resident(vext)],
        out_specs=[pl.BlockSpec((None, ATTN_KV_HEADS, tm // ATTN_TQ, ATTN_EXT_DIM, ATTN_GROUP * ATTN_TQ),
                                lambda i: (i // nl, 0, i % nl, 0, 0)),
                   pl.BlockSpec((None, pages, ATTN_KV_HEADS * ATTN_V_ROWS, ATTN_KV_PAGE),
                                lambda i: (i // nl, i % nl, 0, 0)),
                   feat_t(MIX_WIDTH),
                   heads(ATTN_KV_HEADS, ATTN_EXT_DIM), heads(NAT_HEADS), heads(NAT_HEADS),
                   pl.BlockSpec((SSM_GROUPS, tm // SSM_CHUNK, S5_TILE), lambda i: (0, i % nl, i // nl)),
                   tok(GATE_WIDTH)],
        out_shape=[jax.ShapeDtypeStruct((bsz, ATTN_KV_HEADS, seq_len // ATTN_TQ, ATTN_EXT_DIM, ATTN_GROUP * ATTN_TQ), BF16),
                   jax.ShapeDtypeStruct((bsz, seq_len // ATTN_KV_PAGE, ATTN_KV_HEADS * ATTN_V_ROWS, ATTN_KV_PAGE), BF16),
                   jax.ShapeDtypeStruct((bsz, MIX_WIDTH, seq_len), BF16),
                   jax.ShapeDtypeStruct((ATTN_KV_HEADS, m, ATTN_EXT_DIM), BF16),
                   jax.ShapeDtypeStruct((NAT_HEADS, m, HEAD_DIM), BF16),
                   jax.ShapeDtypeStruct((NAT_HEADS, m, HEAD_DIM), BF16),
                   jax.ShapeDtypeStruct((SSM_GROUPS, seq_len // SSM_CHUNK, bsz * S5_TILE), BF16),
                   jax.ShapeDtypeStruct((m, GATE_WIDTH), BF16)],
        scratch_shapes=[pltpu.VMEM((MIX_WIDTH // LANES, tm, LANES), F32)],
        compiler_params=_params("parallel"),
        name="in_proj",
    )(x2d, wt, w_bf, qgain, kgain, cos_t, sin_t, cos, sin, seg, qext, vext)
    return outs, bound


def _finish_attention(o_ref, qb, o):
    o_ref[ATTN_TQ * qb:ATTN_TQ * (qb + 1), :] = jnp.concatenate(
        [o[:, ATTN_TQ * g:ATTN_TQ * (g + 1)].T for g in range(ATTN_GROUP)], axis=1).astype(o_ref.dtype)


def _flash_bounded_kernel(qg_ref, k_ref, vt_ref, o_ref, *, n_pages):
    d = HEAD_DIM
    for qb in range(qg_ref.shape[0]):
        qg = qg_ref[qb]
        acc = jnp.zeros((ATTN_V_ROWS, qg.shape[1]), F32)
        for c in range(n_pages):
            s = _dot(k_ref[ATTN_KV_PAGE * c:ATTN_KV_PAGE * (c + 1), :], qg)
            acc = acc + _dot(vt_ref[c], jnp.exp2(s).astype(BF16))
        _finish_attention(o_ref, qb, acc[:d] / acc[d:d + 1])


def _flash_online_kernel(qg_ref, k_ref, vt_ref, o_ref, acc_scr, *, n_pages):
    d = HEAD_DIM
    for qb in range(qg_ref.shape[0]):
        qg = qg_ref[qb]
        nq = qg.shape[1]
        acc_scr[...] = jnp.zeros_like(acc_scr)

        def scores(c):
            return _dot(k_ref[ATTN_KV_PAGE * c:ATTN_KV_PAGE * (c + 1), :], qg)

        m = jnp.full((1, nq), -jnp.inf, F32)
        l = jnp.zeros((1, nq), F32)
        s_next = scores(0)
        for c in range(n_pages):
            s = s_next
            if c + 1 < n_pages:
                s_next = scores(c + 1)
            m_new = jnp.maximum(m, jnp.max(s, axis=0, keepdims=True))
            alpha = jnp.exp2(m - m_new)
            p = jnp.exp2(s - m_new)
            l = alpha * l + jnp.sum(p, axis=0, keepdims=True)
            acc_scr[...] = alpha * acc_scr[...] + _dot(vt_ref[c, 0:d, :], p.astype(BF16))
            m = m_new
        _finish_attention(o_ref, qb, acc_scr[...] / l)


def _gqa_attention(qg, k, vt, bound, bsz, seq_len):
    d, hk, grp, tq = HEAD_DIM, ATTN_KV_HEADS, ATTN_GROUP, ATTN_TQ
    nqb = ATTN_Q_BLOCKS_PER_STEP
    nq, n_pages = seq_len // (tq * nqb), seq_len // ATTN_KV_PAGE

    def call(body, scratch):
        return pl.pallas_call(
            functools.partial(body, n_pages=n_pages),
            grid=(bsz, hk, nq),
            in_specs=[pl.BlockSpec((None, None, nqb, ATTN_EXT_DIM, grp * tq), lambda b, j, i: (b, j, i, 0, 0)),
                      pl.BlockSpec((None, seq_len, ATTN_EXT_DIM), lambda b, j, i: (j, b, 0)),
                      pl.BlockSpec((None, n_pages, ATTN_V_ROWS, ATTN_KV_PAGE), lambda b, j, i: (b, 0, j, 0))],
            out_specs=pl.BlockSpec((tq * nqb, grp * d), lambda b, j, i: (b * nq + i, j)),
            out_shape=jax.ShapeDtypeStruct((bsz * seq_len, ATTN_Q_HEADS * d), BF16),
            scratch_shapes=scratch,
            compiler_params=_params("parallel", "parallel", "parallel"),
            name=body.__name__.strip("_").replace("_kernel", ""),
        )(qg, k, vt)

    return lax.cond(bound <= ATTN_BOUND_LIMIT,
                    lambda: call(_flash_bounded_kernel, []),
                    lambda: call(_flash_online_kernel, [pltpu.VMEM((d, grp * tq), F32)]))


NAT_ROWS_PER_STEP = 8
NAT_HEAD_UNROLL = 8
NAT_STEP_TOKENS = NAT_ROWS_PER_STEP * GRID_W
NAT_WIN_TOKENS = NAT_WIN_ROWS * GRID_W


def _nat_kernel(q_ref, ktp_ref, ktc_ref, ktn_ref, vp_ref, vc_ref, vn_ref, bias_ref, o_ref, kt_scr, v_scr, oh_scr):
    g = pl.program_id(1)
    ng = pl.num_programs(1)
    st = NAT_STEP_TOKENS
    d = HEAD_DIM
    kt_scr[:, 0:st] = ktp_ref[...]
    kt_scr[:, st:2 * st] = ktc_ref[...]
    kt_scr[:, 2 * st:3 * st] = ktn_ref[...]
    v_scr[:, 0:st, :] = vp_ref[...]
    v_scr[:, st:2 * st, :] = vc_ref[...]
    v_scr[:, 2 * st:3 * st, :] = vn_ref[...]

    def all_heads(frame_rows, bias_index):
        starts = [fr * GRID_W for fr in frame_rows]

        def head(h, carry):
            q = q_ref[h]
            r0 = pl.multiple_of(h * d, d)
            s = jnp.concatenate(
                [_dot(q[GRID_W * r:GRID_W * (r + 1), :],
                      kt_scr[pl.ds(r0, d), starts[r]:starts[r] + NAT_WIN_TOKENS]) + bias_ref[bias_index[r], h]
                 for r in range(NAT_ROWS_PER_STEP)], axis=0)
            m = jnp.max(s, axis=1, keepdims=True)
            p = jnp.exp(s - m)
            l = jnp.sum(p, axis=1, keepdims=True)
            pb = p.astype(BF16)
            o = jnp.concatenate(
                [_dot(pb[GRID_W * r:GRID_W * (r + 1), :], v_scr[h, starts[r]:starts[r] + NAT_WIN_TOKENS, :])
                 for r in range(NAT_ROWS_PER_STEP)], axis=0)
            oh_scr[h] = o / l
            return carry

        lax.fori_loop(0, NAT_HEADS, head, 0, unroll=NAT_HEAD_UNROLL)

    half = NAT_WIN_ROWS // 2
    n = NAT_ROWS_PER_STEP

    @pl.when(g == 0)
    def _():
        all_heads([n + max(r - half, 0) for r in range(n)], [min(r, half) for r in range(n)])

    @pl.when(jnp.logical_and(g > 0, g < ng - 1))
    def _():
        all_heads([r + half for r in range(n)], [half] * n)

    @pl.when(jnp.logical_and(g == ng - 1, g > 0))
    def _():
        all_heads([min(r + half, n) for r in range(n)], [max(r, half) for r in range(n)])

    o_ref[...] = jnp.concatenate([oh_scr[h] for h in range(NAT_HEADS)], axis=1).astype(o_ref.dtype)


def _nat_bias_kernel(rows_ref, mask_ref, o_ref):
    in_window = mask_ref[...] > 0.5
    for v in range(NAT_WIN_ROWS):
        parts = [pltpu.roll(jnp.broadcast_to(rows_ref[v, p:p + 1, :], (GRID_W, LANES)),
                            LANES - (NAT_WIN_COLS - 1), 1, stride=1, stride_axis=0)
                 for p in range(NAT_WIN_ROWS // 2)]
        o_ref[v] = jnp.where(in_window, jnp.concatenate(parts, axis=1), MASK_VALUE)


def _nat_bias_table(rel_bias):
    layers = rel_bias.shape[0]
    w, nb = GRID_W, 2 * NAT_WIN_COLS - 1
    cols = jnp.arange(w)
    col_start = jnp.clip(cols - NAT_WIN_COLS // 2, 0, w - NAT_WIN_COLS)
    in_win = (cols[None, :] >= col_start[:, None]) & (cols[None, :] < col_start[:, None] + NAT_WIN_COLS)
    mask = jnp.tile(in_win, (1, NAT_WIN_ROWS)).astype(F32)
    v = jnp.arange(NAT_WIN_ROWS)
    row_off = v[None, :] - v[:, None] + (NAT_WIN_ROWS - 1)
    row_hot = (row_off[:, :, None] == jnp.arange(2 * NAT_WIN_ROWS - 1)[None, None, :]).astype(F32)
    by_row = jnp.sum(rel_bias.astype(F32)[:, :, None, None] * row_hot[None, None, :, :, :, None], axis=4)
    pairs = by_row.reshape(layers, NAT_HEADS, NAT_WIN_ROWS, NAT_WIN_ROWS // 2, 2, nb)
    packed = jnp.pad(pairs, ((0, 0),) * 5 + ((0, w - nb),)).reshape(layers, NAT_HEADS, NAT_WIN_ROWS, NAT_WIN_ROWS // 2, LANES)
    return pl.pallas_call(
        _nat_bias_kernel,
        grid=(layers, NAT_HEADS),
        in_specs=[pl.BlockSpec((None, None, NAT_WIN_ROWS, NAT_WIN_ROWS // 2, LANES), lambda l, h: (l, h, 0, 0, 0)),
                  pl.BlockSpec(mask.shape, lambda l, h: (0, 0))],
        out_specs=pl.BlockSpec((None, NAT_WIN_ROWS, None, w, NAT_WIN_TOKENS), lambda l, h: (l, 0, h, 0, 0)),
        out_shape=jax.ShapeDtypeStruct((layers, NAT_WIN_ROWS, NAT_HEADS, w, NAT_WIN_TOKENS), F32),
        compiler_params=_params("parallel", "parallel"),
        name="nat_bias",
    )(packed, mask)


def _neighbourhood_attention(nq, nkt, nv, bias, layer, bsz, seq_len):
    h, d, st = NAT_HEADS, HEAD_DIM, NAT_STEP_TOKENS
    ng = seq_len // st
    assert ng >= 2 and seq_len // GRID_W >= NAT_WIN_ROWS
    prev = lambda g: jnp.maximum(g - 1, 0)
    nxt = lambda g: jnp.minimum(g + 1, ng - 1)
    same = lambda g: g
    kt_spec = lambda f: pl.BlockSpec((None, h * d, st), lambda b, g: (b, 0, f(g)))
    v_spec = lambda f: pl.BlockSpec((h, st, d), lambda b, g: (0, b * ng + f(g), 0))
    return pl.pallas_call(
        _nat_kernel,
        grid=(bsz, ng),
        in_specs=[v_spec(same), kt_spec(prev), kt_spec(same), kt_spec(nxt),
                  v_spec(prev), v_spec(same), v_spec(nxt), _resident_layer(bias, layer)],
        out_specs=pl.BlockSpec((st, h * d), lambda b, g: (b * ng + g, 0)),
        out_shape=jax.ShapeDtypeStruct((bsz * seq_len, h * d), BF16),
        scratch_shapes=[pltpu.VMEM((h * d, 3 * st), BF16), pltpu.VMEM((h, 3 * st, d), BF16),
                        pltpu.VMEM((h, st, d), F32)],
        compiler_params=_params("parallel", "parallel"),
        name="nat",
    )(nq, nkt, nkt, nkt, nv, nv, nv, bias)


def _s5_matrices(a_re, a_im, log_dt, b_re, b_im, c_re, c_im, d_skip):
    t_len, hs = SSM_CHUNK, SSM_GROUP
    a_re = a_re.astype(F32)
    a_im = a_im.astype(F32)
    dt = jnp.exp(log_dt.astype(F32))[..., None]
    decay = jnp.exp(a_re * dt)
    phase = a_im * dt
    lam_re = decay * jnp.cos(phase)
    lam_im = decay * jnp.sin(phase)
    den = a_re * a_re + a_im * a_im
    num_re = lam_re - 1.0
    coef_re = (num_re * a_re + lam_im * a_im) / den
    coef_im = (lam_im * a_re - num_re * a_im) / den
    b_re = b_re.astype(F32)[None]
    b_im = b_im.astype(F32)[None]
    bbar_re = coef_re[..., None] * b_re - coef_im[..., None] * b_im
    bbar_im = coef_re[..., None] * b_im + coef_im[..., None] * b_re
    c_re = c_re.astype(F32)
    c_im = c_im.astype(F32)

    def powers(exponents):
        e = jnp.asarray(exponents, F32)[:, None, None, None]
        mag = jnp.exp(e * (a_re * dt)[None])
        return mag * jnp.cos(e * phase[None]), mag * jnp.sin(e * phase[None])

    def c_times(p_re, p_im):
        return (c_re[None] * p_re[:, :, :, None, :] - c_im[None] * p_im[:, :, :, None, :],
                c_re[None] * p_im[:, :, :, None, :] + c_im[None] * p_re[:, :, :, None, :])

    def times_bbar(p_re, p_im):
        return (p_re[..., None] * bbar_re[None] - p_im[..., None] * bbar_im[None],
                p_re[..., None] * bbar_im[None] + p_im[..., None] * bbar_re[None])

    tk = jnp.arange(t_len)
    cl_re, cl_im = c_times(*powers(tk))
    taps = jnp.sum(cl_re[..., None] * bbar_re[None, :, :, None] - cl_im[..., None] * bbar_im[None, :, :, None], axis=4)
    lag = tk[None, :] - tk[:, None]
    hot_f = (lag[:, :, None] == tk[None, None, :]).astype(F32)
    hot_r = (-lag[:, :, None] == tk[None, None, :]).astype(F32)
    eye_t = jnp.eye(t_len, dtype=F32)[:, :, None, None, None]
    skip = eye_t * (jnp.eye(hs, dtype=F32)[None] * d_skip.astype(F32)[:, :, None])[None, None]
    hp = lax.Precision.HIGHEST
    toep = (jnp.einsum('ktx,xgoi->ktgoi', hot_f, taps[:, 0], precision=hp)
            + jnp.einsum('ktx,xgoi->ktgoi', hot_r, taps[:, 1], precision=hp) + skip)
    toep = toep.transpose(2, 0, 4, 1, 3).reshape(SSM_GROUPS, S5_TILE, S5_TILE)

    to_in = lambda m: m.transpose(1, 0, 3, 2).reshape(SSM_GROUPS, S5_TILE, SSM_STATE)
    f_re, f_im = times_bbar(*powers(t_len - 1 - tk))
    r_re, r_im = times_bbar(*powers(tk))
    b_mat = jnp.concatenate([to_in(f_re[:, 0]), to_in(r_re[:, 1]), to_in(f_im[:, 0]), to_in(r_im[:, 1])], axis=2)

    to_out = lambda m: m.transpose(1, 3, 0, 2).reshape(SSM_GROUPS, SSM_STATE, S5_TILE)
    of_re, of_im = c_times(*powers(tk + 1))
    or_re, or_im = c_times(*powers(t_len - tk))
    m_mat = jnp.concatenate([to_out(of_re[:, 0]), to_out(or_re[:, 1]), to_out(-of_im[:, 0]), to_out(-or_im[:, 1])],
                            axis=1)

    def chunk_powers(n_fwd, n_rev):
        (fr, fi), (rr, ri) = powers(n_fwd * t_len), powers(n_rev * t_len)
        return jnp.concatenate([fr[:, 0], rr[:, 1], fi[:, 0], ri[:, 1]], axis=-1)
    doubling = jnp.array([1, 2, 4])
    steps = chunk_powers(doubling, doubling)
    rows = jnp.arange(S5_SCAN_ROWS)
    carry = chunk_powers(rows + 1, S5_SCAN_ROWS - rows)
    return toep.astype(BF16), b_mat.astype(BF16), m_mat.astype(BF16), steps, carry


def _s5_kernel(u_ref, b_ref, t_ref, m_ref, step_ref, carry_ref, y_ref, z_scr, s_scr, *, n_tiles, bsz):
    for b in range(bsz):
        cols = slice(S5_TILE * b, S5_TILE * (b + 1))
        z_scr[:, cols] = _dot(u_ref[:, cols], b_ref[...])
    _s5_chunk_scan(z_scr, step_ref, carry_ref, s_scr, n_tiles=n_tiles, pairs=bsz)
    for b in range(bsz):
        cols = slice(S5_TILE * b, S5_TILE * (b + 1))
        y = _dot(u_ref[:, cols], t_ref[...]) + _dot(s_scr[:, cols].astype(BF16), m_ref[...])
        y_ref[:, cols] = y.astype(y_ref.dtype)


def _s5_chunk_scan(z_ref, step_ref, carry_ref, s_ref, *, n_tiles, pairs):
    rows = S5_SCAN_ROWS
    lane = lax.broadcasted_iota(jnp.int32, (rows, LANES), 1)
    sub = lax.broadcasted_iota(jnp.int32, (rows, LANES), 0)
    is_fwd = lane < SSM_STATE
    is_rev = jnp.logical_not(is_fwd)
    both = lambda fwd_rows, rev_rows: jnp.logical_or(jnp.logical_and(is_fwd, fwd_rows),
                                                     jnp.logical_and(is_rev, rev_rows))
    edge = both(sub == 0, sub == rows - 1)

    def upstream(x, dist):
        valid = both(sub >= dist, sub < rows - dist)
        return jnp.where(valid, jnp.where(is_fwd, pltpu.roll(x, dist, 0), pltpu.roll(x, rows - dist, 0)), 0.0)

    def step(k, carry):
        rf = pl.multiple_of(k * rows, rows)
        rr = pl.multiple_of((n_tiles - 1 - k) * rows, rows)
        new = []
        for j in range(pairs):
            cre = slice(S5_STATE_COLS * j, S5_STATE_COLS * j + LANES)
            cim = slice(S5_STATE_COLS * j + LANES, S5_STATE_COLS * (j + 1))
            c_re, c_im = carry[j]
            x_re = jnp.where(is_fwd, z_ref[pl.ds(rf, rows), cre], z_ref[pl.ds(rr, rows), cre])
            x_im = jnp.where(is_fwd, z_ref[pl.ds(rf, rows), cim], z_ref[pl.ds(rr, rows), cim])
            for i, dist in enumerate((1, 2, 4)):
                lr = step_ref[i:i + 1, cre]
                li = step_ref[i:i + 1, cim]
                u_re, u_im = upstream(x_re, dist), upstream(x_im, dist)
                x_re, x_im = x_re + lr * u_re - li * u_im, x_im + lr * u_im + li * u_re
            pr, pi = carry_ref[:, cre], carry_ref[:, cim]
            a_re = x_re + pr * c_re - pi * c_im
            a_im = x_im + pr * c_im + pi * c_re
            e_re = jnp.where(edge, c_re, upstream(a_re, 1))
            e_im = jnp.where(edge, c_im, upstream(a_im, 1))
            pltpu.store(s_ref.at[pl.ds(rf, rows), cre], e_re, mask=is_fwd)
            pltpu.store(s_ref.at[pl.ds(rf, rows), cim], e_im, mask=is_fwd)
            pltpu.store(s_ref.at[pl.ds(rr, rows), cre], e_re, mask=is_rev)
            pltpu.store(s_ref.at[pl.ds(rr, rows), cim], e_im, mask=is_rev)
            last = lambda a: jnp.where(is_fwd, jnp.broadcast_to(a[rows - 1:rows], a.shape),
                                       jnp.broadcast_to(a[0:1], a.shape))
            new.append((last(a_re), last(a_im)))
        return tuple(new)

    zero = jnp.zeros((rows, LANES), F32)
    lax.fori_loop(0, n_tiles, step, tuple((zero, zero) for _ in range(pairs)))


def _s5_bidirectional(u2, mats, layer, bsz, seq_len):
    toep, b_mat, m_mat, steps, carry = mats
    g = SSM_GROUPS
    n_chunks = seq_len // SSM_CHUNK
    assert n_chunks % S5_SCAN_ROWS == 0
    width = bsz * S5_TILE
    u_spec = pl.BlockSpec((None, n_chunks, width), lambda j: (j, 0, 0))
    w_spec = pl.BlockSpec((None, None, S5_TILE, S5_TILE), lambda j: (layer, j, 0, 0))
    return pl.pallas_call(
        functools.partial(_s5_kernel, n_tiles=n_chunks // S5_SCAN_ROWS, bsz=bsz),
        grid=(g,),
        in_specs=[u_spec, w_spec, w_spec, w_spec,
                  pl.BlockSpec((None, steps.shape[1], width), lambda j: (layer, 0, j)),
                  pl.BlockSpec((None, S5_SCAN_ROWS, width), lambda j: (layer, 0, j))],
        out_specs=u_spec,
        out_shape=jax.ShapeDtypeStruct((g, n_chunks, width), BF16),
        scratch_shapes=[pltpu.VMEM((n_chunks, width), F32), pltpu.VMEM((n_chunks, width), F32)],
        compiler_params=_params("parallel"),
        name="s5",
    )(u2, b_mat, toep, m_mat, steps, carry)


MERGE_ROW_PARTS = 2


def _merge_kernel(x_ref, attn_ref, nat_ref, y_ref, gate_ref, wglu_ref, wb_ref, wout_ref, gain_ref, bias_ref, o_ref,
                  y_scr):
    tm = x_ref.shape[0]
    _from_chunk_layout(y_ref, y_scr, tm // SSM_CHUNK)
    d = D_MODEL
    part = tm // MERGE_ROW_PARTS
    for r in range(MERGE_ROW_PARTS):
        rows = slice(part * r, part * (r + 1))
        z = jax.nn.gelu(jnp.concatenate([y_scr[q, rows, :] for q in range(y_scr.shape[0])], axis=1))
        ssm = z * jax.nn.sigmoid(_dot(z.astype(BF16), wglu_ref[...]))
        merged = gate_ref[rows, 0:d].astype(F32) * _dot(attn_ref[rows, :], wb_ref[0])
        merged += gate_ref[rows, d:2 * d].astype(F32) * _dot(nat_ref[rows, :], wb_ref[1])
        merged += gate_ref[rows, 2 * d:3 * d].astype(F32) * _dot(ssm.astype(BF16), wb_ref[2])
        mix = _dot(merged.astype(BF16), wout_ref[...])
        o_ref[rows, :] = _layer_norm(DEEPNORM_ALPHA * x_ref[rows, :] + mix, gain_ref[...], bias_ref[...])


def _merge(x2d, attn_o, nat_o, y_ssm, gates, w_glu, w_branch, w_out, layer, gain, bias, seq_len, tm):
    m = x2d.shape[0]
    nl = seq_len // tm
    row = lambda width: pl.BlockSpec((tm, width), lambda i: (i, 0))
    gain, bias = gain.astype(F32)[None, :], bias.astype(F32)[None, :]
    return pl.pallas_call(
        _merge_kernel,
        grid=(m // tm,),
        in_specs=[row(D_MODEL), row(MIX_WIDTH), row(MIX_WIDTH),
                  pl.BlockSpec((SSM_GROUPS, tm // SSM_CHUNK, S5_TILE), lambda i: (0, i % nl, i // nl)),
                  row(GATE_WIDTH), _resident_layer(w_glu, layer), _resident_layer(w_branch, layer),
                  _resident_layer(w_out, layer), _resident(gain), _resident(bias)],
        out_specs=row(D_MODEL),
        out_shape=jax.ShapeDtypeStruct((m, D_MODEL), F32),
        scratch_shapes=[pltpu.VMEM((MIX_WIDTH // LANES, tm, LANES), F32)],
        compiler_params=_params("parallel"),
        name="merge",
    )(x2d, attn_o, nat_o, y_ssm, gates, w_glu, w_branch, w_out, gain, bias)


def _ffn_kernel(x_ref, wup_ref, wdown_ref, gain_ref, bias_ref, o_ref, xb_scr, acc_scr):
    f = pl.program_id(1)

    @pl.when(f == 0)
    def _():
        xb_scr[...] = x_ref[...].astype(BF16)
        acc_scr[...] = jnp.zeros_like(acc_scr)

    h = jnp.maximum(_dot(xb_scr[...], wup_ref[...]), 0.0)
    acc_scr[...] += _dot((h * h).astype(BF16), wdown_ref[...])

    @pl.when(f == pl.num_programs(1) - 1)
    def _():
        o_ref[...] = _layer_norm(DEEPNORM_ALPHA * x_ref[...] + acc_scr[...], gain_ref[...], bias_ref[...])


def _ffn(x2d, w_up, w_down, layer, gain, bias, tm, tf):
    m = x2d.shape[0]
    gain, bias = gain.astype(F32)[None, :], bias.astype(F32)[None, :]
    vec = pl.BlockSpec((1, D_MODEL), lambda i, f: (0, 0))
    return pl.pallas_call(
        _ffn_kernel,
        grid=(m // tm, FFN_DIM // tf),
        in_specs=[pl.BlockSpec((tm, D_MODEL), lambda i, f: (i, 0)),
                  pl.BlockSpec((None, D_MODEL, tf), lambda i, f: (layer, 0, f)),
                  pl.BlockSpec((None, tf, D_MODEL), lambda i, f: (layer, f, 0)),
                  vec, vec],
        out_specs=pl.BlockSpec((tm, D_MODEL), lambda i, f: (i, 0)),
        out_shape=jax.ShapeDtypeStruct((m, D_MODEL), F32),
        scratch_shapes=[pltpu.VMEM((tm, D_MODEL), BF16), pltpu.VMEM((tm, D_MODEL), F32)],
        compiler_params=_params("parallel", "arbitrary"),
        name="ffn",
    )(x2d, w_up, w_down, gain, bias)


def _tile_sizes(seq_len):
    proj_tm = min(512, seq_len)
    ffn_tm = min(1024, seq_len)
    ffn_tf = 1024
    return proj_tm, ffn_tm, ffn_tf


def kernel(x, w_in, q_norm_gain, k_norm_gain, nat_rel_bias, ssm_a_re, ssm_a_im, ssm_log_dt, ssm_b_re, ssm_b_im, ssm_c_re, ssm_c_im, ssm_d, ssm_w_glu, w_branch, w_out, ln1_gain, ln1_bias, w_ffn_up, w_ffn_down, ln2_gain, ln2_bias):
    bsz, seq_len, _ = x.shape
    proj_tm, ffn_tm, ffn_tf = _tile_sizes(seq_len)
    rope = _rope_tables(seq_len)
    w_bf = w_in.astype(BF16)
    wt = _transposed_proj_weights(w_bf)
    toep, b_mat, m_mat, steps, carry = jax.vmap(_s5_matrices)(
        ssm_a_re, ssm_a_im, ssm_log_dt, ssm_b_re, ssm_b_im, ssm_c_re, ssm_c_im, ssm_d)
    per_batch = lambda t: jnp.tile(t[:, :, :, None, :], (1, 1, 1, bsz, 1)).reshape(t.shape[0], t.shape[1], -1)
    mats = (toep, b_mat, m_mat, per_batch(steps), per_batch(carry))
    nat_bias = _nat_bias_table(nat_rel_bias)
    w_glu, w_br, w_o = ssm_w_glu.astype(BF16), w_branch.astype(BF16), w_out.astype(BF16)
    w_up, w_down = w_ffn_up.astype(BF16), w_ffn_down.astype(BF16)
    h = x.reshape(bsz * seq_len, D_MODEL)
    for layer in range(w_in.shape[0]):
        (qt, vt, nkt, k, nq, nv, u2, gates), bound = _input_projections(
            h, w_bf, wt, layer, q_norm_gain[layer], k_norm_gain[layer], rope, bsz, seq_len, proj_tm)
        attn_o = _gqa_attention(qt, k, vt, bound, bsz, seq_len)
        nat_o = _neighbourhood_attention(nq, nkt, nv, nat_bias, layer, bsz, seq_len)
        y2 = _s5_bidirectional(u2, mats, layer, bsz, seq_len)
        h = _merge(h, attn_o, nat_o, y2, gates, w_glu, w_br, w_o, layer, ln1_gain[layer], ln1_bias[layer],
                   seq_len, proj_tm)
        h = _ffn(h, w_up, w_down, layer, ln2_gain[layer], ln2_bias[layer], ffn_tm, ffn_tf)
    return h.reshape(bsz, seq_len, D_MODEL)
```

```python
import functools

import jax
import jax.numpy as jnp
from jax import lax
from jax.experimental import pallas as pl
from jax.experimental.pallas import tpu as pltpu

D_MODEL = 1024
DEPTH = 2
GRID_W = 64
HEAD_DIM = 64
MIX_WIDTH = 512
ATTN_Q_HEADS = 8
ATTN_KV_HEADS = 2
ATTN_GROUP = ATTN_Q_HEADS // ATTN_KV_HEADS
NAT_HEADS = 8
NAT_WIN_ROWS = 8
NAT_WIN_COLS = 16
SSM_GROUP = 16
SSM_GROUPS = 32
SSM_STATE = 64
SSM_CHUNK = 16
N_BRANCHES = 3
FFN_DIM = 4 * D_MODEL
ROPE_THETA = 10000.0
LN_EPS = 1e-5
RMS_EPS = 1e-6
DEEPNORM_ALPHA = (2 * DEPTH) ** 0.25
ATTN_SCALE = HEAD_DIM ** -0.5
LOG2_E = 1.4426950408889634
MASK_VALUE = -1e30

Q_WIDTH = ATTN_Q_HEADS * HEAD_DIM
KV_WIDTH = ATTN_KV_HEADS * HEAD_DIM
GATE_WIDTH = N_BRANCHES * D_MODEL

LANES = 128
SUBLANES = 8
MXU_WIDTH = 256
V7X_VMEM_BYTES = 64 * 1024 * 1024
VMEM_LIMIT = V7X_VMEM_BYTES - 8 * 1024 * 1024

F32 = jnp.float32
BF16 = jnp.bfloat16
NT_DIMS = (((1,), (1,)), ((), ()))


def _params(*semantics):
    return pltpu.CompilerParams(dimension_semantics=semantics, vmem_limit_bytes=VMEM_LIMIT)


def _dot(a, b):
    return jnp.dot(a, b, preferred_element_type=F32)


def _layer_norm(x, gain, bias):
    mu = jnp.mean(x, axis=-1, keepdims=True)
    xc = x - mu
    var = jnp.mean(xc * xc, axis=-1, keepdims=True)
    return xc * lax.rsqrt(var + LN_EPS) * gain + bias


def _resident(a):
    return pl.BlockSpec(a.shape, lambda *_: (0,) * a.ndim, pipeline_mode=pl.Buffered(1))


def _resident_layer(a, layer):
    return pl.BlockSpec((None,) + a.shape[1:], lambda *_: (layer,) + (0,) * (a.ndim - 1),
                        pipeline_mode=pl.Buffered(1))


W_IN_OFFSETS = (0, Q_WIDTH, Q_WIDTH + KV_WIDTH, Q_WIDTH + 2 * KV_WIDTH, Q_WIDTH + 2 * KV_WIDTH + MIX_WIDTH,
                Q_WIDTH + 2 * KV_WIDTH + 2 * MIX_WIDTH, Q_WIDTH + 2 * KV_WIDTH + 3 * MIX_WIDTH,
                Q_WIDTH + 2 * KV_WIDTH + 4 * MIX_WIDTH)


ATTN_KV_PAGE = 512
ATTN_TQ = MXU_WIDTH
ATTN_Q_BLOCKS_PER_STEP = 4
ATTN_EXT_DIM = LANES
ATTN_V_ROWS = HEAD_DIM + 16
ATTN_BOUND_LIMIT = 60.0
PROJ_T_ROWS = Q_WIDTH + KV_WIDTH + MIX_WIDTH
S5_TILE = SSM_CHUNK * SSM_GROUP
S5_STATE_COLS = 4 * SSM_STATE
S5_SCAN_ROWS = SUBLANES


GROUPS_PER_TILE = LANES // SSM_GROUP
CHUNKS_PER_TILE = LANES // SSM_GROUP


def _to_chunk_layout(x_scr, u_ref, n_chunks):
    lane_grp = lax.broadcasted_iota(jnp.int32, (n_chunks, LANES), 1) // SSM_GROUP
    for half in range(SSM_CHUNK // CHUNKS_PER_TILE):
        for q in range(x_scr.shape[0]):
            steps = [x_scr[q, pl.ds(CHUNKS_PER_TILE * half + tp, n_chunks, stride=SSM_CHUNK), :]
                     for tp in range(CHUNKS_PER_TILE)]
            for gm in range(GROUPS_PER_TILE):
                tile = None
                for tp in range(CHUNKS_PER_TILE):
                    shift = (SSM_GROUP * (tp - gm)) % LANES
                    moved = pltpu.roll(steps[tp], shift, 1) if shift else steps[tp]
                    tile = moved if tile is None else jnp.where(lane_grp == tp, moved, tile)
                u_ref[GROUPS_PER_TILE * q + gm, :, LANES * half:LANES * (half + 1)] = tile.astype(u_ref.dtype)


def _from_chunk_layout(y_ref, y_scr, n_chunks):
    lane_grp = lax.broadcasted_iota(jnp.int32, (n_chunks, LANES), 1) // SSM_GROUP
    for half in range(SSM_CHUNK // CHUNKS_PER_TILE):
        for q in range(y_scr.shape[0]):
            groups = [y_ref[GROUPS_PER_TILE * q + gm, :, LANES * half:LANES * (half + 1)].astype(F32)
                      for gm in range(GROUPS_PER_TILE)]
            for tp in range(CHUNKS_PER_TILE):
                tile = None
                for gm in range(GROUPS_PER_TILE):
                    shift = (SSM_GROUP * (gm - tp)) % LANES
                    moved = pltpu.roll(groups[gm], shift, 1) if shift else groups[gm]
                    tile = moved if tile is None else jnp.where(lane_grp == gm, moved, tile)
                y_scr[q, pl.ds(CHUNKS_PER_TILE * half + tp, n_chunks, stride=SSM_CHUNK), :] = tile


def _proj_kernel(x_ref, wt_ref, w_ref, qgain_ref, kgain_ref, cost_ref, sint_ref, cos_ref, sin_ref, seg_ref,
                 qext_ref, vext_ref, qt_ref, vt_ref, nkt_ref, k_ref, nq_ref, nv_ref, u_ref, g_ref, su_scr):
    tm = x_ref.shape[0]
    d = HEAD_DIM
    xb = x_ref[...].astype(BF16)

    yt = lax.dot_general(wt_ref[...], xb, NT_DIMS, preferred_element_type=F32)
    cost = cost_ref[...]
    sint = sint_ref[...]
    qgain = qgain_ref[...]
    for h in range(ATTN_Q_HEADS):
        blk = yt[d * h:d * (h + 1), :]
        ms = jnp.mean(blk * blk, axis=0, keepdims=True)
        yn = blk * lax.rsqrt(ms + RMS_EPS) * qgain
        partner = jnp.concatenate([yn[16:32], yn[0:16], yn[48:64], yn[32:48]], axis=0)
        qh = ((yn * cost + partner * sint) * (ATTN_SCALE * LOG2_E)).astype(BF16)
        j, g = divmod(h, ATTN_GROUP)
        for qb in range(tm // ATTN_TQ):
            qt_ref[j, qb, 0:d, ATTN_TQ * g:ATTN_TQ * (g + 1)] = qh[:, ATTN_TQ * qb:ATTN_TQ * (qb + 1)]
    for j in range(ATTN_KV_HEADS):
        for qb in range(tm // ATTN_TQ):
            qt_ref[j, qb, d:ATTN_EXT_DIM, :] = qext_ref[...]
    vt = yt[Q_WIDTH:Q_WIDTH + KV_WIDTH, :].astype(BF16)
    for pg in range(tm // ATTN_KV_PAGE):
        cols = slice(ATTN_KV_PAGE * pg, ATTN_KV_PAGE * (pg + 1))
        for j in range(ATTN_KV_HEADS):
            vt_ref[pg, ATTN_V_ROWS * j:ATTN_V_ROWS * j + d, :] = vt[d * j:d * (j + 1), cols]
            vt_ref[pg, ATTN_V_ROWS * j + d:ATTN_V_ROWS * (j + 1), :] = vext_ref[...]
    nkt_ref[...] = yt[Q_WIDTH + KV_WIDTH:, :].astype(BF16)

    o_ak, o_nq, o_nv, o_su, o_gate = (W_IN_OFFSETS[i] for i in (1, 3, 5, 6, 7))
    yk = _dot(xb, w_ref[:, o_ak:o_ak + KV_WIDTH])
    y2 = yk * yk
    hi = y2.astype(BF16)
    lo = (y2 - hi.astype(F32)).astype(BF16)
    ms = (_dot(hi, seg_ref[...]) + _dot(lo, seg_ref[...])) * (1.0 / d)
    kn = yk * lax.rsqrt(ms + RMS_EPS) * kgain_ref[...]
    lane = lax.broadcasted_iota(jnp.int32, (tm, LANES), 1)
    partner = jnp.where((lane % 32) < 16, pltpu.roll(kn, LANES - 16, 1), pltpu.roll(kn, 16, 1))
    kk = kn * cos_ref[...] + partner * sin_ref[...]
    one_hot = (lane == d).astype(F32)
    for j in range(ATTN_KV_HEADS):
        kj = kk if j == 0 else pltpu.roll(kk, LANES - d * j, 1)
        k_ref[j] = jnp.where(lane < d, kj, one_hot).astype(BF16)

    ynq = _dot(xb, w_ref[:, o_nq:o_nq + MIX_WIDTH]) * ATTN_SCALE
    ynv = _dot(xb, w_ref[:, o_nv:o_nv + MIX_WIDTH])
    for h in range(NAT_HEADS):
        nq_ref[h] = ynq[:, d * h:d * (h + 1)].astype(BF16)
        nv_ref[h] = ynv[:, d * h:d * (h + 1)].astype(BF16)
    su = _dot(xb, w_ref[:, o_su:o_su + MIX_WIDTH])
    for q in range(MIX_WIDTH // LANES):
        su_scr[q] = su[:, LANES * q:LANES * (q + 1)]
    _to_chunk_layout(su_scr, u_ref, tm // SSM_CHUNK)
    for n in range(N_BRANCHES):
        y = _dot(xb, w_ref[:, o_gate + D_MODEL * n:o_gate + D_MODEL * (n + 1)])
        g_ref[:, D_MODEL * n:D_MODEL * (n + 1)] = jax.nn.sigmoid(y).astype(BF16)


def _rope_tables(seq_len):
    t = jnp.arange(seq_len)
    row = (t // GRID_W).astype(F32)
    col = (t % GRID_W).astype(F32)
    axis_dim = HEAD_DIM // 2
    inv_freq = 1.0 / (ROPE_THETA ** (jnp.arange(0, axis_dim, 2, dtype=F32) / axis_dim))
    ang_r = row[:, None] * inv_freq[None, :]
    ang_c = col[:, None] * inv_freq[None, :]
    cos_head = jnp.concatenate([jnp.cos(ang_r), jnp.cos(ang_r), jnp.cos(ang_c), jnp.cos(ang_c)], axis=1)
    sin_head = jnp.concatenate([-jnp.sin(ang_r), jnp.sin(ang_r), -jnp.sin(ang_c), jnp.sin(ang_c)], axis=1)
    reps = LANES // HEAD_DIM
    return (jnp.tile(cos_head, (1, reps)), jnp.tile(sin_head, (1, reps)), cos_head.T, sin_head.T)


def _transposed_proj_weights(w_bf):
    o = W_IN_OFFSETS
    parts = [w_bf[:, :, o[0]:o[1]], w_bf[:, :, o[2]:o[3]], w_bf[:, :, o[4]:o[5]]]
    return jnp.concatenate(parts, axis=2).transpose(0, 2, 1)


def _input_projections(x2d, w_bf, wt, layer, q_gain, k_gain, rope, bsz, seq_len, tm):
    m = x2d.shape[0]
    nl = seq_len // tm
    pages = tm // ATTN_KV_PAGE
    qgain = jnp.broadcast_to(q_gain.astype(F32)[:, None], (HEAD_DIM, tm))
    kgain = jnp.tile(k_gain.astype(F32), ATTN_KV_HEADS)[None, :]
    seg = (jnp.arange(KV_WIDTH)[:, None] // HEAD_DIM == jnp.arange(KV_WIDTH)[None, :] // HEAD_DIM).astype(BF16)
    bound = (HEAD_DIM * ATTN_SCALE * LOG2_E * 1.02) * jnp.max(jnp.abs(q_gain.astype(F32))) * jnp.max(jnp.abs(k_gain.astype(F32)))
    first_row = lambda rows, width: (jnp.arange(rows)[:, None] == 0) & (jnp.arange(width)[None, :] >= 0)
    qext = jnp.where(first_row(ATTN_EXT_DIM - HEAD_DIM, ATTN_GROUP * ATTN_TQ), -bound, 0.0).astype(BF16)
    vext = first_row(ATTN_V_ROWS - HEAD_DIM, ATTN_KV_PAGE).astype(BF16)
    cos, sin, cos_t, sin_t = rope
    tok = lambda width: pl.BlockSpec((tm, width), lambda i: (i, 0))
    heads = lambda n, width=HEAD_DIM: pl.BlockSpec((n, tm, width), lambda i: (0, i, 0))
    feat_t = lambda rows: pl.BlockSpec((None, rows, tm), lambda i: (i // nl, 0, i % nl))
    outs = pl.pallas_call(
        _proj_kernel,
        grid=(m // tm,),
        in_specs=[tok(D_MODEL), _resident_layer(wt, layer), _resident_layer(w_bf, layer), _resident(qgain), _resident(kgain),
                  pl.BlockSpec((HEAD_DIM, tm), lambda i: (0, i % nl)),
                  pl.BlockSpec((HEAD_DIM, tm), lambda i: (0, i % nl)),
                  pl.BlockSpec((tm, LANES), lambda i: (i % nl, 0)),
                  pl.BlockSpec((tm, LANES), lambda i: (i % nl, 0)),
                  _resident(seg), _resident(qext), _resident(vext)],
        out_specs=[pl.BlockSpec((None, ATTN_KV_HEADS, tm // ATTN_TQ, ATTN_EXT_DIM, ATTN_GROUP * ATTN_TQ),
                                lambda i: (i // nl, 0, i % nl, 0, 0)),
                   pl.BlockSpec((None, pages, ATTN_KV_HEADS * ATTN_V_ROWS, ATTN_KV_PAGE),
                                lambda i: (i // nl, i % nl, 0, 0)),
                   feat_t(MIX_WIDTH),
                   heads(ATTN_KV_HEADS, ATTN_EXT_DIM), heads(NAT_HEADS), heads(NAT_HEADS),
                   pl.BlockSpec((SSM_GROUPS, tm // SSM_CHUNK, S5_TILE), lambda i: (0, i % nl, i // nl)),
                   tok(GATE_WIDTH)],
        out_shape=[jax.ShapeDtypeStruct((bsz, ATTN_KV_HEADS, seq_len // ATTN_TQ, ATTN_EXT_DIM, ATTN_GROUP * ATTN_TQ), BF16),
                   jax.ShapeDtypeStruct((bsz, seq_len // ATTN_KV_PAGE, ATTN_KV_HEADS * ATTN_V_ROWS, ATTN_KV_PAGE), BF16),
                   jax.ShapeDtypeStruct((bsz, MIX_WIDTH, seq_len), BF16),
                   jax.ShapeDtypeStruct((ATTN_KV_HEADS, m, ATTN_EXT_DIM), BF16),
                   jax.ShapeDtypeStruct((NAT_HEADS, m, HEAD_DIM), BF16),
                   jax.ShapeDtypeStruct((NAT_HEADS, m, HEAD_DIM), BF16),
                   jax.ShapeDtypeStruct((SSM_GROUPS, seq_len // SSM_CHUNK, bsz * S5_TILE), BF16),
                   jax.ShapeDtypeStruct((m, GATE_WIDTH), BF16)],
        scratch_shapes=[pltpu.VMEM((MIX_WIDTH // LANES, tm, LANES), F32)],
        compiler_params=_params("parallel"),
        name="in_proj",
    )(x2d, wt, w_bf, qgain, kgain, cos_t, sin_t, cos, sin, seg, qext, vext)
    return outs, bound


def _finish_attention(o_ref, qb, o):
    o_ref[ATTN_TQ * qb:ATTN_TQ * (qb + 1), :] = jnp.concatenate(
        [o[:, ATTN_TQ * g:ATTN_TQ * (g + 1)].T for g in range(ATTN_GROUP)], axis=1).astype(o_ref.dtype)


def _flash_bounded_kernel(qg_ref, k_ref, vt_ref, o_ref, *, n_pages):
    d = HEAD_DIM
    for qb in range(qg_ref.shape[0]):
        qg = qg_ref[qb]
        acc = jnp.zeros((ATTN_V_ROWS, qg.shape[1]), F32)
        for c in range(n_pages):
            s = _dot(k_ref[ATTN_KV_PAGE * c:ATTN_KV_PAGE * (c + 1), :], qg)
            acc = acc + _dot(vt_ref[c], jnp.exp2(s).astype(BF16))
        _finish_attention(o_ref, qb, acc[:d] / acc[d:d + 1])


def _flash_online_kernel(qg_ref, k_ref, vt_ref, o_ref, acc_scr, *, n_pages):
    d = HEAD_DIM
    for qb in range(qg_ref.shape[0]):
        qg = qg_ref[qb]
        nq = qg.shape[1]
        acc_scr[...] = jnp.zeros_like(acc_scr)

        def scores(c):
            return _dot(k_ref[ATTN_KV_PAGE * c:ATTN_KV_PAGE * (c + 1), :], qg)

        m = jnp.full((1, nq), -jnp.inf, F32)
        l = jnp.zeros((1, nq), F32)
        s_next = scores(0)
        for c in range(n_pages):
            s = s_next
            if c + 1 < n_pages:
                s_next = scores(c + 1)
            m_new = jnp.maximum(m, jnp.max(s, axis=0, keepdims=True))
            alpha = jnp.exp2(m - m_new)
            p = jnp.exp2(s - m_new)
            l = alpha * l + jnp.sum(p, axis=0, keepdims=True)
            acc_scr[...] = alpha * acc_scr[...] + _dot(vt_ref[c, 0:d, :], p.astype(BF16))
            m = m_new
        _finish_attention(o_ref, qb, acc_scr[...] / l)


def _gqa_attention(qg, k, vt, bound, bsz, seq_len):
    d, hk, grp, tq = HEAD_DIM, ATTN_KV_HEADS, ATTN_GROUP, ATTN_TQ
    nqb = ATTN_Q_BLOCKS_PER_STEP
    nq, n_pages = seq_len // (tq * nqb), seq_len // ATTN_KV_PAGE

    def call(body, scratch):
        return pl.pallas_call(
            functools.partial(body, n_pages=n_pages),
            grid=(bsz, hk, nq),
            in_specs=[pl.BlockSpec((None, None, nqb, ATTN_EXT_DIM, grp * tq), lambda b, j, i: (b, j, i, 0, 0)),
                      pl.BlockSpec((None, seq_len, ATTN_EXT_DIM), lambda b, j, i: (j, b, 0)),
                      pl.BlockSpec((None, n_pages, ATTN_V_ROWS, ATTN_KV_PAGE), lambda b, j, i: (b, 0, j, 0))],
            out_specs=pl.BlockSpec((tq * nqb, grp * d), lambda b, j, i: (b * nq + i, j)),
            out_shape=jax.ShapeDtypeStruct((bsz * seq_len, ATTN_Q_HEADS * d), BF16),
            scratch_shapes=scratch,
            compiler_params=_params("parallel", "parallel", "parallel"),
            name=body.__name__.strip("_").replace("_kernel", ""),
        )(qg, k, vt)

    return lax.cond(bound <= ATTN_BOUND_LIMIT,
                    lambda: call(_flash_bounded_kernel, []),
                    lambda: call(_flash_online_kernel, [pltpu.VMEM((d, grp * tq), F32)]))


NAT_ROWS_PER_STEP = 8
NAT_HEAD_UNROLL = 8
NAT_STEP_TOKENS = NAT_ROWS_PER_STEP * GRID_W
NAT_WIN_TOKENS = NAT_WIN_ROWS * GRID_W


def _nat_kernel(q_ref, ktp_ref, ktc_ref, ktn_ref, vp_ref, vc_ref, vn_ref, bias_ref, o_ref, kt_scr, v_scr, oh_scr):
    g = pl.program_id(1)
    ng = pl.num_programs(1)
    st = NAT_STEP_TOKENS
    d = HEAD_DIM
    kt_scr[:, 0:st] = ktp_ref[...]
    kt_scr[:, st:2 * st] = ktc_ref[...]
    kt_scr[:, 2 * st:3 * st] = ktn_ref[...]
    v_scr[:, 0:st, :] = vp_ref[...]
    v_scr[:, st:2 * st, :] = vc_ref[...]
    v_scr[:, 2 * st:3 * st, :] = vn_ref[...]

    def all_heads(frame_rows, bias_index):
        starts = [fr * GRID_W for fr in frame_rows]

        def head(h, carry):
            q = q_ref[h]
            r0 = pl.multiple_of(h * d, d)
            s = jnp.concatenate(
                [_dot(q[GRID_W * r:GRID_W * (r + 1), :],
                      kt_scr[pl.ds(r0, d), starts[r]:starts[r] + NAT_WIN_TOKENS]) + bias_ref[bias_index[r], h]
                 for r in range(NAT_ROWS_PER_STEP)], axis=0)
            m = jnp.max(s, axis=1, keepdims=True)
            p = jnp.exp(s - m)
            l = jnp.sum(p, axis=1, keepdims=True)
            pb = p.astype(BF16)
            o = jnp.concatenate(
                [_dot(pb[GRID_W * r:GRID_W * (r + 1), :], v_scr[h, starts[r]:starts[r] + NAT_WIN_TOKENS, :])
                 for r in range(NAT_ROWS_PER_STEP)], axis=0)
            oh_scr[h] = o / l
            return carry

        lax.fori_loop(0, NAT_HEADS, head, 0, unroll=NAT_HEAD_UNROLL)

    half = NAT_WIN_ROWS // 2
    n = NAT_ROWS_PER_STEP

    @pl.when(g == 0)
    def _():
        all_heads([n + max(r - half, 0) for r in range(n)], [min(r, half) for r in range(n)])

    @pl.when(jnp.logical_and(g > 0, g < ng - 1))
    def _():
        all_heads([r + half for r in range(n)], [half] * n)

    @pl.when(jnp.logical_and(g == ng - 1, g > 0))
    def _():
        all_heads([min(r + half, n) for r in range(n)], [max(r, half) for r in range(n)])

    o_ref[...] = jnp.concatenate([oh_scr[h] for h in range(NAT_HEADS)], axis=1).astype(o_ref.dtype)


def _nat_bias_kernel(rows_ref, mask_ref, o_ref):
    in_window = mask_ref[...] > 0.5
    for v in range(NAT_WIN_ROWS):
        parts = [pltpu.roll(jnp.broadcast_to(rows_ref[v, p:p + 1, :], (GRID_W, LANES)),
                            LANES - (NAT_WIN_COLS - 1), 1, stride=1, stride_axis=0)
                 for p in range(NAT_WIN_ROWS // 2)]
        o_ref[v] = jnp.where(in_window, jnp.concatenate(parts, axis=1), MASK_VALUE)


def _nat_bias_table(rel_bias):
    layers = rel_bias.shape[0]
    w, nb = GRID_W, 2 * NAT_WIN_COLS - 1
    cols = jnp.arange(w)
    col_start = jnp.clip(cols - NAT_WIN_COLS // 2, 0, w - NAT_WIN_COLS)
    in_win = (cols[None, :] >= col_start[:, None]) & (cols[None, :] < col_start[:, None] + NAT_WIN_COLS)
    mask = jnp.tile(in_win, (1, NAT_WIN_ROWS)).astype(F32)
    v = jnp.arange(NAT_WIN_ROWS)
    row_off = v[None, :] - v[:, None] + (NAT_WIN_ROWS - 1)
    row_hot = (row_off[:, :, None] == jnp.arange(2 * NAT_WIN_ROWS - 1)[None, None, :]).astype(F32)
    by_row = jnp.sum(rel_bias.astype(F32)[:, :, None, None] * row_hot[None, None, :, :, :, None], axis=4)
    pairs = by_row.reshape(layers, NAT_HEADS, NAT_WIN_ROWS, NAT_WIN_ROWS // 2, 2, nb)
    packed = jnp.pad(pairs, ((0, 0),) * 5 + ((0, w - nb),)).reshape(layers, NAT_HEADS, NAT_WIN_ROWS, NAT_WIN_ROWS // 2, LANES)
    return pl.pallas_call(
        _nat_bias_kernel,
        grid=(layers, NAT_HEADS),
        in_specs=[pl.BlockSpec((None, None, NAT_WIN_ROWS, NAT_WIN_ROWS // 2, LANES), lambda l, h: (l, h, 0, 0, 0)),
                  pl.BlockSpec(mask.shape, lambda l, h: (0, 0))],
        out_specs=pl.BlockSpec((None, NAT_WIN_ROWS, None, w, NAT_WIN_TOKENS), lambda l, h: (l, 0, h, 0, 0)),
        out_shape=jax.ShapeDtypeStruct((layers, NAT_WIN_ROWS, NAT_HEADS, w, NAT_WIN_TOKENS), F32),
        compiler_params=_params("parallel", "parallel"),
        name="nat_bias",
    )(packed, mask)


def _neighbourhood_attention(nq, nkt, nv, bias, layer, bsz, seq_len):
    h, d, st = NAT_HEADS, HEAD_DIM, NAT_STEP_TOKENS
    ng = seq_len // st
    assert ng >= 2 and seq_len // GRID_W >= NAT_WIN_ROWS
    prev = lambda g: jnp.maximum(g - 1, 0)
    nxt = lambda g: jnp.minimum(g + 1, ng - 1)
    same = lambda g: g
    kt_spec = lambda f: pl.BlockSpec((None, h * d, st), lambda b, g: (b, 0, f(g)))
    v_spec = lambda f: pl.BlockSpec((h, st, d), lambda b, g: (0, b * ng + f(g), 0))
    return pl.pallas_call(
        _nat_kernel,
        grid=(bsz, ng),
        in_specs=[v_spec(same), kt_spec(prev), kt_spec(same), kt_spec(nxt),
                  v_spec(prev), v_spec(same), v_spec(nxt), _resident_layer(bias, layer)],
        out_specs=pl.BlockSpec((st, h * d), lambda b, g: (b * ng + g, 0)),
        out_shape=jax.ShapeDtypeStruct((bsz * seq_len, h * d), BF16),
        scratch_shapes=[pltpu.VMEM((h * d, 3 * st), BF16), pltpu.VMEM((h, 3 * st, d), BF16),
                        pltpu.VMEM((h, st, d), F32)],
        compiler_params=_params("parallel", "parallel"),
        name="nat",
    )(nq, nkt, nkt, nkt, nv, nv, nv, bias)


def _s5_matrices(a_re, a_im, log_dt, b_re, b_im, c_re, c_im, d_skip):
    t_len, hs = SSM_CHUNK, SSM_GROUP
    a_re = a_re.astype(F32)
    a_im = a_im.astype(F32)
    dt = jnp.exp(log_dt.astype(F32))[..., None]
    decay = jnp.exp(a_re * dt)
    phase = a_im * dt
    lam_re = decay * jnp.cos(phase)
    lam_im = decay * jnp.sin(phase)
    den = a_re * a_re + a_im * a_im
    num_re = lam_re - 1.0
    coef_re = (num_re * a_re + lam_im * a_im) / den
    coef_im = (lam_im * a_re - num_re * a_im) / den
    b_re = b_re.astype(F32)[None]
    b_im = b_im.astype(F32)[None]
    bbar_re = coef_re[..., None] * b_re - coef_im[..., None] * b_im
    bbar_im = coef_re[..., None] * b_im + coef_im[..., None] * b_re
    c_re = c_re.astype(F32)
    c_im = c_im.astype(F32)

    def powers(exponents):
        e = jnp.asarray(exponents, F32)[:, None, None, None]
        mag = jnp.exp(e * (a_re * dt)[None])
        return mag * jnp.cos(e * phase[None]), mag * jnp.sin(e * phase[None])

    def c_times(p_re, p_im):
        return (c_re[None] * p_re[:, :, :, None, :] - c_im[None] * p_im[:, :, :, None, :],
                c_re[None] * p_im[:, :, :, None, :] + c_im[None] * p_re[:, :, :, None, :])

    def times_bbar(p_re, p_im):
        return (p_re[..., None] * bbar_re[None] - p_im[..., None] * bbar_im[None],
                p_re[..., None] * bbar_im[None] + p_im[..., None] * bbar_re[None])

    tk = jnp.arange(t_len)
    cl_re, cl_im = c_times(*powers(tk))
    taps = jnp.sum(cl_re[..., None] * bbar_re[None, :, :, None] - cl_im[..., None] * bbar_im[None, :, :, None], axis=4)
    lag = tk[None, :] - tk[:, None]
    hot_f = (lag[:, :, None] == tk[None, None, :]).astype(F32)
    hot_r = (-lag[:, :, None] == tk[None, None, :]).astype(F32)
    eye_t = jnp.eye(t_len, dtype=F32)[:, :, None, None, None]
    skip = eye_t * (jnp.eye(hs, dtype=F32)[None] * d_skip.astype(F32)[:, :, None])[None, None]
    hp = lax.Precision.HIGHEST
    toep = (jnp.einsum('ktx,xgoi->ktgoi', hot_f, taps[:, 0], precision=hp)
            + jnp.einsum('ktx,xgoi->ktgoi', hot_r, taps[:, 1], precision=hp) + skip)
    toep = toep.transpose(2, 0, 4, 1, 3).reshape(SSM_GROUPS, S5_TILE, S5_TILE)

    to_in = lambda m: m.transpose(1, 0, 3, 2).reshape(SSM_GROUPS, S5_TILE, SSM_STATE)
    f_re, f_im = times_bbar(*powers(t_len - 1 - tk))
    r_re, r_im = times_bbar(*powers(tk))
    b_mat = jnp.concatenate([to_in(f_re[:, 0]), to_in(r_re[:, 1]), to_in(f_im[:, 0]), to_in(r_im[:, 1])], axis=2)

    to_out = lambda m: m.transpose(1, 3, 0, 2).reshape(SSM_GROUPS, SSM_STATE, S5_TILE)
    of_re, of_im = c_times(*powers(tk + 1))
    or_re, or_im = c_times(*powers(t_len - tk))
    m_mat = jnp.concatenate([to_out(of_re[:, 0]), to_out(or_re[:, 1]), to_out(-of_im[:, 0]), to_out(-or_im[:, 1])],
                            axis=1)

    def chunk_powers(n_fwd, n_rev):
        (fr, fi), (rr, ri) = powers(n_fwd * t_len), powers(n_rev * t_len)
        return jnp.concatenate([fr[:, 0], rr[:, 1], fi[:, 0], ri[:, 1]], axis=-1)
    doubling = jnp.array([1, 2, 4])
    steps = chunk_powers(doubling, doubling)
    rows = jnp.arange(S5_SCAN_ROWS)
    carry = chunk_powers(rows + 1, S5_SCAN_ROWS - rows)
    return toep.astype(BF16), b_mat.astype(BF16), m_mat.astype(BF16), steps, carry


def _s5_kernel(u_ref, b_ref, t_ref, m_ref, step_ref, carry_ref, y_ref, z_scr, s_scr, *, n_tiles, bsz):
    for b in range(bsz):
        cols = slice(S5_TILE * b, S5_TILE * (b + 1))
        z_scr[:, cols] = _dot(u_ref[:, cols], b_ref[...])
    _s5_chunk_scan(z_scr, step_ref, carry_ref, s_scr, n_tiles=n_tiles, pairs=bsz)
    for b in range(bsz):
        cols = slice(S5_TILE * b, S5_TILE * (b + 1))
        y = _dot(u_ref[:, cols], t_ref[...]) + _dot(s_scr[:, cols].astype(BF16), m_ref[...])
        y_ref[:, cols] = y.astype(y_ref.dtype)


def _s5_chunk_scan(z_ref, step_ref, carry_ref, s_ref, *, n_tiles, pairs):
    rows = S5_SCAN_ROWS
    lane = lax.broadcasted_iota(jnp.int32, (rows, LANES), 1)
    sub = lax.broadcasted_iota(jnp.int32, (rows, LANES), 0)
    is_fwd = lane < SSM_STATE
    is_rev = jnp.logical_not(is_fwd)
    both = lambda fwd_rows, rev_rows: jnp.logical_or(jnp.logical_and(is_fwd, fwd_rows),
                                                     jnp.logical_and(is_rev, rev_rows))
    edge = both(sub == 0, sub == rows - 1)

    def upstream(x, dist):
        valid = both(sub >= dist, sub < rows - dist)
        return jnp.where(valid, jnp.where(is_fwd, pltpu.roll(x, dist, 0), pltpu.roll(x, rows - dist, 0)), 0.0)

    def step(k, carry):
        rf = pl.multiple_of(k * rows, rows)
        rr = pl.multiple_of((n_tiles - 1 - k) * rows, rows)
        new = []
        for j in range(pairs):
            cre = slice(S5_STATE_COLS * j, S5_STATE_COLS * j + LANES)
            cim = slice(S5_STATE_COLS * j + LANES, S5_STATE_COLS * (j + 1))
            c_re, c_im = carry[j]
            x_re = jnp.where(is_fwd, z_ref[pl.ds(rf, rows), cre], z_ref[pl.ds(rr, rows), cre])
            x_im = jnp.where(is_fwd, z_ref[pl.ds(rf, rows), cim], z_ref[pl.ds(rr, rows), cim])
            for i, dist in enumerate((1, 2, 4)):
                lr = step_ref[i:i + 1, cre]
                li = step_ref[i:i + 1, cim]
                u_re, u_im = upstream(x_re, dist), upstream(x_im, dist)
                x_re, x_im = x_re + lr * u_re - li * u_im, x_im + lr * u_im + li * u_re
            pr, pi = carry_ref[:, cre], carry_ref[:, cim]
            a_re = x_re + pr * c_re - pi * c_im
            a_im = x_im + pr * c_im + pi * c_re
            e_re = jnp.where(edge, c_re, upstream(a_re, 1))
            e_im = jnp.where(edge, c_im, upstream(a_im, 1))
            pltpu.store(s_ref.at[pl.ds(rf, rows), cre], e_re, mask=is_fwd)
            pltpu.store(s_ref.at[pl.ds(rf, rows), cim], e_im, mask=is_fwd)
            pltpu.store(s_ref.at[pl.ds(rr, rows), cre], e_re, mask=is_rev)
            pltpu.store(s_ref.at[pl.ds(rr, rows), cim], e_im, mask=is_rev)
            last = lambda a: jnp.where(is_fwd, jnp.broadcast_to(a[rows - 1:rows], a.shape),
                                       jnp.broadcast_to(a[0:1], a.shape))
            new.append((last(a_re), last(a_im)))
        return tuple(new)

    zero = jnp.zeros((rows, LANES), F32)
    lax.fori_loop(0, n_tiles, step, tuple((zero, zero) for _ in range(pairs)))


def _s5_bidirectional(u2, mats, layer, bsz, seq_len):
    toep, b_mat, m_mat, steps, carry = mats
    g = SSM_GROUPS
    n_chunks = seq_len // SSM_CHUNK
    assert n_chunks % S5_SCAN_ROWS == 0
    width = bsz * S5_TILE
    u_spec = pl.BlockSpec((None, n_chunks, width), lambda j: (j, 0, 0))
    w_spec = pl.BlockSpec((None, None, S5_TILE, S5_TILE), lambda j: (layer, j, 0, 0))
    return pl.pallas_call(
        functools.partial(_s5_kernel, n_tiles=n_chunks // S5_SCAN_ROWS, bsz=bsz),
        grid=(g,),
        in_specs=[u_spec, w_spec, w_spec, w_spec,
                  pl.BlockSpec((None, steps.shape[1], width), lambda j: (layer, 0, j)),
                  pl.BlockSpec((None, S5_SCAN_ROWS, width), lambda j: (layer, 0, j))],
        out_specs=u_spec,
        out_shape=jax.ShapeDtypeStruct((g, n_chunks, width), BF16),
        scratch_shapes=[pltpu.VMEM((n_chunks, width), F32), pltpu.VMEM((n_chunks, width), F32)],
        compiler_params=_params("parallel"),
        name="s5",
    )(u2, b_mat, toep, m_mat, steps, carry)


MERGE_ROW_PARTS = 4


def _merge_kernel(x_ref, attn_ref, nat_ref, y_ref, gate_ref, wglu_ref, wb_ref, wout_ref, gain_ref, bias_ref, o_ref,
                  y_scr):
    tm = x_ref.shape[0]
    _from_chunk_layout(y_ref, y_scr, tm // SSM_CHUNK)
    d = D_MODEL
    part = tm // MERGE_ROW_PARTS
    for r in range(MERGE_ROW_PARTS):
        rows = slice(part * r, part * (r + 1))
        z = jax.nn.gelu(jnp.concatenate([y_scr[q, rows, :] for q in range(y_scr.shape[0])], axis=1))
        ssm = z * jax.nn.sigmoid(_dot(z.astype(BF16), wglu_ref[...]))
        merged = gate_ref[rows, 0:d].astype(F32) * _dot(attn_ref[rows, :], wb_ref[0])
        merged += gate_ref[rows, d:2 * d].astype(F32) * _dot(nat_ref[rows, :], wb_ref[1])
        merged += gate_ref[rows, 2 * d:3 * d].astype(F32) * _dot(ssm.astype(BF16), wb_ref[2])
        mix = _dot(merged.astype(BF16), wout_ref[...])
        o_ref[rows, :] = _layer_norm(DEEPNORM_ALPHA * x_ref[rows, :] + mix, gain_ref[...], bias_ref[...])


def _merge(x2d, attn_o, nat_o, y_ssm, gates, w_glu, w_branch, w_out, layer, gain, bias, seq_len, tm):
    m = x2d.shape[0]
    nl = seq_len // tm
    row = lambda width: pl.BlockSpec((tm, width), lambda i: (i, 0))
    gain, bias = gain.astype(F32)[None, :], bias.astype(F32)[None, :]
    return pl.pallas_call(
        _merge_kernel,
        grid=(m // tm,),
        in_specs=[row(D_MODEL), row(MIX_WIDTH), row(MIX_WIDTH),
                  pl.BlockSpec((SSM_GROUPS, tm // SSM_CHUNK, S5_TILE), lambda i: (0, i % nl, i // nl)),
                  row(GATE_WIDTH), _resident_layer(w_glu, layer), _resident_layer(w_branch, layer),
                  _resident_layer(w_out, layer), _resident(gain), _resident(bias)],
        out_specs=row(D_MODEL),
        out_shape=jax.ShapeDtypeStruct((m, D_MODEL), F32),
        scratch_shapes=[pltpu.VMEM((MIX_WIDTH // LANES, tm, LANES), F32)],
        compiler_params=_params("parallel"),
        name="merge",
    )(x2d, attn_o, nat_o, y_ssm, gates, w_glu, w_branch, w_out, gain, bias)


def _ffn_kernel(x_ref, wup_ref, wdown_ref, gain_ref, bias_ref, o_ref, xb_scr, acc_scr):
    f = pl.program_id(1)

    @pl.when(f == 0)
    def _():
        xb_scr[...] = x_ref[...].astype(BF16)
        acc_scr[...] = jnp.zeros_like(acc_scr)

    h = jnp.maximum(_dot(xb_scr[...], wup_ref[...]), 0.0)
    acc_scr[...] += _dot((h * h).astype(BF16), wdown_ref[...])

    @pl.when(f == pl.num_programs(1) - 1)
    def _():
        o_ref[...] = _layer_norm(DEEPNORM_ALPHA * x_ref[...] + acc_scr[...], gain_ref[...], bias_ref[...])


def _ffn(x2d, w_up, w_down, layer, gain, bias, tm, tf):
    m = x2d.shape[0]
    gain, bias = gain.astype(F32)[None, :], bias.astype(F32)[None, :]
    vec = pl.BlockSpec((1, D_MODEL), lambda i, f: (0, 0))
    return pl.pallas_call(
        _ffn_kernel,
        grid=(m // tm, FFN_DIM // tf),
        in_specs=[pl.BlockSpec((tm, D_MODEL), lambda i, f: (i, 0)),
                  pl.BlockSpec((None, D_MODEL, tf), lambda i, f: (layer, 0, f)),
                  pl.BlockSpec((None, tf, D_MODEL), lambda i, f: (layer, f, 0)),
                  vec, vec],
        out_specs=pl.BlockSpec((tm, D_MODEL), lambda i, f: (i, 0)),
        out_shape=jax.ShapeDtypeStruct((m, D_MODEL), F32),
        scratch_shapes=[pltpu.VMEM((tm, D_MODEL), BF16), pltpu.VMEM((tm, D_MODEL), F32)],
        compiler_params=_params("parallel", "arbitrary"),
        name="ffn",
    )(x2d, w_up, w_down, gain, bias)


def _tile_sizes(seq_len):
    proj_tm = min(512, seq_len)
    ffn_tm = min(1024, seq_len)
    ffn_tf = 1024
    return proj_tm, ffn_tm, ffn_tf


def kernel(x, w_in, q_norm_gain, k_norm_gain, nat_rel_bias, ssm_a_re, ssm_a_im, ssm_log_dt, ssm_b_re, ssm_b_im, ssm_c_re, ssm_c_im, ssm_d, ssm_w_glu, w_branch, w_out, ln1_gain, ln1_bias, w_ffn_up, w_ffn_down, ln2_gain, ln2_bias):
    bsz, seq_len, _ = x.shape
    proj_tm, ffn_tm, ffn_tf = _tile_sizes(seq_len)
    rope = _rope_tables(seq_len)
    w_bf = w_in.astype(BF16)
    wt = _transposed_proj_weights(w_bf)
    toep, b_mat, m_mat, steps, carry = jax.vmap(_s5_matrices)(
        ssm_a_re, ssm_a_im, ssm_log_dt, ssm_b_re, ssm_b_im, ssm_c_re, ssm_c_im, ssm_d)
    per_batch = lambda t: jnp.tile(t[:, :, :, None, :], (1, 1, 1, bsz, 1)).reshape(t.shape[0], t.shape[1], -1)
    mats = (toep, b_mat, m_mat, per_batch(steps), per_batch(carry))
    nat_bias = _nat_bias_table(nat_rel_bias)
    w_glu, w_br, w_o = ssm_w_glu.astype(BF16), w_branch.astype(BF16), w_out.astype(BF16)
    w_up, w_down = w_ffn_up.astype(BF16), w_ffn_down.astype(BF16)
    h = x.reshape(bsz * seq_len, D_MODEL)
    for layer in range(w_in.shape[0]):
        (qt, vt, nkt, k, nq, nv, u2, gates), bound = _input_projections(
            h, w_bf, wt, layer, q_norm_gain[layer], k_norm_gain[layer], rope, bsz, seq_len, proj_tm)
        attn_o = _gqa_attention(qt, k, vt, bound, bsz, seq_len)
        nat_o = _neighbourhood_attention(nq, nkt, nv, nat_bias, layer, bsz, seq_len)
        y2 = _s5_bidirectional(u2, mats, layer, bsz, seq_len)
        h = _merge(h, attn_o, nat_o, y2, gates, w_glu, w_br, w_o, layer, ln1_gain[layer], ln1_bias[layer],
                   seq_len, ffn_tm)
        h = _ffn(h, w_up, w_down, layer, ln2_gain[layer], ln2_bias[layer], ffn_tm, ffn_tf)
    return h.reshape(bsz, seq_len, D_MODEL)
```
